```python
import math
import jax, jax.numpy as jnp
from jax import lax
import numpy as np

D_MODEL = 1024
BATCH = 16
SEQ = 4096
DEPTH = 1
DEC_BATCH = 8
DEC_SEQ = 32
PAST_LEN = 1024

CHUNK = 64
WINDOW = 128
WINDOW_CHUNKS = WINDOW // CHUNK
ATTN_WIDTH = D_MODEL // 2
SSM_WIDTH = D_MODEL - ATTN_WIDTH
HEAD_DIM = 64
N_HEADS = ATTN_WIDTH // HEAD_DIM
N_KV_HEADS = 2
GQA_REP = N_HEADS // N_KV_HEADS
SSM_CH = 16
SSM_GROUPS = SSM_WIDTH // SSM_CH
SSM_STATE = 64
D_FF = -(-8 * D_MODEL // (3 * 256)) * 256
ROPE_THETA = 10000.0
EPS = 1e-6
Q_COLS = N_HEADS * HEAD_DIM
KV_COLS = N_KV_HEADS * HEAD_DIM
IN_COLS = Q_COLS + 2 * KV_COLS + SSM_WIDTH

kernel_name = "hybrid_streaming_s5_swa_step"


def _rmsnorm(x, g):
    xf = x.astype(jnp.float32)
    y = xf * lax.rsqrt(jnp.mean(xf * xf, axis=-1, keepdims=True) + EPS)
    return (y * g.astype(jnp.float32)).astype(x.dtype)


def _rope(x, pos):
    half = HEAD_DIM // 2
    inv = ROPE_THETA ** (-jnp.arange(half, dtype=jnp.float32) * 2.0 / HEAD_DIM)
    ang = pos.astype(jnp.float32)[:, None] * inv[None, :]
    cos = jnp.cos(ang)[None, :, None, :]
    sin = jnp.sin(ang)[None, :, None, :]
    xf = x.astype(jnp.float32)
    x1, x2 = xf[..., :half], xf[..., half:]
    return jnp.concatenate([x1 * cos - x2 * sin, x2 * cos + x1 * sin], axis=-1).astype(x.dtype)


def _sink_attention(q, k, v, valid, sinks):
    s = jnp.einsum('bnqgrd,bnkgd->bngrqk', q.astype(jnp.float32), k.astype(jnp.float32)) * (HEAD_DIM ** -0.5)
    s = jnp.where(valid[None, :, None, None, None, :], s, -jnp.inf)
    sink = sinks.astype(jnp.float32).reshape(N_KV_HEADS, GQA_REP)[None, None, :, :, None]
    m = jnp.maximum(jnp.max(s, axis=-1), sink)
    p = jnp.exp(s - m[..., None])
    p = p / (jnp.sum(p, axis=-1) + jnp.exp(sink - m))[..., None]
    o = jnp.einsum('bngrqk,bnkgd->bnqgrd', p, v.astype(jnp.float32))
    return o.astype(q.dtype)


def _swa_prompt(q, k, v, sinks):
    bsz, s = q.shape[0], q.shape[1]
    nc = s // CHUNK
    qb = q.reshape(bsz, nc, CHUNK, N_KV_HEADS, GQA_REP, HEAD_DIM)

    def band(t):
        tp = jnp.pad(t, ((0, 0), (WINDOW, 0), (0, 0), (0, 0)))
        tp = tp.reshape(bsz, nc + WINDOW_CHUNKS, CHUNK, N_KV_HEADS, HEAD_DIM)
        return jnp.concatenate([tp[:, j:j + nc] for j in range(WINDOW_CHUNKS + 1)], axis=2)

    key_chunk = (jnp.arange(nc)[:, None]
                 + jnp.repeat(jnp.arange(WINDOW_CHUNKS + 1), CHUNK)[None, :] - WINDOW_CHUNKS)
    o = _sink_attention(qb, band(k), band(v), key_chunk >= 0, sinks)
    return o.reshape(bsz, s, ATTN_WIDTH)


def _swa_sample(q, k, v, past_k, past_v, sinks):
    bsz, s = q.shape[0], q.shape[1]
    qb = q.reshape(bsz, 1, s, N_KV_HEADS, GQA_REP, HEAD_DIM)
    kb = jnp.concatenate([past_k.astype(k.dtype), k], axis=1)[:, None]
    vb = jnp.concatenate([past_v.astype(v.dtype), v], axis=1)[:, None]
    valid = jnp.ones((1, kb.shape[2]), dtype=bool)
    o = _sink_attention(qb, kb, vb, valid, sinks)
    return o.reshape(bsz, s, ATTN_WIDTH)


def _ssm_combine(left, right):
    a1, b1 = left
    a2, b2 = right
    return a1 * a2, a2 * b1 + b2


def _s5(u, h0, a_re, a_im, log_dt, b_re, b_im, c_re, c_im, d_skip):
    bsz, s = u.shape[0], u.shape[1]
    f32 = jnp.float32
    uf = u.astype(f32).reshape(bsz, s, SSM_GROUPS, SSM_CH)
    lam = lax.complex(a_re.astype(f32), a_im.astype(f32))
    dt = jnp.exp(log_dt.astype(f32))[:, None]
    lam_bar = jnp.exp(lam * dt)
    b_bar = ((lam_bar - 1.0) / lam)[:, :, None] * lax.complex(b_re.astype(f32), b_im.astype(f32))
    bu = jnp.einsum('gph,bsgh->bsgp', b_bar, uf.astype(jnp.complex64))
    a = jnp.broadcast_to(lam_bar[None, None], (1, s, SSM_GROUPS, SSM_STATE))
    a_cum, xs = lax.associative_scan(_ssm_combine, (a, bu), axis=1)
    if h0 is not None:
        xs = xs + a_cum * h0[:, None]
    c = lax.complex(c_re.astype(f32), c_im.astype(f32))
    y = jnp.einsum('ghp,bsgp->bsgh', c, xs).real + d_skip.astype(f32).reshape(SSM_GROUPS, SSM_CH) * uf
    return y.reshape(bsz, s, SSM_WIDTH), xs[:, -1]


def _layer(x, c, pos, past_k, past_v, h0, lw):
    bsz, s = x.shape[0], x.shape[1]
    mod = jax.nn.silu(c) @ lw['w_ada'] + lw['b_ada']
    sh1, sc1, g1, sh2, sc2, g2 = jnp.split(mod[:, None, :], 6, axis=-1)

    h = _rmsnorm(x, lw['ln1_g']) * (1.0 + sc1) + sh1
    proj = h @ lw['w_in']
    q, k, v, u = jnp.split(proj, [Q_COLS, Q_COLS + KV_COLS, Q_COLS + 2 * KV_COLS], axis=-1)
    q = _rope(_rmsnorm(q.reshape(bsz, s, N_HEADS, HEAD_DIM), lw['q_norm_g']), pos)
    k = _rope(_rmsnorm(k.reshape(bsz, s, N_KV_HEADS, HEAD_DIM), lw['k_norm_g']), pos)
    v = v.reshape(bsz, s, N_KV_HEADS, HEAD_DIM)
    if past_k is None:
        attn = _swa_prompt(q, k, v, lw['attn_sinks'])
        k_rows, v_rows = k[:, -WINDOW:], v[:, -WINDOW:]
    else:
        attn = _swa_sample(q, k, v, past_k, past_v, lw['attn_sinks'])
        k_rows, v_rows = k, v

    ssm_y, h_last = _s5(u, h0, lw['ssm_A_re'], lw['ssm_A_im'], lw['ssm_log_dt'], lw['ssm_B_re'],
                        lw['ssm_B_im'], lw['ssm_C_re'], lw['ssm_C_im'], lw['ssm_D'])
    g = jax.nn.gelu(ssm_y)
    ssm_o = (g * jax.nn.sigmoid(g @ lw['ssm_glu_w'].astype(jnp.float32) + lw['ssm_glu_b'].astype(jnp.float32))).astype(x.dtype)

    merged = jnp.concatenate([_rmsnorm(attn, lw['attn_out_g']), _rmsnorm(ssm_o, lw['ssm_out_g'])], axis=-1)
    x = x + g1 * (merged @ lw['w_out'])

    h2 = _rmsnorm(x, lw['ln2_g']) * (1.0 + sc2) + sh2
    ff = (jax.nn.silu(h2 @ lw['w_gate']) * (h2 @ lw['w_up'])) @ lw['w_down']
    x = x + g2 * ff
    state = jnp.stack([h_last.real, h_last.imag], axis=-1)
    return x, k_rows, v_rows, state


def setup_inputs(seed: int = 0) -> dict:
    key = jax.random.key(seed)
    ks = iter(jax.random.split(key, 40))
    f32 = jnp.float32
    nrm = lambda shape, scale: jax.random.normal(next(ks), shape, f32) * scale
    gain = lambda shape: 1.0 + nrm(shape, 0.02)
    n_idx = jnp.arange(SSM_STATE, dtype=f32)
    return {
        "x_prompt": nrm((BATCH, SEQ, D_MODEL), 1.0),
        "x_sample": nrm((DEC_BATCH, DEC_SEQ, D_MODEL), 1.0),
        "cache_k": nrm((DEPTH, DEC_BATCH, WINDOW, N_KV_HEADS, HEAD_DIM), 1.0),
        "cache_v": nrm((DEPTH, DEC_BATCH, WINDOW, N_KV_HEADS, HEAD_DIM), 1.0),
        "state_ssm": nrm((DEPTH, DEC_BATCH, SSM_GROUPS, SSM_STATE, 2), 0.5),
        "c_prompt": nrm((BATCH, D_MODEL), 1.0),
        "c_sample": nrm((DEC_BATCH, D_MODEL), 1.0),
        "w_ada": nrm((DEPTH, D_MODEL, 6 * D_MODEL), 0.5 * D_MODEL ** -0.5),
        "b_ada": nrm((DEPTH, 6 * D_MODEL), 0.02),
        "ln1_g": gain((DEPTH, D_MODEL)),
        "w_in": nrm((DEPTH, D_MODEL, IN_COLS), D_MODEL ** -0.5),
        "q_norm_g": gain((DEPTH, HEAD_DIM)),
        "k_norm_g": gain((DEPTH, HEAD_DIM)),
        "attn_sinks": nrm((DEPTH, N_HEADS), 0.5),
        "ssm_A_re": -0.5 + nrm((DEPTH, SSM_GROUPS, SSM_STATE), 0.01),
        "ssm_A_im": jnp.pi * n_idx + nrm((DEPTH, SSM_GROUPS, SSM_STATE), 0.01),
        "ssm_log_dt": jax.random.uniform(next(ks), (DEPTH, SSM_GROUPS), f32, math.log(1e-3), math.log(1e-1)),
        "ssm_B_re": nrm((DEPTH, SSM_GROUPS, SSM_STATE, SSM_CH), (2 * SSM_CH) ** -0.5),
        "ssm_B_im": nrm((DEPTH, SSM_GROUPS, SSM_STATE, SSM_CH), (2 * SSM_CH) ** -0.5),
        "ssm_C_re": nrm((DEPTH, SSM_GROUPS, SSM_CH, SSM_STATE), SSM_STATE ** -0.5),
        "ssm_C_im": nrm((DEPTH, SSM_GROUPS, SSM_CH, SSM_STATE), SSM_STATE ** -0.5),
        "ssm_D": nrm((DEPTH, SSM_WIDTH), 1.0),
        "ssm_glu_w": nrm((DEPTH, SSM_WIDTH, SSM_WIDTH), SSM_WIDTH ** -0.5),
        "ssm_glu_b": nrm((DEPTH, SSM_WIDTH), 0.02),
        "attn_out_g": gain((DEPTH, ATTN_WIDTH)),
        "ssm_out_g": gain((DEPTH, SSM_WIDTH)),
        "w_out": nrm((DEPTH, D_MODEL, D_MODEL), D_MODEL ** -0.5),
        "ln2_g": gain((DEPTH, D_MODEL)),
        "w_gate": nrm((DEPTH, D_MODEL, D_FF), D_MODEL ** -0.5),
        "w_up": nrm((DEPTH, D_MODEL, D_FF), D_MODEL ** -0.5),
        "w_down": nrm((DEPTH, D_FF, D_MODEL), D_FF ** -0.5),
    }


def reference(x_prompt, x_sample, cache_k, cache_v, state_ssm, c_prompt, c_sample,
              w_ada, b_ada, ln1_g, w_in, q_norm_g, k_norm_g, attn_sinks,
              ssm_A_re, ssm_A_im, ssm_log_dt, ssm_B_re, ssm_B_im, ssm_C_re, ssm_C_im,
              ssm_D, ssm_glu_w, ssm_glu_b, attn_out_g, ssm_out_g, w_out,
              ln2_g, w_gate, w_up, w_down):
    y_prompt, y_sample = x_prompt, x_sample
    pos_prompt = jnp.arange(x_prompt.shape[1])
    pos_sample = PAST_LEN + jnp.arange(x_sample.shape[1])
    kp_l, vp_l, hp_l, ks_l, vs_l, hs_l = [], [], [], [], [], []
    for l in range(DEPTH):
        lw = dict(w_ada=w_ada[l], b_ada=b_ada[l], ln1_g=ln1_g[l], w_in=w_in[l],
                  q_norm_g=q_norm_g[l], k_norm_g=k_norm_g[l], attn_sinks=attn_sinks[l],
                  ssm_A_re=ssm_A_re[l], ssm_A_im=ssm_A_im[l], ssm_log_dt=ssm_log_dt[l],
                  ssm_B_re=ssm_B_re[l], ssm_B_im=ssm_B_im[l], ssm_C_re=ssm_C_re[l],
                  ssm_C_im=ssm_C_im[l], ssm_D=ssm_D[l], ssm_glu_w=ssm_glu_w[l],
                  ssm_glu_b=ssm_glu_b[l], attn_out_g=attn_out_g[l], ssm_out_g=ssm_out_g[l],
                  w_out=w_out[l], ln2_g=ln2_g[l], w_gate=w_gate[l], w_up=w_up[l],
                  w_down=w_down[l])
        y_prompt, kp, vp, hp = _layer(y_prompt, c_prompt, pos_prompt, None, None, None, lw)
        h0 = lax.complex(state_ssm[l, ..., 0].astype(jnp.float32), state_ssm[l, ..., 1].astype(jnp.float32))
        y_sample, ksn, vsn, hsn = _layer(y_sample, c_sample, pos_sample, cache_k[l], cache_v[l], h0, lw)
        kp_l.append(kp); vp_l.append(vp); hp_l.append(hp)
        ks_l.append(ksn); vs_l.append(vsn); hs_l.append(hsn)
    return (y_prompt, y_sample, jnp.stack(kp_l), jnp.stack(vp_l), jnp.stack(hp_l),
            jnp.stack(ks_l), jnp.stack(vs_l), jnp.stack(hs_l))
```

```python
import functools
import math

import jax
import jax.numpy as jnp
from jax import lax
from jax.experimental import pallas as pl
from jax.experimental.pallas import tpu as pltpu

F32 = jnp.float32
BF16 = jnp.bfloat16

CHUNK = 64
WINDOW = 128
HEAD_DIM = 64
N_HEADS = 8
N_KV_HEADS = 2
SSM_CH = 16
SSM_STATE = 64
SSM_L = 16
ROPE_THETA = 10000.0
EPS = 1e-6
PAST_LEN = 1024
LANES = 128
VMEM_LIMIT = 56 * 1024 * 1024


def _const_spec(shape):
    nd = len(shape)
    return pl.BlockSpec(shape, lambda *_: (0,) * nd, pipeline_mode=pl.Buffered(1))


def _params(n_grid):
    return pltpu.CompilerParams(dimension_semantics=("arbitrary",) * n_grid, vmem_limit_bytes=VMEM_LIMIT)


def _rms(x):
    return x * lax.rsqrt(jnp.mean(x * x, axis=-1, keepdims=True) + EPS)


def _dot(a, b):
    return jnp.dot(a, b, preferred_element_type=F32)


def _mod_kernel(c_ref, w_ref, b_ref, o_ref):
    c = c_ref[...]
    s = (c * jax.nn.sigmoid(c)).astype(BF16)
    o_ref[...] = _dot(s, w_ref[...].astype(BF16)) + b_ref[...]


def _mod_call(c, w, b):
    rows, d = c.shape
    cols = w.shape[1]
    tile = 1536
    return pl.pallas_call(
        _mod_kernel,
        grid=(cols // tile,),
        in_specs=[pl.BlockSpec((rows, d), lambda j: (0, 0)),
                  pl.BlockSpec((d, tile), lambda j: (0, j)),
                  pl.BlockSpec((1, tile), lambda j: (0, j))],
        out_specs=pl.BlockSpec((rows, tile), lambda j: (0, j)),
        out_shape=jax.ShapeDtypeStruct((rows, cols), F32),
        compiler_params=_params(1),
        name="mod",
    )(c, w, b.reshape(1, cols))


def _head_rms(t, ones_ref):
    width = t.shape[1]
    sq = t * t
    hi = sq.astype(BF16)
    lo = (sq - hi.astype(F32)).astype(BF16)
    parts = []
    for c0 in range(0, width, 256):
        w = min(256, width - c0)
        ones = ones_ref[:w, :w]
        parts.append(_dot(hi[:, c0:c0 + w], ones) + _dot(lo[:, c0:c0 + w], ones))
    ssq = parts[0] if len(parts) == 1 else jnp.concatenate(parts, axis=1)
    return t * lax.rsqrt(ssq * (1.0 / HEAD_DIM) + EPS)


def _rope(t, cos, s1, s2):
    outs = []
    for c0 in range(0, t.shape[1], LANES):
        xb = t[:, c0:c0 + LANES]
        outs.append(xb * cos + pltpu.roll(xb, LANES - HEAD_DIM // 2, 1) * s1 + pltpu.roll(xb, HEAD_DIM // 2, 1) * s2)
    return outs[0] if len(outs) == 1 else jnp.concatenate(outs, axis=1)


def _inproj_kernel(x_ref, sc_ref, sh_ref, ln_ref, w_ref, gq_ref, gk_ref, ones_ref, cos_ref, s1_ref, s2_ref,
                   q_ref, k_ref, v_ref, u_ref, klast_ref, vlast_ref, *, q_cols, kv_cols, keep):
    x = x_ref[0]
    h = (_rms(x) * ln_ref[...]) * (1.0 + sc_ref[0]) + sh_ref[0]
    proj = _dot(h.astype(BF16), w_ref[...])
    q = proj[:, :q_cols]
    k = proj[:, q_cols:q_cols + kv_cols]
    v = proj[:, q_cols + kv_cols:q_cols + 2 * kv_cols]
    u = proj[:, q_cols + 2 * kv_cols:]
    cos, s1, s2 = cos_ref[...], s1_ref[...], s2_ref[...]
    qr = _rope(_head_rms(q, ones_ref) * gq_ref[...], cos, s1, s2)
    kr = _rope(_head_rms(k, ones_ref) * gk_ref[...], cos, s1, s2)
    q_ref[0] = (qr * (HEAD_DIM ** -0.5)).astype(BF16)
    k_ref[0] = kr.astype(BF16)
    v_ref[0] = v.astype(BF16)
    u_ref[0] = u.astype(BF16)
    rows = x.shape[0]
    klast_ref[0] = kr[rows - keep:, :]
    vlast_ref[0] = v[rows - keep:, :]


def _inproj_call(x, sc, sh, ln_g, w_in, gq, gk, ones, cos, s1, s2, *, tile, keep):
    nb, rows, d = x.shape
    in_cols = w_in.shape[1]
    q_cols = N_HEADS * HEAD_DIM
    kv_cols = N_KV_HEADS * HEAD_DIM
    u_cols = in_cols - q_cols - 2 * kv_cols
    mrows = sc.shape[1]
    mtile = 1 if mrows == 1 else tile
    mod_spec = pl.BlockSpec((1, mtile, d), (lambda b, i: (b, 0, 0)) if mrows == 1 else (lambda b, i: (b, i, 0)))
    row_spec = lambda c: pl.BlockSpec((1, tile, c), lambda b, i: (b, i, 0))
    tab_spec = pl.BlockSpec((tile, LANES), lambda b, i: (i, 0))
    last_spec = pl.BlockSpec((1, keep, kv_cols), lambda b, i: (b, 0, 0))
    kern = functools.partial(_inproj_kernel, q_cols=q_cols, kv_cols=kv_cols, keep=keep)
    return pl.pallas_call(
        kern,
        grid=(nb, rows // tile),
        in_specs=[row_spec(d), mod_spec, mod_spec, _const_spec((1, d)), _const_spec((d, in_cols)),
                  _const_spec((1, q_cols)), _const_spec((1, kv_cols)), _const_spec((256, 256)),
                  tab_spec, tab_spec, tab_spec],
        out_specs=[row_spec(q_cols), row_spec(kv_cols), row_spec(kv_cols), row_spec(u_cols), last_spec, last_spec],
        out_shape=[jax.ShapeDtypeStruct((nb, rows, q_cols), BF16),
                   jax.ShapeDtypeStruct((nb, rows, kv_cols), BF16),
                   jax.ShapeDtypeStruct((nb, rows, kv_cols), BF16),
                   jax.ShapeDtypeStruct((nb, rows, u_cols), BF16),
                   jax.ShapeDtypeStruct((nb, keep, kv_cols), F32),
                   jax.ShapeDtypeStruct((nb, keep, kv_cols), F32)],
        compiler_params=_params(2),
        name="inproj",
    )(x, sc, sh, ln_g, w_in, gq, gk, ones, cos, s1, s2)


def _ssm_kernel(u_ref, h0_ref, wi_ref, toep_ref, cpow_ref, lam_ref, y_ref, hout_ref,
                s_re, s_im, hp_re, hp_im, *, nseq, nchunk):
    groups = u_ref.shape[0]
    rows = nseq * nchunk
    npair = groups // 2
    for p in range(npair):
        lhs = jnp.concatenate([u_ref[2 * p], u_ref[2 * p + 1]], axis=1)
        s = _dot(lhs, wi_ref[p])
        s_re[pl.ds(p, rows, stride=npair), :] = s[:, :LANES]
        s_im[pl.ds(p, rows, stride=npair), :] = s[:, LANES:]

    a_re = lam_ref[0]
    a_im = lam_ref[1]
    for b in range(nseq):
        def step(n, carry):
            h_re, h_im = carry
            r0 = pl.multiple_of((b * nchunk + n) * npair, npair)
            hp_re[pl.ds(r0, npair), :] = h_re
            hp_im[pl.ds(r0, npair), :] = h_im
            n_re = a_re * h_re - a_im * h_im + s_re[pl.ds(r0, npair), :]
            n_im = a_re * h_im + a_im * h_re + s_im[pl.ds(r0, npair), :]
            return n_re, n_im
        h_re, h_im = lax.fori_loop(0, nchunk, step, (h0_ref[b, 0], h0_ref[b, 1]))
        hout_ref[b, 0] = h_re
        hout_ref[b, 1] = h_im

    for g in range(groups):
        p = g // 2
        hp = jnp.concatenate([hp_re[pl.ds(p, rows, stride=npair), :],
                              hp_im[pl.ds(p, rows, stride=npair), :]], axis=1).astype(BF16)
        y = _dot(u_ref[g], toep_ref[g]) + _dot(hp, cpow_ref[g])
        y_ref[g] = y.astype(BF16)


def _ssm_call(u2, h0, wi, toep, cpow, lam, *, nseq, nchunk):
    groups, total_rows, width = u2.shape
    rows = nseq * nchunk
    npair = groups // 2
    kern = functools.partial(_ssm_kernel, nseq=nseq, nchunk=nchunk)
    return pl.pallas_call(
        kern,
        grid=(total_rows // rows,),
        in_specs=[pl.BlockSpec((groups, rows, width), lambda i: (0, i, 0)),
                  pl.BlockSpec((nseq, 2, npair, LANES), lambda i: (i, 0, 0, 0)),
                  _const_spec(wi.shape), _const_spec(toep.shape), _const_spec(cpow.shape), _const_spec(lam.shape)],
        out_specs=[pl.BlockSpec((groups, rows, width), lambda i: (0, i, 0)),
                   pl.BlockSpec((nseq, 2, npair, LANES), lambda i: (i, 0, 0, 0))],
        out_shape=[jax.ShapeDtypeStruct((groups, total_rows, width), BF16),
                   jax.ShapeDtypeStruct(h0.shape, F32)],
        scratch_shapes=[pltpu.VMEM((rows * npair, LANES), F32)] * 4,
        compiler_params=_params(1),
        name="ssm",
    )(u2, h0, wi, toep, cpow, lam)


def _kv_variants(a):
    lo = lax.broadcasted_iota(jnp.int32, a.shape, 1) < HEAD_DIM
    sw = pltpu.roll(a, HEAD_DIM, 1)
    zero = jnp.zeros_like(a)
    return ((jnp.where(lo, a, zero), jnp.where(lo, zero, sw)),
            (jnp.where(lo, sw, zero), jnp.where(lo, zero, a)))


def _sink_softmax_pv(qst, kb, vb, sink_col, valid):
    s = lax.dot_general(qst, kb, (((1,), (1,)), ((), ())), preferred_element_type=F32)
    if valid is not None:
        s = jnp.where(valid, s, -jnp.inf)
    m = jnp.maximum(jnp.max(s, axis=-1, keepdims=True), sink_col)
    p = jnp.exp(s - m)
    den = jnp.sum(p, axis=-1, keepdims=True) + jnp.exp(sink_col - m)
    return _dot(p.astype(BF16), vb) / den


def _sink_cols(sink_ref, g, half_rows):
    rep = N_HEADS // N_KV_HEADS
    row = lax.broadcasted_iota(jnp.int32, (2 * half_rows, 1), 0)
    return [jnp.where(row < half_rows, sink_ref[rep * g + e], sink_ref[rep * g + 2 + e]) for e in (0, 1)]


def _attn_prompt(q_ref, kc_ref, kp_ref, vc_ref, vp_ref, sink_ref, kz, vz, attn, tile):
    kvar = _kv_variants(jnp.concatenate([kp_ref[0], kc_ref[0]], axis=0).astype(F32))
    vvar = _kv_variants(jnp.concatenate([vp_ref[0], vc_ref[0]], axis=0).astype(F32))
    for g in range(2):
        for e in range(2):
            kz[g, e] = kvar[g][e].astype(BF16)
            vz[g, e] = vvar[g][e].astype(BF16)
    sinks = [_sink_cols(sink_ref, g, CHUNK) for g in range(2)]
    band = CHUNK + WINDOW
    first_chunk = pl.program_id(1) * (tile // CHUNK)
    col_chunk = lax.broadcasted_iota(jnp.int32, (1, band), 1) // CHUNK

    def chunk_body(cc, carry):
        r0 = pl.multiple_of(cc * CHUNK, CHUNK)
        valid = (col_chunk + (first_chunk + cc)) >= (WINDOW // CHUNK)
        for g in range(2):
            c0 = 2 * LANES * g
            qst = jnp.concatenate([q_ref[0, pl.ds(r0, CHUNK), c0:c0 + LANES],
                                   q_ref[0, pl.ds(r0, CHUNK), c0 + LANES:c0 + 2 * LANES]], axis=0)
            o = None
            for e in range(2):
                oe = _sink_softmax_pv(qst, kz[g, e, pl.ds(r0, band), :], vz[g, e, pl.ds(r0, band), :],
                                      sinks[g][e], valid)
                o = oe if o is None else o + oe
            attn[pl.ds(r0, CHUNK), c0:c0 + LANES] = o[:CHUNK]
            attn[pl.ds(r0, CHUNK), c0 + LANES:c0 + 2 * LANES] = o[CHUNK:]
        return carry

    lax.fori_loop(0, tile // CHUNK, chunk_body, 0)


def _attn_sample(q_ref, kc_ref, kp_ref, vc_ref, vp_ref, sink_ref, attn, nseq, seq):
    sinks = [_sink_cols(sink_ref, g, seq) for g in range(2)]
    for b in range(nseq):
        r0 = b * seq
        kvar = _kv_variants(jnp.concatenate([kp_ref[b], kc_ref[0, r0:r0 + seq, :]], axis=0).astype(F32))
        vvar = _kv_variants(jnp.concatenate([vp_ref[b], vc_ref[0, r0:r0 + seq, :]], axis=0).astype(F32))
        for g in range(2):
            c0 = 2 * LANES * g
            qst = jnp.concatenate([q_ref[0, r0:r0 + seq, c0:c0 + LANES],
                                   q_ref[0, r0:r0 + seq, c0 + LANES:c0 + 2 * LANES]], axis=0)
            o = None
            for e in range(2):
                oe = _sink_softmax_pv(qst, kvar[g][e].astype(BF16), vvar[g][e].astype(BF16), sinks[g][e], None)
                o = oe if o is None else o + oe
            attn[r0:r0 + seq, c0:c0 + LANES] = o[:seq]
            attn[r0:r0 + seq, c0 + LANES:c0 + 2 * LANES] = o[seq:]


def _gelu_tanh(x):
    return 0.5 * x * (1.0 + jnp.tanh(math.sqrt(2.0 / math.pi) * (x + 0.044715 * (x * x * x))))


def _mix_kernel(sink_ref, x_ref, g1_ref, q_ref, kc_ref, kp_ref, vc_ref, vp_ref, y_ref,
                gluw_ref, glub_ref, ga_ref, gs_ref, wout_ref, o_ref, *scratch, tile, nseq):
    if nseq is None:
        kz, vz, attn = scratch
        _attn_prompt(q_ref, kc_ref, kp_ref, vc_ref, vp_ref, sink_ref, kz, vz, attn, tile)
    else:
        (attn,) = scratch
        _attn_sample(q_ref, kc_ref, kp_ref, vc_ref, vp_ref, sink_ref, attn, nseq, tile // nseq)
    an = _rms(attn[...]) * ga_ref[...]
    gl = _gelu_tanh(y_ref[0].astype(F32))
    so = gl * jax.nn.sigmoid(_dot(gl.astype(BF16), gluw_ref[...]) + glub_ref[...])
    sn = _rms(so) * gs_ref[...]
    merged = jnp.concatenate([an, sn], axis=1).astype(BF16)
    o_ref[0] = x_ref[0] + g1_ref[0] * _dot(merged, wout_ref[...])


def _mix_call(sinks, x, g1, q, k, v, k_past, v_past, y, gluw, glub, ga, gs, wout, *, tile, nseq):
    nb, rows, d = x.shape
    aw = q.shape[2]
    kvw = k.shape[2]
    sw = y.shape[2]
    mrows = g1.shape[1]
    mtile = 1 if mrows == 1 else tile
    mod_spec = pl.BlockSpec((1, mtile, d), (lambda b, i: (b, 0, 0)) if mrows == 1 else (lambda b, i: (b, i, 0)))
    row_spec = lambda c: pl.BlockSpec((1, tile, c), lambda b, i: (b, i, 0))
    if nseq is None:
        wpt = tile // WINDOW
        past_spec = pl.BlockSpec((1, WINDOW, kvw), lambda b, i: (b, jnp.maximum(i * wpt - 1, 0), 0))
        k_past, v_past = k, v
        scratch = [pltpu.VMEM((2, 2, tile + WINDOW, LANES), BF16)] * 2 + [pltpu.VMEM((tile, aw), F32)]
    else:
        past_spec = pl.BlockSpec((nseq, WINDOW, kvw), lambda b, i: (0, 0, 0))
        scratch = [pltpu.VMEM((tile, aw), F32)]
    kern = functools.partial(_mix_kernel, tile=tile, nseq=nseq)
    return pl.pallas_call(
        kern,
        grid=(nb, rows // tile),
        in_specs=[pl.BlockSpec(memory_space=pltpu.SMEM),
                  row_spec(d), mod_spec, row_spec(aw), row_spec(kvw), past_spec, row_spec(kvw), past_spec,
                  row_spec(sw), _const_spec(gluw.shape), _const_spec(glub.shape), _const_spec(ga.shape),
                  _const_spec(gs.shape), _const_spec(wout.shape)],
        out_specs=row_spec(d),
        out_shape=jax.ShapeDtypeStruct((nb, rows, d), F32),
        scratch_shapes=scratch,
        compiler_params=_params(2),
        name="mix",
    )(sinks, x, g1, q, k, k_past, v, v_past, y, gluw, glub, ga, gs, wout)


def _ffn_kernel(x_ref, sc_ref, sh_ref, g2_ref, ln_ref, wg_ref, wu_ref, wd_ref, o_ref, *, ff_chunk):
    x = x_ref[0]
    h = ((_rms(x) * ln_ref[...]) * (1.0 + sc_ref[0]) + sh_ref[0]).astype(BF16)
    acc = None
    for c0 in range(0, wg_ref.shape[1], ff_chunk):
        a = _dot(h, wg_ref[:, c0:c0 + ff_chunk])
        b = _dot(h, wu_ref[:, c0:c0 + ff_chunk])
        part = _dot((a * jax.nn.sigmoid(a) * b).astype(BF16), wd_ref[c0:c0 + ff_chunk, :])
        acc = part if acc is None else acc + part
    o_ref[0] = x + g2_ref[0] * acc


def _ffn_call(x, sc, sh, g2, ln_g, wg, wu, wd, *, tile):
    nb, rows, d = x.shape
    dff = wg.shape[1]
    ff_chunk = dff // 2 if (dff // 2) % LANES == 0 else dff
    mrows = sc.shape[1]
    mtile = 1 if mrows == 1 else tile
    mod_spec = pl.BlockSpec((1, mtile, d), (lambda b, i: (b, 0, 0)) if mrows == 1 else (lambda b, i: (b, i, 0)))
    row_spec = pl.BlockSpec((1, tile, d), lambda b, i: (b, i, 0))
    kern = functools.partial(_ffn_kernel, ff_chunk=ff_chunk)
    return pl.pallas_call(
        kern,
        grid=(nb, rows // tile),
        in_specs=[row_spec, mod_spec, mod_spec, mod_spec, _const_spec((1, d)),
                  _const_spec(wg.shape), _const_spec(wu.shape), _const_spec(wd.shape)],
        out_specs=row_spec,
        out_shape=jax.ShapeDtypeStruct((nb, rows, d), F32),
        compiler_params=_params(2),
        name="ffn",
    )(x, sc, sh, g2, ln_g, wg, wu, wd)


def _ssm_weights(a_re, a_im, log_dt, b_re, b_im, c_re, c_im, d_skip):
    hp = lax.Precision.HIGHEST
    groups, state = a_re.shape
    ch = b_re.shape[2]
    dt = jnp.exp(log_dt)[:, None]
    lam = lax.complex(a_re, a_im)
    lam_bar = jnp.exp(lam * dt)
    coef = (lam_bar - 1.0) / lam
    bb = coef[:, :, None] * lax.complex(b_re, b_im)
    j = jnp.arange(SSM_L + 1, dtype=F32)[:, None, None]
    lp = jnp.exp((lam * dt)[None] * j)
    m = lp[:SSM_L, :, :, None] * bb[None]
    m_re, m_im = jnp.real(m), jnp.imag(m)
    kern = (jnp.einsum('gop,jgpi->gjoi', c_re, m_re, precision=hp)
            - jnp.einsum('gop,jgpi->gjoi', c_im, m_im, precision=hp))
    kern = kern.at[:, 0].add(jax.vmap(jnp.diag)(d_skip.reshape(groups, ch)))
    lag = jnp.arange(SSM_L)[None, :] - jnp.arange(SSM_L)[:, None]
    toep = jnp.where((lag >= 0)[None, :, :, None, None],
                     jnp.transpose(kern[:, jnp.clip(lag, 0)], (0, 1, 2, 4, 3)), 0.0)
    toep = jnp.transpose(toep, (0, 1, 3, 2, 4)).reshape(groups, SSM_L * ch, SSM_L * ch)
    wi = jnp.transpose(m[::-1], (1, 0, 3, 2)).reshape(groups, SSM_L * ch, state)
    zero = jnp.zeros_like(jnp.real(wi))
    wi_re, wi_im = jnp.real(wi), jnp.imag(wi)
    top = jnp.concatenate([wi_re[0::2], zero[0::2], wi_im[0::2], zero[0::2]], axis=2)
    bot = jnp.concatenate([zero[1::2], wi_re[1::2], zero[1::2], wi_im[1::2]], axis=2)
    wi_pair = jnp.concatenate([top, bot], axis=1)
    cl = lax.complex(c_re, c_im)[:, None] * lp[1:, :, None, :].transpose(1, 0, 2, 3)
    cp_re = jnp.transpose(jnp.real(cl), (0, 3, 1, 2)).reshape(groups, state, SSM_L * ch)
    cp_im = jnp.transpose(jnp.imag(cl), (0, 3, 1, 2)).reshape(groups, state, SSM_L * ch)
    zc = jnp.zeros_like(cp_re)
    even = jnp.concatenate([cp_re, zc, -cp_im, zc], axis=1)
    odd = jnp.concatenate([zc, cp_re, zc, -cp_im], axis=1)
    cpow = jnp.where((jnp.arange(groups) % 2 == 0)[:, None, None], even, odd)
    lam_l = lp[SSM_L].reshape(groups // 2, 2 * state)
    lam16 = jnp.stack([jnp.real(lam_l), jnp.imag(lam_l)])
    return wi_pair.astype(BF16), toep.astype(BF16), cpow.astype(BF16), lam16


def _rope_tables(pos):
    half = HEAD_DIM // 2
    inv = ROPE_THETA ** (-jnp.arange(half, dtype=F32) * 2.0 / HEAD_DIM)
    lane = jnp.arange(LANES)
    ang = pos.astype(F32)[:, None] * inv[lane % half][None, :]
    first = ((lane % HEAD_DIM) < half)[None, :]
    sin = jnp.sin(ang)
    return jnp.cos(ang), jnp.where(first, -sin, 0.0), jnp.where(first, 0.0, sin)


def _to_chunks(u, groups):
    b, s, w = u.shape
    ch = w // groups
    t = u.reshape(b, s // SSM_L, SSM_L, groups, ch)
    return jnp.transpose(t, (3, 0, 1, 2, 4)).reshape(groups, b * (s // SSM_L), SSM_L * ch)


def _from_chunks(y2, b, s):
    groups, _, w = y2.shape
    ch = w // SSM_L
    t = y2.reshape(groups, b, s // SSM_L, SSM_L, ch)
    return jnp.transpose(t, (1, 2, 3, 0, 4)).reshape(b, s, groups * ch)


def _state_in(state):
    b, groups, p, _ = state.shape
    return jnp.transpose(state, (0, 3, 1, 2)).reshape(b, 2, groups // 2, 2 * p)


def _state_out(h, groups):
    b = h.shape[0]
    return jnp.transpose(h.reshape(b, 2, groups, -1), (0, 2, 3, 1))


def _stream(x, mods, tabs, past, h0, lw, *, tile, keep, nseq_tile, seqs, seq_len):
    sh1, sc1, g1, sh2, sc2, g2 = mods
    nb, rows, d = x.shape
    groups = lw['groups']
    q, k, v, u, k_last, v_last = _inproj_call(x, sc1, sh1, lw['ln1'], lw['w_in'], lw['gq'], lw['gk'], lw['ones'],
                                              *tabs, tile=tile, keep=keep)
    u2 = _to_chunks(u.reshape(seqs, seq_len, -1), groups)
    nchunk = seq_len // SSM_L
    y2, h_last = _ssm_call(u2, h0, lw['wi'], lw['toep'], lw['cpow'], lw['lam'],
                           nseq=(1 if nseq_tile is None else nseq_tile), nchunk=nchunk)
    y = _from_chunks(y2, seqs, seq_len).reshape(nb, rows, -1)
    kp, vp = (None, None) if past is None else past
    x1 = _mix_call(lw['sinks'], x, g1, q, k, v, kp, vp, y, lw['gluw'], lw['glub'], lw['ga'], lw['gs'],
                   lw['w_out'], tile=tile, nseq=nseq_tile)
    out = _ffn_call(x1, sc2, sh2, g2, lw['ln2'], lw['wg'], lw['wu'], lw['wd'], tile=tile)
    return out, k_last, v_last, h_last


def kernel(x_prompt, x_sample, cache_k, cache_v, state_ssm, c_prompt, c_sample, w_ada, b_ada, ln1_g, w_in, q_norm_g, k_norm_g, attn_sinks, ssm_A_re, ssm_A_im, ssm_log_dt, ssm_B_re, ssm_B_im, ssm_C_re, ssm_C_im, ssm_D, ssm_glu_w, ssm_glu_b, attn_out_g, ssm_out_g, w_out, ln2_g, w_gate, w_up, w_down):
    depth = w_ada.shape[0]
    bp, sp, d = x_prompt.shape
    bs, ss, _ = x_sample.shape
    groups = ssm_A_re.shape[1]
    kvw = N_KV_HEADS * HEAD_DIM
    tile_p = min(512, sp)
    seg = jnp.arange(256) // HEAD_DIM
    ones = (seg[:, None] == seg[None, :]).astype(BF16)
    tabs_p = _rope_tables(jnp.arange(sp))
    tabs_s = tuple(jnp.tile(t, (bs, 1)) for t in _rope_tables(PAST_LEN + jnp.arange(ss)))

    yp = x_prompt
    ys = x_sample.reshape(1, bs * ss, d)
    outs = [[] for _ in range(6)]
    for l in range(depth):
        wi, toep, cpow, lam = _ssm_weights(ssm_A_re[l], ssm_A_im[l], ssm_log_dt[l], ssm_B_re[l], ssm_B_im[l],
                                           ssm_C_re[l], ssm_C_im[l], ssm_D[l])
        lw = dict(groups=groups, ln1=ln1_g[l][None], w_in=w_in[l].astype(BF16),
                  gq=jnp.tile(q_norm_g[l], N_HEADS)[None], gk=jnp.tile(k_norm_g[l], N_KV_HEADS)[None], ones=ones,
                  wi=wi, toep=toep, cpow=cpow, lam=lam, sinks=attn_sinks[l],
                  gluw=ssm_glu_w[l].astype(BF16), glub=ssm_glu_b[l][None], ga=attn_out_g[l][None],
                  gs=ssm_out_g[l][None], w_out=w_out[l].astype(BF16), ln2=ln2_g[l][None],
                  wg=w_gate[l].astype(BF16), wu=w_up[l].astype(BF16), wd=w_down[l].astype(BF16))
        mod = _mod_call(jnp.concatenate([c_prompt, c_sample], axis=0), w_ada[l], b_ada[l])
        mods_p = tuple(m[:, None, :] for m in jnp.split(mod[:bp], 6, axis=-1))
        mods_s = tuple(jnp.repeat(m, ss, axis=0)[None] for m in jnp.split(mod[bp:], 6, axis=-1))

        h0_p = jnp.zeros((bp, 2, groups // 2, 2 * SSM_STATE), F32)
        yp, kpl, vpl, hpl = _stream(yp, mods_p, tabs_p, None, h0_p, lw, tile=tile_p, keep=WINDOW,
                                    nseq_tile=None, seqs=bp, seq_len=sp)
        past = (cache_k[l].reshape(bs, WINDOW, kvw).astype(BF16), cache_v[l].reshape(bs, WINDOW, kvw).astype(BF16))
        ys, ksl, vsl, hsl = _stream(ys, mods_s, tabs_s, past, _state_in(state_ssm[l]), lw, tile=bs * ss,
                                    keep=bs * ss, nseq_tile=bs, seqs=bs, seq_len=ss)
        outs[0].append(kpl.reshape(bp, WINDOW, N_KV_HEADS, HEAD_DIM))
        outs[1].append(vpl.reshape(bp, WINDOW, N_KV_HEADS, HEAD_DIM))
        outs[2].append(_state_out(hpl, groups))
        outs[3].append(ksl.reshape(bs, ss, N_KV_HEADS, HEAD_DIM))
        outs[4].append(vsl.reshape(bs, ss, N_KV_HEADS, HEAD_DIM))
        outs[5].append(_state_out(hsl, groups))
    return (yp, ys.reshape(bs, ss, d)) + tuple(jnp.stack(o) for o in outs)
```

```python
import functools
import math

import jax
import jax.numpy as jnp
from jax import lax
from jax.experimental import pallas as pl
from jax.experimental.pallas import tpu as pltpu

F32 = jnp.float32
BF16 = jnp.bfloat16

CHUNK = 64
WINDOW = 128
HEAD_DIM = 64
N_HEADS = 8
N_KV_HEADS = 2
SSM_CH = 16
SSM_STATE = 64
SSM_L = 16
ROPE_THETA = 10000.0
EPS = 1e-6
PAST_LEN = 1024
LANES = 128
VMEM_LIMIT = 56 * 1024 * 1024


def _const_spec(shape):
    nd = len(shape)
    return pl.BlockSpec(shape, lambda *_: (0,) * nd, pipeline_mode=pl.Buffered(1))


def _params(n_grid):
    return pltpu.CompilerParams(dimension_semantics=("arbitrary",) * n_grid, vmem_limit_bytes=VMEM_LIMIT)


def _rms(x):
    return x * lax.rsqrt(jnp.mean(x * x, axis=-1, keepdims=True) + EPS)


def _dot(a, b):
    return jnp.dot(a, b, preferred_element_type=F32)


def _mod_kernel(c_ref, w_ref, b_ref, o_ref):
    c = c_ref[...]
    s = (c * jax.nn.sigmoid(c)).astype(BF16)
    o_ref[...] = _dot(s, w_ref[...].astype(BF16)) + b_ref[...]


def _mod_call(c, w, b):
    rows, d = c.shape
    cols = w.shape[1]
    tile = 1536
    return pl.pallas_call(
        _mod_kernel,
        grid=(cols // tile,),
        in_specs=[pl.BlockSpec((rows, d), lambda j: (0, 0)),
                  pl.BlockSpec((d, tile), lambda j: (0, j)),
                  pl.BlockSpec((1, tile), lambda j: (0, j))],
        out_specs=pl.BlockSpec((rows, tile), lambda j: (0, j)),
        out_shape=jax.ShapeDtypeStruct((rows, cols), F32),
        compiler_params=_params(1),
        name="mod",
    )(c, w, b.reshape(1, cols))


def _head_rms(t, ones_ref):
    width = t.shape[1]
    sq = t * t
    hi = sq.astype(BF16)
    lo = (sq - hi.astype(F32)).astype(BF16)
    parts = []
    for c0 in range(0, width, 256):
        w = min(256, width - c0)
        ones = ones_ref[:w, :w]
        parts.append(_dot(hi[:, c0:c0 + w], ones) + _dot(lo[:, c0:c0 + w], ones))
    ssq = parts[0] if len(parts) == 1 else jnp.concatenate(parts, axis=1)
    return t * lax.rsqrt(ssq * (1.0 / HEAD_DIM) + EPS)


def _rope(t, cos, s1, s2):
    outs = []
    for c0 in range(0, t.shape[1], LANES):
        xb = t[:, c0:c0 + LANES]
        outs.append(xb * cos + pltpu.roll(xb, LANES - HEAD_DIM // 2, 1) * s1 + pltpu.roll(xb, HEAD_DIM // 2, 1) * s2)
    return outs[0] if len(outs) == 1 else jnp.concatenate(outs, axis=1)


def _granule_transpose(arrs):
    gran = lax.broadcasted_iota(jnp.int32, arrs[0].shape, 1) // SSM_CH
    cur = list(arrs)
    for s in (4, 2, 1):
        upper = (gran & s) != 0
        nxt = list(cur)
        for a0 in range(8):
            if a0 & s:
                continue
            lo, hi = cur[a0], cur[a0 + s]
            nxt[a0] = jnp.where(upper, pltpu.roll(hi, SSM_CH * s, 1), lo)
            nxt[a0 + s] = jnp.where(upper, hi, pltpu.roll(lo, LANES - SSM_CH * s, 1))
        cur = nxt
    return cur


def _inproj_kernel(x_ref, sc_ref, sh_ref, ln_ref, w_ref, gq_ref, gk_ref, ones_ref, cos_ref, s1_ref, s2_ref,
                   q_ref, k_ref, v_ref, u_ref, klast_ref, vlast_ref, uscr, *, q_cols, kv_cols, keep):
    x = x_ref[0]
    h = (_rms(x) * ln_ref[...]) * (1.0 + sc_ref[0]) + sh_ref[0]
    proj = _dot(h.astype(BF16), w_ref[...])
    q = proj[:, :q_cols]
    k = proj[:, q_cols:q_cols + kv_cols]
    v = proj[:, q_cols + kv_cols:q_cols + 2 * kv_cols]
    u = proj[:, q_cols + 2 * kv_cols:]
    cos, s1, s2 = cos_ref[...], s1_ref[...], s2_ref[...]
    qr = _rope(_head_rms(q, ones_ref) * gq_ref[...], cos, s1, s2)
    kr = _rope(_head_rms(k, ones_ref) * gk_ref[...], cos, s1, s2)
    q_ref[0] = (qr * (HEAD_DIM ** -0.5)).astype(BF16)
    k_ref[0] = kr.astype(BF16)
    v_ref[0] = v.astype(BF16)
    rows = x.shape[0]
    nchunk = rows // SSM_L
    for o in range(u.shape[1] // LANES):
        uscr[o] = u[:, LANES * o:LANES * (o + 1)]
    for o in range(u.shape[1] // LANES):
        for hh in range(SSM_L // 8):
            outs = _granule_transpose([uscr[o, pl.ds(8 * hh + t, nchunk, stride=SSM_L), :] for t in range(8)])
            for g8 in range(8):
                u_ref[8 * o + g8, :, LANES * hh:LANES * (hh + 1)] = outs[g8].astype(BF16)
    klast_ref[0] = kr[rows - keep:, :]
    vlast_ref[0] = v[rows - keep:, :]


def _inproj_call(x, sc, sh, ln_g, w_in, gq, gk, ones, cos, s1, s2, *, tile, keep):
    nb, rows, d = x.shape
    in_cols = w_in.shape[1]
    q_cols = N_HEADS * HEAD_DIM
    kv_cols = N_KV_HEADS * HEAD_DIM
    u_cols = in_cols - q_cols - 2 * kv_cols
    mrows = sc.shape[1]
    mtile = 1 if mrows == 1 else tile
    mod_spec = pl.BlockSpec((1, mtile, d), (lambda b, i: (b, 0, 0)) if mrows == 1 else (lambda b, i: (b, i, 0)))
    row_spec = lambda c: pl.BlockSpec((1, tile, c), lambda b, i: (b, i, 0))
    tab_spec = pl.BlockSpec((tile, LANES), lambda b, i: (i, 0))
    last_spec = pl.BlockSpec((1, keep, kv_cols), lambda b, i: (b, 0, 0))
    groups = u_cols // SSM_CH
    tiles = rows // tile
    chunk_spec = pl.BlockSpec((groups, tile // SSM_L, SSM_L * SSM_CH), lambda b, i: (0, b * tiles + i, 0))
    kern = functools.partial(_inproj_kernel, q_cols=q_cols, kv_cols=kv_cols, keep=keep)
    return pl.pallas_call(
        kern,
        grid=(nb, tiles),
        in_specs=[row_spec(d), mod_spec, mod_spec, _const_spec((1, d)), _const_spec((d, in_cols)),
                  _const_spec((1, q_cols)), _const_spec((1, kv_cols)), _const_spec((256, 256)),
                  tab_spec, tab_spec, tab_spec],
        out_specs=[row_spec(q_cols), row_spec(kv_cols), row_spec(kv_cols), chunk_spec, last_spec, last_spec],
        out_shape=[jax.ShapeDtypeStruct((nb, rows, q_cols), BF16),
                   jax.ShapeDtypeStruct((nb, rows, kv_cols), BF16),
                   jax.ShapeDtypeStruct((nb, rows, kv_cols), BF16),
                   jax.ShapeDtypeStruct((groups, nb * rows // SSM_L, SSM_L * SSM_CH), BF16),
                   jax.ShapeDtypeStruct((nb, keep, kv_cols), F32),
                   jax.ShapeDtypeStruct((nb, keep, kv_cols), F32)],
        scratch_shapes=[pltpu.VMEM((u_cols // LANES, tile, LANES), F32)],
        compiler_params=_params(2),
        name="inproj",
    )(x, sc, sh, ln_g, w_in, gq, gk, ones, cos, s1, s2)


def _ssm_kernel(u_ref, h0_ref, wi_ref, toep_ref, cpow_ref, lam_ref, y_ref, hout_ref,
                s_re, s_im, hp_re, hp_im, *, nseq, nchunk):
    groups = u_ref.shape[0]
    rows = nseq * nchunk
    npair = groups // 2
    for p in range(npair):
        lhs = jnp.concatenate([u_ref[2 * p], u_ref[2 * p + 1]], axis=1)
        s = _dot(lhs, wi_ref[p])
        s_re[pl.ds(p, rows, stride=npair), :] = s[:, :LANES]
        s_im[pl.ds(p, rows, stride=npair), :] = s[:, LANES:]

    a_re = lam_ref[0]
    a_im = lam_ref[1]
    for b in range(nseq):
        def step(n, carry):
            h_re, h_im = carry
            r0 = pl.multiple_of((b * nchunk + n) * npair, npair)
            hp_re[pl.ds(r0, npair), :] = h_re
            hp_im[pl.ds(r0, npair), :] = h_im
            n_re = a_re * h_re - a_im * h_im + s_re[pl.ds(r0, npair), :]
            n_im = a_re * h_im + a_im * h_re + s_im[pl.ds(r0, npair), :]
            return n_re, n_im
        h_re, h_im = lax.fori_loop(0, nchunk, step, (h0_ref[b, 0], h0_ref[b, 1]))
        hout_ref[b, 0] = h_re
        hout_ref[b, 1] = h_im

    for g in range(groups):
        p = g // 2
        hp = jnp.concatenate([hp_re[pl.ds(p, rows, stride=npair), :],
                              hp_im[pl.ds(p, rows, stride=npair), :]], axis=1).astype(BF16)
        y = _dot(u_ref[g], toep_ref[g]) + _dot(hp, cpow_ref[g])
        y_ref[g] = y.astype(BF16)


def _ssm_call(u2, h0, wi, toep, cpow, lam, *, nseq, nchunk):
    groups, total_rows, width = u2.shape
    rows = nseq * nchunk
    npair = groups // 2
    kern = functools.partial(_ssm_kernel, nseq=nseq, nchunk=nchunk)
    return pl.pallas_call(
        kern,
        grid=(total_rows // rows,),
        in_specs=[pl.BlockSpec((groups, rows, width), lambda i: (0, i, 0)),
                  pl.BlockSpec((nseq, 2, npair, LANES), lambda i: (i, 0, 0, 0)),
                  _const_spec(wi.shape), _const_spec(toep.shape), _const_spec(cpow.shape), _const_spec(lam.shape)],
        out_specs=[pl.BlockSpec((groups, rows, width), lambda i: (0, i, 0)),
                   pl.BlockSpec((nseq, 2, npair, LANES), lambda i: (i, 0, 0, 0))],
        out_shape=[jax.ShapeDtypeStruct((groups, total_rows, width), BF16),
                   jax.ShapeDtypeStruct(h0.shape, F32)],
        scratch_shapes=[pltpu.VMEM((rows * npair, LANES), F32)] * 4,
        compiler_params=_params(1),
        name="ssm",
    )(u2, h0, wi, toep, cpow, lam)


def _kv_variants(a):
    lo = lax.broadcasted_iota(jnp.int32, a.shape, 1) < HEAD_DIM
    sw = pltpu.roll(a, HEAD_DIM, 1)
    zero = jnp.zeros_like(a)
    return ((jnp.where(lo, a, zero), jnp.where(lo, zero, sw)),
            (jnp.where(lo, sw, zero), jnp.where(lo, zero, a)))


def _sink_softmax_pv(qst, kb, vb, sink_col, valid):
    s = lax.dot_general(qst, kb, (((1,), (1,)), ((), ())), preferred_element_type=F32)
    if valid is not None:
        s = jnp.where(valid, s, -jnp.inf)
    m = jnp.maximum(jnp.max(s, axis=-1, keepdims=True), sink_col)
    p = jnp.exp(s - m)
    den = jnp.sum(p, axis=-1, keepdims=True) + jnp.exp(sink_col - m)
    return _dot(p.astype(BF16), vb) / den


def _sink_cols(sink_ref, g, half_rows):
    rep = N_HEADS // N_KV_HEADS
    row = lax.broadcasted_iota(jnp.int32, (2 * half_rows, 1), 0)
    return [jnp.where(row < half_rows, sink_ref[rep * g + e], sink_ref[rep * g + 2 + e]) for e in (0, 1)]


def _attn_prompt(q_ref, kc_ref, kp_ref, vc_ref, vp_ref, sink_ref, kz, vz, attn, tile):
    kvar = _kv_variants(jnp.concatenate([kp_ref[0], kc_ref[0]], axis=0).astype(F32))
    vvar = _kv_variants(jnp.concatenate([vp_ref[0], vc_ref[0]], axis=0).astype(F32))
    for g in range(2):
        for e in range(2):
            kz[g, e] = kvar[g][e].astype(BF16)
            vz[g, e] = vvar[g][e].astype(BF16)
    sinks = [_sink_cols(sink_ref, g, CHUNK) for g in range(2)]
    band = CHUNK + WINDOW
    first_chunk = pl.program_id(1) * (tile // CHUNK)
    col_chunk = lax.broadcasted_iota(jnp.int32, (1, band), 1) // CHUNK

    def chunk_body(cc, carry):
        r0 = pl.multiple_of(cc * CHUNK, CHUNK)
        valid = (col_chunk + (first_chunk + cc)) >= (WINDOW // CHUNK)
        for g in range(2):
            c0 = 2 * LANES * g
            qst = jnp.concatenate([q_ref[0, pl.ds(r0, CHUNK), c0:c0 + LANES],
                                   q_ref[0, pl.ds(r0, CHUNK), c0 + LANES:c0 + 2 * LANES]], axis=0)
            o = None
            for e in range(2):
                oe = _sink_softmax_pv(qst, kz[g, e, pl.ds(r0, band), :], vz[g, e, pl.ds(r0, band), :],
                                      sinks[g][e], valid)
                o = oe if o is None else o + oe
            attn[pl.ds(r0, CHUNK), c0:c0 + LANES] = o[:CHUNK]
            attn[pl.ds(r0, CHUNK), c0 + LANES:c0 + 2 * LANES] = o[CHUNK:]
        return carry

    lax.fori_loop(0, tile // CHUNK, chunk_body, 0)


def _attn_sample(q_ref, kc_ref, kp_ref, vc_ref, vp_ref, sink_ref, attn, nseq, seq):
    sinks = [_sink_cols(sink_ref, g, seq) for g in range(2)]
    for b in range(nseq):
        r0 = b * seq
        kvar = _kv_variants(jnp.concatenate([kp_ref[b], kc_ref[0, r0:r0 + seq, :]], axis=0).astype(F32))
        vvar = _kv_variants(jnp.concatenate([vp_ref[b], vc_ref[0, r0:r0 + seq, :]], axis=0).astype(F32))
        for g in range(2):
            c0 = 2 * LANES * g
            qst = jnp.concatenate([q_ref[0, r0:r0 + seq, c0:c0 + LANES],
                                   q_ref[0, r0:r0 + seq, c0 + LANES:c0 + 2 * LANES]], axis=0)
            o = None
            for e in range(2):
                oe = _sink_softmax_pv(qst, kvar[g][e].astype(BF16), vvar[g][e].astype(BF16), sinks[g][e], None)
                o = oe if o is None else o + oe
            attn[r0:r0 + seq, c0:c0 + LANES] = o[:seq]
            attn[r0:r0 + seq, c0 + LANES:c0 + 2 * LANES] = o[seq:]


def _gelu_tanh(x):
    return 0.5 * x * (1.0 + jnp.tanh(math.sqrt(2.0 / math.pi) * (x + 0.044715 * (x * x * x))))


def _mix_kernel(sink_ref, x_ref, g1_ref, q_ref, kc_ref, kp_ref, vc_ref, vp_ref, y_ref,
                gluw_ref, glub_ref, ga_ref, gs_ref, wout_ref, o_ref, *scratch, tile, nseq):
    if nseq is None:
        yscr, kz, vz, attn = scratch
        _attn_prompt(q_ref, kc_ref, kp_ref, vc_ref, vp_ref, sink_ref, kz, vz, attn, tile)
    else:
        yscr, attn = scratch
        _attn_sample(q_ref, kc_ref, kp_ref, vc_ref, vp_ref, sink_ref, attn, nseq, tile // nseq)
    an = _rms(attn[...]) * ga_ref[...]
    nslab = yscr.shape[0]
    for o in range(nslab):
        for hh in range(SSM_L // 8):
            outs = _granule_transpose([y_ref[8 * o + g8, :, LANES * hh:LANES * (hh + 1)].astype(F32)
                                       for g8 in range(8)])
            for t8 in range(8):
                yscr[o, pl.ds(8 * hh + t8, tile // SSM_L, stride=SSM_L), :] = outs[t8]
    gl = _gelu_tanh(jnp.concatenate([yscr[o] for o in range(nslab)], axis=1))
    so = gl * jax.nn.sigmoid(_dot(gl.astype(BF16), gluw_ref[...]) + glub_ref[...])
    sn = _rms(so) * gs_ref[...]
    merged = jnp.concatenate([an, sn], axis=1).astype(BF16)
    o_ref[0] = x_ref[0] + g1_ref[0] * _dot(merged, wout_ref[...])


def _mix_call(sinks, x, g1, q, k, v, k_past, v_past, y, gluw, glub, ga, gs, wout, *, tile, nseq):
    nb, rows, d = x.shape
    aw = q.shape[2]
    kvw = k.shape[2]
    groups, _, cw = y.shape
    tiles = rows // tile
    mrows = g1.shape[1]
    mtile = 1 if mrows == 1 else tile
    mod_spec = pl.BlockSpec((1, mtile, d), (lambda b, i: (b, 0, 0)) if mrows == 1 else (lambda b, i: (b, i, 0)))
    row_spec = lambda c: pl.BlockSpec((1, tile, c), lambda b, i: (b, i, 0))
    chunk_spec = pl.BlockSpec((groups, tile // SSM_L, cw), lambda b, i: (0, b * tiles + i, 0))
    scratch = [pltpu.VMEM((groups * SSM_CH // LANES, tile, LANES), F32)]
    if nseq is None:
        wpt = tile // WINDOW
        past_spec = pl.BlockSpec((1, WINDOW, kvw), lambda b, i: (b, jnp.maximum(i * wpt - 1, 0), 0))
        k_past, v_past = k, v
        scratch += [pltpu.VMEM((2, 2, tile + WINDOW, LANES), BF16)] * 2 + [pltpu.VMEM((tile, aw), F32)]
    else:
        past_spec = pl.BlockSpec((nseq, WINDOW, kvw), lambda b, i: (0, 0, 0))
        scratch += [pltpu.VMEM((tile, aw), F32)]
    kern = functools.partial(_mix_kernel, tile=tile, nseq=nseq)
    return pl.pallas_call(
        kern,
        grid=(nb, tiles),
        in_specs=[pl.BlockSpec(memory_space=pltpu.SMEM),
                  row_spec(d), mod_spec, row_spec(aw), row_spec(kvw), past_spec, row_spec(kvw), past_spec,
                  chunk_spec, _const_spec(gluw.shape), _const_spec(glub.shape), _const_spec(ga.shape),
                  _const_spec(gs.shape), _const_spec(wout.shape)],
        out_specs=row_spec(d),
        out_shape=jax.ShapeDtypeStruct((nb, rows, d), F32),
        scratch_shapes=scratch,
        compiler_params=_params(2),
        name="mix",
    )(sinks, x, g1, q, k, k_past, v, v_past, y, gluw, glub, ga, gs, wout)


def _ffn_kernel(x_ref, sc_ref, sh_ref, g2_ref, ln_ref, wg_ref, wu_ref, wd_ref, o_ref, *, ff_chunk):
    x = x_ref[0]
    h = ((_rms(x) * ln_ref[...]) * (1.0 + sc_ref[0]) + sh_ref[0]).astype(BF16)
    acc = None
    for c0 in range(0, wg_ref.shape[1], ff_chunk):
        a = _dot(h, wg_ref[:, c0:c0 + ff_chunk])
        b = _dot(h, wu_ref[:, c0:c0 + ff_chunk])
        part = _dot((a * jax.nn.sigmoid(a) * b).astype(BF16), wd_ref[c0:c0 + ff_chunk, :])
        acc = part if acc is None else acc + part
    o_ref[0] = x + g2_ref[0] * acc


def _ffn_call(x, sc, sh, g2, ln_g, wg, wu, wd, *, tile):
    nb, rows, d = x.shape
    dff = wg.shape[1]
    ff_chunk = dff // 2 if (dff // 2) % LANES == 0 else dff
    mrows = sc.shape[1]
    mtile = 1 if mrows == 1 else tile
    mod_spec = pl.BlockSpec((1, mtile, d), (lambda b, i: (b, 0, 0)) if mrows == 1 else (lambda b, i: (b, i, 0)))
    row_spec = pl.BlockSpec((1, tile, d), lambda b, i: (b, i, 0))
    kern = functools.partial(_ffn_kernel, ff_chunk=ff_chunk)
    return pl.pallas_call(
        kern,
        grid=(nb, rows // tile),
        in_specs=[row_spec, mod_spec, mod_spec, mod_spec, _const_spec((1, d)),
                  _const_spec(wg.shape), _const_spec(wu.shape), _const_spec(wd.shape)],
        out_specs=row_spec,
        out_shape=jax.ShapeDtypeStruct((nb, rows, d), F32),
        compiler_params=_params(2),
        name="ffn",
    )(x, sc, sh, g2, ln_g, wg, wu, wd)


def _ssm_weights(a_re, a_im, log_dt, b_re, b_im, c_re, c_im, d_skip):
    hp = lax.Precision.HIGHEST
    groups, state = a_re.shape
    ch = b_re.shape[2]
    dt = jnp.exp(log_dt)[:, None]
    lam = lax.complex(a_re, a_im)
    lam_bar = jnp.exp(lam * dt)
    coef = (lam_bar - 1.0) / lam
    bb = coef[:, :, None] * lax.complex(b_re, b_im)
    j = jnp.arange(SSM_L + 1, dtype=F32)[:, None, None]
    lp = jnp.exp((lam * dt)[None] * j)
    m = lp[:SSM_L, :, :, None] * bb[None]
    m_re, m_im = jnp.real(m), jnp.imag(m)
    kern = (jnp.einsum('gop,jgpi->gjoi', c_re, m_re, precision=hp)
            - jnp.einsum('gop,jgpi->gjoi', c_im, m_im, precision=hp))
    kern = kern.at[:, 0].add(jax.vmap(jnp.diag)(d_skip.reshape(groups, ch)))
    lag = jnp.arange(SSM_L)[None, :] - jnp.arange(SSM_L)[:, None]
    toep = jnp.where((lag >= 0)[None, :, :, None, None],
                     jnp.transpose(kern[:, jnp.clip(lag, 0)], (0, 1, 2, 4, 3)), 0.0)
    toep = jnp.transpose(toep, (0, 1, 3, 2, 4)).reshape(groups, SSM_L * ch, SSM_L * ch)
    wi = jnp.transpose(m[::-1], (1, 0, 3, 2)).reshape(groups, SSM_L * ch, state)
    zero = jnp.zeros_like(jnp.real(wi))
    wi_re, wi_im = jnp.real(wi), jnp.imag(wi)
    top = jnp.concatenate([wi_re[0::2], zero[0::2], wi_im[0::2], zero[0::2]], axis=2)
    bot = jnp.concatenate([zero[1::2], wi_re[1::2], zero[1::2], wi_im[1::2]], axis=2)
    wi_pair = jnp.concatenate([top, bot], axis=1)
    cl = lax.complex(c_re, c_im)[:, None] * lp[1:, :, None, :].transpose(1, 0, 2, 3)
    cp_re = jnp.transpose(jnp.real(cl), (0, 3, 1, 2)).reshape(groups, state, SSM_L * ch)
    cp_im = jnp.transpose(jnp.imag(cl), (0, 3, 1, 2)).reshape(groups, state, SSM_L * ch)
    zc = jnp.zeros_like(cp_re)
    even = jnp.concatenate([cp_re, zc, -cp_im, zc], axis=1)
    odd = jnp.concatenate([zc, cp_re, zc, -cp_im], axis=1)
    cpow = jnp.where((jnp.arange(groups) % 2 == 0)[:, None, None], even, odd)
    lam_l = lp[SSM_L].reshape(groups // 2, 2 * state)
    lam16 = jnp.stack([jnp.real(lam_l), jnp.imag(lam_l)])
    return wi_pair.astype(BF16), toep.astype(BF16), cpow.astype(BF16), lam16


def _rope_tables(pos):
    half = HEAD_DIM // 2
    inv = ROPE_THETA ** (-jnp.arange(half, dtype=F32) * 2.0 / HEAD_DIM)
    lane = jnp.arange(LANES)
    ang = pos.astype(F32)[:, None] * inv[lane % half][None, :]
    first = ((lane % HEAD_DIM) < half)[None, :]
    sin = jnp.sin(ang)
    return jnp.cos(ang), jnp.where(first, -sin, 0.0), jnp.where(first, 0.0, sin)


def _state_in(state):
    b, groups, p, _ = state.shape
    return jnp.transpose(state, (0, 3, 1, 2)).reshape(b, 2, groups // 2, 2 * p)


def _state_out(h, groups):
    b = h.shape[0]
    return jnp.transpose(h.reshape(b, 2, groups, -1), (0, 2, 3, 1))


def _stream(x, mods, tabs, past, h0, lw, *, tile, keep, nseq_tile, seqs, seq_len):
    sh1, sc1, g1, sh2, sc2, g2 = mods
    nb, rows, d = x.shape
    groups = lw['groups']
    q, k, v, u2, k_last, v_last = _inproj_call(x, sc1, sh1, lw['ln1'], lw['w_in'], lw['gq'], lw['gk'], lw['ones'],
                                               *tabs, tile=tile, keep=keep)
    y2, h_last = _ssm_call(u2, h0, lw['wi'], lw['toep'], lw['cpow'], lw['lam'],
                           nseq=(1 if nseq_tile is None else nseq_tile), nchunk=seq_len // SSM_L)
    kp, vp = (None, None) if past is None else past
    x1 = _mix_call(lw['sinks'], x, g1, q, k, v, kp, vp, y2, lw['gluw'], lw['glub'], lw['ga'], lw['gs'],
                   lw['w_out'], tile=tile, nseq=nseq_tile)
    out = _ffn_call(x1, sc2, sh2, g2, lw['ln2'], lw['wg'], lw['wu'], lw['wd'], tile=tile)
    return out, k_last, v_last, h_last


def kernel(x_prompt, x_sample, cache_k, cache_v, state_ssm, c_prompt, c_sample, w_ada, b_ada, ln1_g, w_in, q_norm_g, k_norm_g, attn_sinks, ssm_A_re, ssm_A_im, ssm_log_dt, ssm_B_re, ssm_B_im, ssm_C_re, ssm_C_im, ssm_D, ssm_glu_w, ssm_glu_b, attn_out_g, ssm_out_g, w_out, ln2_g, w_gate, w_up, w_down):
    depth = w_ada.shape[0]
    bp, sp, d = x_prompt.shape
    bs, ss, _ = x_sample.shape
    groups = ssm_A_re.shape[1]
    kvw = N_KV_HEADS * HEAD_DIM
    tile_p = min(512, sp)
    seg = jnp.arange(256) // HEAD_DIM
    ones = (seg[:, None] == seg[None, :]).astype(BF16)
    tabs_p = _rope_tables(jnp.arange(sp))
    tabs_s = tuple(jnp.tile(t, (bs, 1)) for t in _rope_tables(PAST_LEN + jnp.arange(ss)))

    yp = x_prompt
    ys = x_sample.reshape(1, bs * ss, d)
    outs = [[] for _ in range(6)]
    for l in range(depth):
        wi, toep, cpow, lam = _ssm_weights(ssm_A_re[l], ssm_A_im[l], ssm_log_dt[l], ssm_B_re[l], ssm_B_im[l],
                                           ssm_C_re[l], ssm_C_im[l], ssm_D[l])
        lw = dict(groups=groups, ln1=ln1_g[l][None], w_in=w_in[l].astype(BF16),
                  gq=jnp.tile(q_norm_g[l], N_HEADS)[None], gk=jnp.tile(k_norm_g[l], N_KV_HEADS)[None], ones=ones,
                  wi=wi, toep=toep, cpow=cpow, lam=lam, sinks=attn_sinks[l],
                  gluw=ssm_glu_w[l].astype(BF16), glub=ssm_glu_b[l][None], ga=attn_out_g[l][None],
                  gs=ssm_out_g[l][None], w_out=w_out[l].astype(BF16), ln2=ln2_g[l][None],
                  wg=w_gate[l].astype(BF16), wu=w_up[l].astype(BF16), wd=w_down[l].astype(BF16))
        mod = _mod_call(jnp.concatenate([c_prompt, c_sample], axis=0), w_ada[l], b_ada[l])
        mods_p = tuple(m[:, None, :] for m in jnp.split(mod[:bp], 6, axis=-1))
        mods_s = tuple(jnp.repeat(m, ss, axis=0)[None] for m in jnp.split(mod[bp:], 6, axis=-1))

        h0_p = jnp.zeros((bp, 2, groups // 2, 2 * SSM_STATE), F32)
        yp, kpl, vpl, hpl = _stream(yp, mods_p, tabs_p, None, h0_p, lw, tile=tile_p, keep=WINDOW,
                                    nseq_tile=None, seqs=bp, seq_len=sp)
        past = (cache_k[l].reshape(bs, WINDOW, kvw).astype(BF16), cache_v[l].reshape(bs, WINDOW, kvw).astype(BF16))
        ys, ksl, vsl, hsl = _stream(ys, mods_s, tabs_s, past, _state_in(state_ssm[l]), lw, tile=bs * ss,
                                    keep=bs * ss, nseq_tile=bs, seqs=bs, seq_len=ss)
        outs[0].append(kpl.reshape(bp, WINDOW, N_KV_HEADS, HEAD_DIM))
        outs[1].append(vpl.reshape(bp, WINDOW, N_KV_HEADS, HEAD_DIM))
        outs[2].append(_state_out(hpl, groups))
        outs[3].append(ksl.reshape(bs, ss, N_KV_HEADS, HEAD_DIM))
        outs[4].append(vsl.reshape(bs, ss, N_KV_HEADS, HEAD_DIM))
        outs[5].append(_state_out(hsl, groups))
    return (yp, ys.reshape(bs, ss, d)) + tuple(jnp.stack(o) for o in outs)
```

```python
import functools
import math

import jax
import jax.numpy as jnp
from jax import lax
from jax.experimental import pallas as pl
from jax.experimental.pallas import tpu as pltpu

F32 = jnp.float32
BF16 = jnp.bfloat16

CHUNK = 64
WINDOW = 128
HEAD_DIM = 64
N_HEADS = 8
N_KV_HEADS = 2
SSM_CH = 16
SSM_STATE = 64
SSM_L = 16
ROPE_THETA = 10000.0
EPS = 1e-6
PAST_LEN = 1024
LANES = 128
VMEM_LIMIT = 56 * 1024 * 1024


def _const_spec(shape):
    nd = len(shape)
    return pl.BlockSpec(shape, lambda *_: (0,) * nd, pipeline_mode=pl.Buffered(1))


def _params(n_grid):
    return pltpu.CompilerParams(dimension_semantics=("arbitrary",) * n_grid, vmem_limit_bytes=VMEM_LIMIT)


def _rms(x):
    return x * lax.rsqrt(jnp.mean(x * x, axis=-1, keepdims=True) + EPS)


def _dot(a, b):
    return jnp.dot(a, b, preferred_element_type=F32)


def _mod_kernel(c_ref, w_ref, b_ref, o_ref):
    c = c_ref[...]
    s = (c * jax.nn.sigmoid(c)).astype(BF16)
    o_ref[...] = _dot(s, w_ref[...].astype(BF16)) + b_ref[...]


def _mod_call(c, w, b):
    rows, d = c.shape
    cols = w.shape[1]
    tile = 1536
    return pl.pallas_call(
        _mod_kernel,
        grid=(cols // tile,),
        in_specs=[pl.BlockSpec((rows, d), lambda j: (0, 0)),
                  pl.BlockSpec((d, tile), lambda j: (0, j)),
                  pl.BlockSpec((1, tile), lambda j: (0, j))],
        out_specs=pl.BlockSpec((rows, tile), lambda j: (0, j)),
        out_shape=jax.ShapeDtypeStruct((rows, cols), F32),
        compiler_params=_params(1),
        name="mod",
    )(c, w, b.reshape(1, cols))


def _head_rms(t, ones_ref):
    width = t.shape[1]
    sq = t * t
    hi = sq.astype(BF16)
    lo = (sq - hi.astype(F32)).astype(BF16)
    parts = []
    for c0 in range(0, width, 256):
        w = min(256, width - c0)
        ones = ones_ref[:w, :w]
        parts.append(_dot(hi[:, c0:c0 + w], ones) + _dot(lo[:, c0:c0 + w], ones))
    ssq = parts[0] if len(parts) == 1 else jnp.concatenate(parts, axis=1)
    return t * lax.rsqrt(ssq * (1.0 / HEAD_DIM) + EPS)


def _rope(t, cos, s1, s2):
    outs = []
    for c0 in range(0, t.shape[1], LANES):
        xb = t[:, c0:c0 + LANES]
        outs.append(xb * cos + pltpu.roll(xb, LANES - HEAD_DIM // 2, 1) * s1 + pltpu.roll(xb, HEAD_DIM // 2, 1) * s2)
    return outs[0] if len(outs) == 1 else jnp.concatenate(outs, axis=1)


def _granule_transpose(arrs):
    gran = lax.broadcasted_iota(jnp.int32, arrs[0].shape, 1) // SSM_CH
    cur = list(arrs)
    for s in (4, 2, 1):
        upper = (gran & s) != 0
        nxt = list(cur)
        for a0 in range(8):
            if a0 & s:
                continue
            lo, hi = cur[a0], cur[a0 + s]
            nxt[a0] = jnp.where(upper, pltpu.roll(hi, SSM_CH * s, 1), lo)
            nxt[a0 + s] = jnp.where(upper, hi, pltpu.roll(lo, LANES - SSM_CH * s, 1))
        cur = nxt
    return cur


def _inproj_kernel(x_ref, sc_ref, sh_ref, ln_ref, w_ref, gq_ref, gk_ref, ones_ref, cos_ref, s1_ref, s2_ref,
                   q_ref, k_ref, v_ref, u_ref, klast_ref, vlast_ref, uscr, *, q_cols, kv_cols, keep):
    x = x_ref[0]
    h = (_rms(x) * ln_ref[...]) * (1.0 + sc_ref[0]) + sh_ref[0]
    proj = _dot(h.astype(BF16), w_ref[...])
    q = proj[:, :q_cols]
    k = proj[:, q_cols:q_cols + kv_cols]
    v = proj[:, q_cols + kv_cols:q_cols + 2 * kv_cols]
    u = proj[:, q_cols + 2 * kv_cols:]
    cos, s1, s2 = cos_ref[...], s1_ref[...], s2_ref[...]
    qr = _rope(_head_rms(q, ones_ref) * gq_ref[...], cos, s1, s2)
    kr = _rope(_head_rms(k, ones_ref) * gk_ref[...], cos, s1, s2)
    q_ref[0] = (qr * (HEAD_DIM ** -0.5)).astype(BF16)
    k_ref[0] = kr.astype(BF16)
    v_ref[0] = v.astype(BF16)
    rows = x.shape[0]
    nchunk = rows // SSM_L
    for o in range(u.shape[1] // LANES):
        uscr[o] = u[:, LANES * o:LANES * (o + 1)]
    for o in range(u.shape[1] // LANES):
        for hh in range(SSM_L // 8):
            outs = _granule_transpose([uscr[o, pl.ds(8 * hh + t, nchunk, stride=SSM_L), :] for t in range(8)])
            for g8 in range(8):
                u_ref[8 * o + g8, :, LANES * hh:LANES * (hh + 1)] = outs[g8].astype(BF16)
    klast_ref[0] = kr[rows - keep:, :]
    vlast_ref[0] = v[rows - keep:, :]


def _inproj_call(x, sc, sh, ln_g, w_in, gq, gk, ones, cos, s1, s2, *, tile, keep):
    nb, rows, d = x.shape
    in_cols = w_in.shape[1]
    q_cols = N_HEADS * HEAD_DIM
    kv_cols = N_KV_HEADS * HEAD_DIM
    u_cols = in_cols - q_cols - 2 * kv_cols
    mrows = sc.shape[1]
    mtile = 1 if mrows == 1 else tile
    mod_spec = pl.BlockSpec((1, mtile, d), (lambda b, i: (b, 0, 0)) if mrows == 1 else (lambda b, i: (b, i, 0)))
    row_spec = lambda c: pl.BlockSpec((1, tile, c), lambda b, i: (b, i, 0))
    tab_spec = pl.BlockSpec((tile, LANES), lambda b, i: (i, 0))
    last_spec = pl.BlockSpec((1, keep, kv_cols), lambda b, i: (b, 0, 0))
    groups = u_cols // SSM_CH
    tiles = rows // tile
    chunk_spec = pl.BlockSpec((groups, tile // SSM_L, SSM_L * SSM_CH), lambda b, i: (0, b * tiles + i, 0))
    kern = functools.partial(_inproj_kernel, q_cols=q_cols, kv_cols=kv_cols, keep=keep)
    return pl.pallas_call(
        kern,
        grid=(nb, tiles),
        in_specs=[row_spec(d), mod_spec, mod_spec, _const_spec((1, d)), _const_spec((d, in_cols)),
                  _const_spec((1, q_cols)), _const_spec((1, kv_cols)), _const_spec((256, 256)),
                  tab_spec, tab_spec, tab_spec],
        out_specs=[row_spec(q_cols), row_spec(kv_cols), row_spec(kv_cols), chunk_spec, last_spec, last_spec],
        out_shape=[jax.ShapeDtypeStruct((nb, rows, q_cols), BF16),
                   jax.ShapeDtypeStruct((nb, rows, kv_cols), BF16),
                   jax.ShapeDtypeStruct((nb, rows, kv_cols), BF16),
                   jax.ShapeDtypeStruct((groups, nb * rows // SSM_L, SSM_L * SSM_CH), BF16),
                   jax.ShapeDtypeStruct((nb, keep, kv_cols), F32),
                   jax.ShapeDtypeStruct((nb, keep, kv_cols), F32)],
        scratch_shapes=[pltpu.VMEM((u_cols // LANES, tile, LANES), F32)],
        compiler_params=_params(2),
        name="inproj",
    )(x, sc, sh, ln_g, w_in, gq, gk, ones, cos, s1, s2)


def _ssm_kernel(u_ref, h0_ref, wi_ref, toep_ref, cpow_ref, lam_ref, y_ref, hout_ref,
                s_re, s_im, hp_re, hp_im, *, nseq, nchunk):
    groups = u_ref.shape[0]
    rows = nseq * nchunk
    npair = groups // 2
    for p in range(npair):
        lhs = jnp.concatenate([u_ref[2 * p], u_ref[2 * p + 1]], axis=1)
        s = _dot(lhs, wi_ref[p])
        s_re[pl.ds(p, rows, stride=npair), :] = s[:, :LANES]
        s_im[pl.ds(p, rows, stride=npair), :] = s[:, LANES:]

    a_re = lam_ref[0]
    a_im = lam_ref[1]
    for b in range(nseq):
        def step(n, carry):
            h_re, h_im = carry
            r0 = pl.multiple_of((b * nchunk + n) * npair, npair)
            hp_re[pl.ds(r0, npair), :] = h_re
            hp_im[pl.ds(r0, npair), :] = h_im
            n_re = a_re * h_re - a_im * h_im + s_re[pl.ds(r0, npair), :]
            n_im = a_re * h_im + a_im * h_re + s_im[pl.ds(r0, npair), :]
            return n_re, n_im
        h_re, h_im = lax.fori_loop(0, nchunk, step, (h0_ref[b, 0], h0_ref[b, 1]))
        hout_ref[b, 0] = h_re
        hout_ref[b, 1] = h_im

    for g in range(groups):
        p = g // 2
        hp = jnp.concatenate([hp_re[pl.ds(p, rows, stride=npair), :],
                              hp_im[pl.ds(p, rows, stride=npair), :]], axis=1).astype(BF16)
        y = _dot(u_ref[g], toep_ref[g]) + _dot(hp, cpow_ref[g])
        y_ref[g] = y.astype(BF16)


def _ssm_call(u2, h0, wi, toep, cpow, lam, *, nseq, nchunk):
    groups, total_rows, width = u2.shape
    rows = nseq * nchunk
    npair = groups // 2
    kern = functools.partial(_ssm_kernel, nseq=nseq, nchunk=nchunk)
    return pl.pallas_call(
        kern,
        grid=(total_rows // rows,),
        in_specs=[pl.BlockSpec((groups, rows, width), lambda i: (0, i, 0)),
                  pl.BlockSpec((nseq, 2, npair, LANES), lambda i: (i, 0, 0, 0)),
                  _const_spec(wi.shape), _const_spec(toep.shape), _const_spec(cpow.shape), _const_spec(lam.shape)],
        out_specs=[pl.BlockSpec((groups, rows, width), lambda i: (0, i, 0)),
                   pl.BlockSpec((nseq, 2, npair, LANES), lambda i: (i, 0, 0, 0))],
        out_shape=[jax.ShapeDtypeStruct((groups, total_rows, width), BF16),
                   jax.ShapeDtypeStruct(h0.shape, F32)],
        scratch_shapes=[pltpu.VMEM((rows * npair, LANES), F32)] * 4,
        compiler_params=_params(1),
        name="ssm",
    )(u2, h0, wi, toep, cpow, lam)


def _kv_dup(a):
    lo = lax.broadcasted_iota(jnp.int32, a.shape, 1) < HEAD_DIM
    sw = pltpu.roll(a, HEAD_DIM, 1)
    return jnp.where(lo, a, sw), jnp.where(lo, sw, a)


def _v_ext(v):
    return jnp.concatenate([v, jnp.ones_like(v)], axis=1).astype(BF16)


def _sink_col(sink_ref, g, half_rows):
    rep = N_HEADS // N_KV_HEADS
    row = lax.broadcasted_iota(jnp.int32, (4 * half_rows, 1), 0)
    top = jnp.where(row < half_rows, sink_ref[rep * g], sink_ref[rep * g + 2])
    bot = jnp.where(row < 3 * half_rows, sink_ref[rep * g + 1], sink_ref[rep * g + 3])
    return jnp.where(row < 2 * half_rows, top, bot)


def _sink_attention(problems):
    scores = []
    for qa, qb, kb, vb, sink_col, valid, store in problems:
        qst = jnp.concatenate([qa, qb], axis=0)
        lo = lax.broadcasted_iota(jnp.int32, qst.shape, 1) < HEAD_DIM
        zero = jnp.zeros_like(qst)
        q4 = jnp.concatenate([jnp.where(lo, qst, zero), jnp.where(lo, zero, qst)], axis=0)
        s = lax.dot_general(q4, kb, (((1,), (1,)), ((), ())), preferred_element_type=F32)
        scores.append(s if valid is None else jnp.where(valid, s, -jnp.inf))
    probs = []
    for s, (_, _, _, _, sink_col, _, _) in zip(scores, problems):
        m = jnp.maximum(jnp.max(s, axis=-1, keepdims=True), sink_col)
        probs.append((jnp.exp(s - m).astype(BF16), jnp.exp(sink_col - m)))
    for (p, sink_mass), (qa, _, _, vb, _, _, store) in zip(probs, problems):
        o4 = _dot(p, vb)
        o = o4[:, :LANES] / (o4[:, LANES:] + sink_mass)
        half = qa.shape[0]
        lo = lax.broadcasted_iota(jnp.int32, (2 * half, LANES), 1) < HEAD_DIM
        store(jnp.where(lo, o[:2 * half], o[2 * half:]))


def _attn_prompt(q_ref, kc_ref, kp_ref, vc_ref, vp_ref, sink_ref, kz, vz, attn, tile):
    kd = _kv_dup(jnp.concatenate([kp_ref[0], kc_ref[0]], axis=0).astype(F32))
    vd = _kv_dup(jnp.concatenate([vp_ref[0], vc_ref[0]], axis=0).astype(F32))
    for g in range(2):
        kz[g] = kd[g].astype(BF16)
        vz[g] = _v_ext(vd[g])
    band = CHUNK + WINDOW
    col_chunk = lax.broadcasted_iota(jnp.int32, (1, band), 1) // CHUNK
    later_tile = pl.program_id(1) > 0
    problems = []
    for c in range(tile // CHUNK):
        r0 = c * CHUNK
        valid = None if c >= WINDOW // CHUNK else jnp.logical_or(col_chunk + c >= WINDOW // CHUNK, later_tile)
        for g in range(2):
            c0 = 2 * LANES * g

            def store(o, r0=r0, c0=c0):
                attn[r0:r0 + CHUNK, c0:c0 + LANES] = o[:CHUNK]
                attn[r0:r0 + CHUNK, c0 + LANES:c0 + 2 * LANES] = o[CHUNK:]

            problems.append((q_ref[0, r0:r0 + CHUNK, c0:c0 + LANES], q_ref[0, r0:r0 + CHUNK, c0 + LANES:c0 + 2 * LANES],
                             kz[g, r0:r0 + band, :], vz[g, r0:r0 + band, :], _sink_col(sink_ref, g, CHUNK),
                             valid, store))
    _sink_attention(problems)


def _attn_sample(q_ref, kc_ref, kp_ref, vc_ref, vp_ref, sink_ref, attn, nseq, seq):
    problems = []
    for b in range(nseq):
        r0 = b * seq
        kd = _kv_dup(jnp.concatenate([kp_ref[b], kc_ref[0, r0:r0 + seq, :]], axis=0).astype(F32))
        vd = _kv_dup(jnp.concatenate([vp_ref[b], vc_ref[0, r0:r0 + seq, :]], axis=0).astype(F32))
        for g in range(2):
            c0 = 2 * LANES * g

            def store(o, r0=r0, c0=c0):
                attn[r0:r0 + seq, c0:c0 + LANES] = o[:seq]
                attn[r0:r0 + seq, c0 + LANES:c0 + 2 * LANES] = o[seq:]

            problems.append((q_ref[0, r0:r0 + seq, c0:c0 + LANES], q_ref[0, r0:r0 + seq, c0 + LANES:c0 + 2 * LANES],
                             kd[g].astype(BF16), _v_ext(vd[g]), _sink_col(sink_ref, g, seq), None, store))
    _sink_attention(problems)


def _gelu_tanh(x):
    return 0.5 * x * (1.0 + jnp.tanh(math.sqrt(2.0 / math.pi) * (x + 0.044715 * (x * x * x))))


def _mix_kernel(sink_ref, x_ref, g1_ref, q_ref, kc_ref, kp_ref, vc_ref, vp_ref, y_ref,
                gluw_ref, glub_ref, ga_ref, gs_ref, wout_ref, o_ref, *scratch, tile, nseq):
    if nseq is None:
        yscr, kz, vz, attn = scratch
        _attn_prompt(q_ref, kc_ref, kp_ref, vc_ref, vp_ref, sink_ref, kz, vz, attn, tile)
    else:
        yscr, attn = scratch
        _attn_sample(q_ref, kc_ref, kp_ref, vc_ref, vp_ref, sink_ref, attn, nseq, tile // nseq)
    an = _rms(attn[...]) * ga_ref[...]
    nslab = yscr.shape[0]
    for o in range(nslab):
        for hh in range(SSM_L // 8):
            outs = _granule_transpose([y_ref[8 * o + g8, :, LANES * hh:LANES * (hh + 1)].astype(F32)
                                       for g8 in range(8)])
            for t8 in range(8):
                yscr[o, pl.ds(8 * hh + t8, tile // SSM_L, stride=SSM_L), :] = outs[t8]
    gl = _gelu_tanh(jnp.concatenate([yscr[o] for o in range(nslab)], axis=1))
    so = gl * jax.nn.sigmoid(_dot(gl.astype(BF16), gluw_ref[...]) + glub_ref[...])
    sn = _rms(so) * gs_ref[...]
    merged = jnp.concatenate([an, sn], axis=1).astype(BF16)
    o_ref[0] = x_ref[0] + g1_ref[0] * _dot(merged, wout_ref[...])


def _mix_call(sinks, x, g1, q, k, v, k_past, v_past, y, gluw, glub, ga, gs, wout, *, tile, nseq):
    nb, rows, d = x.shape
    aw = q.shape[2]
    kvw = k.shape[2]
    groups, _, cw = y.shape
    tiles = rows // tile
    mrows = g1.shape[1]
    mtile = 1 if mrows == 1 else tile
    mod_spec = pl.BlockSpec((1, mtile, d), (lambda b, i: (b, 0, 0)) if mrows == 1 else (lambda b, i: (b, i, 0)))
    row_spec = lambda c: pl.BlockSpec((1, tile, c), lambda b, i: (b, i, 0))
    chunk_spec = pl.BlockSpec((groups, tile // SSM_L, cw), lambda b, i: (0, b * tiles + i, 0))
    scratch = [pltpu.VMEM((groups * SSM_CH // LANES, tile, LANES), F32)]
    if nseq is None:
        wpt = tile // WINDOW
        past_spec = pl.BlockSpec((1, WINDOW, kvw), lambda b, i: (b, jnp.maximum(i * wpt - 1, 0), 0))
        k_past, v_past = k, v
        scratch += [pltpu.VMEM((2, tile + WINDOW, LANES), BF16), pltpu.VMEM((2, tile + WINDOW, 2 * LANES), BF16),
                    pltpu.VMEM((tile, aw), F32)]
    else:
        past_spec = pl.BlockSpec((nseq, WINDOW, kvw), lambda b, i: (0, 0, 0))
        scratch += [pltpu.VMEM((tile, aw), F32)]
    kern = functools.partial(_mix_kernel, tile=tile, nseq=nseq)
    return pl.pallas_call(
        kern,
        grid=(nb, tiles),
        in_specs=[pl.BlockSpec(memory_space=pltpu.SMEM),
                  row_spec(d), mod_spec, row_spec(aw), row_spec(kvw), past_spec, row_spec(kvw), past_spec,
                  chunk_spec, _const_spec(gluw.shape), _const_spec(glub.shape), _const_spec(ga.shape),
                  _const_spec(gs.shape), _const_spec(wout.shape)],
        out_specs=row_spec(d),
        out_shape=jax.ShapeDtypeStruct((nb, rows, d), F32),
        scratch_shapes=scratch,
        compiler_params=_params(2),
        name="mix",
    )(sinks, x, g1, q, k, k_past, v, v_past, y, gluw, glub, ga, gs, wout)


def _ffn_kernel(x_ref, sc_ref, sh_ref, g2_ref, ln_ref, wg_ref, wu_ref, wd_ref, o_ref, *, ff_chunk):
    x = x_ref[0]
    h = ((_rms(x) * ln_ref[...]) * (1.0 + sc_ref[0]) + sh_ref[0]).astype(BF16)
    acc = None
    for c0 in range(0, wg_ref.shape[1], ff_chunk):
        a = _dot(h, wg_ref[:, c0:c0 + ff_chunk])
        b = _dot(h, wu_ref[:, c0:c0 + ff_chunk])
        part = _dot((a * jax.nn.sigmoid(a) * b).astype(BF16), wd_ref[c0:c0 + ff_chunk, :])
        acc = part if acc is None else acc + part
    o_ref[0] = x + g2_ref[0] * acc


def _ffn_call(x, sc, sh, g2, ln_g, wg, wu, wd, *, tile):
    nb, rows, d = x.shape
    dff = wg.shape[1]
    ff_chunk = dff // 2 if (dff // 2) % LANES == 0 else dff
    mrows = sc.shape[1]
    mtile = 1 if mrows == 1 else tile
    mod_spec = pl.BlockSpec((1, mtile, d), (lambda b, i: (b, 0, 0)) if mrows == 1 else (lambda b, i: (b, i, 0)))
    row_spec = pl.BlockSpec((1, tile, d), lambda b, i: (b, i, 0))
    kern = functools.partial(_ffn_kernel, ff_chunk=ff_chunk)
    return pl.pallas_call(
        kern,
        grid=(nb, rows // tile),
        in_specs=[row_spec, mod_spec, mod_spec, mod_spec, _const_spec((1, d)),
                  _const_spec(wg.shape), _const_spec(wu.shape), _const_spec(wd.shape)],
        out_specs=row_spec,
        out_shape=jax.ShapeDtypeStruct((nb, rows, d), F32),
        compiler_params=_params(2),
        name="ffn",
    )(x, sc, sh, g2, ln_g, wg, wu, wd)


def _ssm_weights(a_re, a_im, log_dt, b_re, b_im, c_re, c_im, d_skip):
    hp = lax.Precision.HIGHEST
    groups, state = a_re.shape
    ch = b_re.shape[2]
    dt = jnp.exp(log_dt)[:, None]
    lam = lax.complex(a_re, a_im)
    lam_bar = jnp.exp(lam * dt)
    coef = (lam_bar - 1.0) / lam
    bb = coef[:, :, None] * lax.complex(b_re, b_im)
    j = jnp.arange(SSM_L + 1, dtype=F32)[:, None, None]
    lp = jnp.exp((lam * dt)[None] * j)
    m = lp[:SSM_L, :, :, None] * bb[None]
    m_re, m_im = jnp.real(m), jnp.imag(m)
    kern = (jnp.einsum('gop,jgpi->gjoi', c_re, m_re, precision=hp)
            - jnp.einsum('gop,jgpi->gjoi', c_im, m_im, precision=hp))
    kern = kern.at[:, 0].add(jax.vmap(jnp.diag)(d_skip.reshape(groups, ch)))
    lag = jnp.arange(SSM_L)[None, :] - jnp.arange(SSM_L)[:, None]
    toep = jnp.where((lag >= 0)[None, :, :, None, None],
                     jnp.transpose(kern[:, jnp.clip(lag, 0)], (0, 1, 2, 4, 3)), 0.0)
    toep = jnp.transpose(toep, (0, 1, 3, 2, 4)).reshape(groups, SSM_L * ch, SSM_L * ch)
    wi = jnp.transpose(m[::-1], (1, 0, 3, 2)).reshape(groups, SSM_L * ch, state)
    zero = jnp.zeros_like(jnp.real(wi))
    wi_re, wi_im = jnp.real(wi), jnp.imag(wi)
    top = jnp.concatenate([wi_re[0::2], zero[0::2], wi_im[0::2], zero[0::2]], axis=2)
    bot = jnp.concatenate([zero[1::2], wi_re[1::2], zero[1::2], wi_im[1::2]], axis=2)
    wi_pair = jnp.concatenate([top, bot], axis=1)
    cl = lax.complex(c_re, c_im)[:, None] * lp[1:, :, None, :].transpose(1, 0, 2, 3)
    cp_re = jnp.transpose(jnp.real(cl), (0, 3, 1, 2)).reshape(groups, state, SSM_L * ch)
    cp_im = jnp.transpose(jnp.imag(cl), (0, 3, 1, 2)).reshape(groups, state, SSM_L * ch)
    zc = jnp.zeros_like(cp_re)
    even = jnp.concatenate([cp_re, zc, -cp_im, zc], axis=1)
    odd = jnp.concatenate([zc, cp_re, zc, -cp_im], axis=1)
    cpow = jnp.where((jnp.arange(groups) % 2 == 0)[:, None, None], even, odd)
    lam_l = lp[SSM_L].reshape(groups // 2, 2 * state)
    lam16 = jnp.stack([jnp.real(lam_l), jnp.imag(lam_l)])
    return wi_pair.astype(BF16), toep.astype(BF16), cpow.astype(BF16), lam16


def _rope_tables(pos):
    half = HEAD_DIM // 2
    inv = ROPE_THETA ** (-jnp.arange(half, dtype=F32) * 2.0 / HEAD_DIM)
    lane = jnp.arange(LANES)
    ang = pos.astype(F32)[:, None] * inv[lane % half][None, :]
    first = ((lane % HEAD_DIM) < half)[None, :]
    sin = jnp.sin(ang)
    return jnp.cos(ang), jnp.where(first, -sin, 0.0), jnp.where(first, 0.0, sin)


def _state_in(state):
    b, groups, p, _ = state.shape
    return jnp.transpose(state, (0, 3, 1, 2)).reshape(b, 2, groups // 2, 2 * p)


def _state_out(h, groups):
    b = h.shape[0]
    return jnp.transpose(h.reshape(b, 2, groups, -1), (0, 2, 3, 1))


def _stream(x, mods, tabs, past, h0, lw, *, tile, keep, nseq_tile, seqs, seq_len):
    sh1, sc1, g1, sh2, sc2, g2 = mods
    nb, rows, d = x.shape
    groups = lw['groups']
    q, k, v, u2, k_last, v_last = _inproj_call(x, sc1, sh1, lw['ln1'], lw['w_in'], lw['gq'], lw['gk'], lw['ones'],
                                               *tabs, tile=tile, keep=keep)
    y2, h_last = _ssm_call(u2, h0, lw['wi'], lw['toep'], lw['cpow'], lw['lam'],
                           nseq=(1 if nseq_tile is None else nseq_tile), nchunk=seq_len // SSM_L)
    kp, vp = (None, None) if past is None else past
    x1 = _mix_call(lw['sinks'], x, g1, q, k, v, kp, vp, y2, lw['gluw'], lw['glub'], lw['ga'], lw['gs'],
                   lw['w_out'], tile=tile, nseq=nseq_tile)
    out = _ffn_call(x1, sc2, sh2, g2, lw['ln2'], lw['wg'], lw['wu'], lw['wd'], tile=tile)
    return out, k_last, v_last, h_last


def kernel(x_prompt, x_sample, cache_k, cache_v, state_ssm, c_prompt, c_sample, w_ada, b_ada, ln1_g, w_in, q_norm_g, k_norm_g, attn_sinks, ssm_A_re, ssm_A_im, ssm_log_dt, ssm_B_re, ssm_B_im, ssm_C_re, ssm_C_im, ssm_D, ssm_glu_w, ssm_glu_b, attn_out_g, ssm_out_g, w_out, ln2_g, w_gate, w_up, w_down):
    depth = w_ada.shape[0]
    bp, sp, d = x_prompt.shape
    bs, ss, _ = x_sample.shape
    groups = ssm_A_re.shape[1]
    kvw = N_KV_HEADS * HEAD_DIM
    tile_p = min(512, sp)
    seg = jnp.arange(256) // HEAD_DIM
    ones = (seg[:, None] == seg[None, :]).astype(BF16)
    tabs_p = _rope_tables(jnp.arange(sp))
    tabs_s = tuple(jnp.tile(t, (bs, 1)) for t in _rope_tables(PAST_LEN + jnp.arange(ss)))

    yp = x_prompt
    ys = x_sample.reshape(1, bs * ss, d)
    outs = [[] for _ in range(6)]
    for l in range(depth):
        wi, toep, cpow, lam = _ssm_weights(ssm_A_re[l], ssm_A_im[l], ssm_log_dt[l], ssm_B_re[l], ssm_B_im[l],
                                           ssm_C_re[l], ssm_C_im[l], ssm_D[l])
        lw = dict(groups=groups, ln1=ln1_g[l][None], w_in=w_in[l].astype(BF16),
                  gq=jnp.tile(q_norm_g[l], N_HEADS)[None], gk=jnp.tile(k_norm_g[l], N_KV_HEADS)[None], ones=ones,
                  wi=wi, toep=toep, cpow=cpow, lam=lam, sinks=attn_sinks[l],
                  gluw=ssm_glu_w[l].astype(BF16), glub=ssm_glu_b[l][None], ga=attn_out_g[l][None],
                  gs=ssm_out_g[l][None], w_out=w_out[l].astype(BF16), ln2=ln2_g[l][None],
                  wg=w_gate[l].astype(BF16), wu=w_up[l].astype(BF16), wd=w_down[l].astype(BF16))
        mod = _mod_call(jnp.concatenate([c_prompt, c_sample], axis=0), w_ada[l], b_ada[l])
        mods_p = tuple(m[:, None, :] for m in jnp.split(mod[:bp], 6, axis=-1))
        mods_s = tuple(jnp.repeat(m, ss, axis=0)[None] for m in jnp.split(mod[bp:], 6, axis=-1))

        h0_p = jnp.zeros((bp, 2, groups // 2, 2 * SSM_STATE), F32)
        yp, kpl, vpl, hpl = _stream(yp, mods_p, tabs_p, None, h0_p, lw, tile=tile_p, keep=WINDOW,
                                    nseq_tile=None, seqs=bp, seq_len=sp)
        past = (cache_k[l].reshape(bs, WINDOW, kvw).astype(BF16), cache_v[l].reshape(bs, WINDOW, kvw).astype(BF16))
        ys, ksl, vsl, hsl = _stream(ys, mods_s, tabs_s, past, _state_in(state_ssm[l]), lw, tile=bs * ss,
                                    keep=bs * ss, nseq_tile=bs, seqs=bs, seq_len=ss)
        outs[0].append(kpl.reshape(bp, WINDOW, N_KV_HEADS, HEAD_DIM))
        outs[1].append(vpl.reshape(bp, WINDOW, N_KV_HEADS, HEAD_DIM))
        outs[2].append(_state_out(hpl, groups))
        outs[3].append(ksl.reshape(bs, ss, N_KV_HEADS, HEAD_DIM))
        outs[4].append(vsl.reshape(bs, ss, N_KV_HEADS, HEAD_DIM))
        outs[5].append(_state_out(hsl, groups))
    return (yp, ys.reshape(bs, ss, d)) + tuple(jnp.stack(o) for o in outs)
```

```python
import functools
import math

import jax
import jax.numpy as jnp
from jax import lax
from jax.experimental import pallas as pl
from jax.experimental.pallas import tpu as pltpu

F32 = jnp.float32
BF16 = jnp.bfloat16

CHUNK = 64
WINDOW = 128
HEAD_DIM = 64
N_HEADS = 8
N_KV_HEADS = 2
SSM_CH = 16
SSM_STATE = 64
SSM_L = 16
ROPE_THETA = 10000.0
EPS = 1e-6
PAST_LEN = 1024
LANES = 128
MXU_TILE = 256
LOG2E = math.log2(math.e)
VMEM_LIMIT = 56 * 1024 * 1024


def _const_spec(shape):
    nd = len(shape)
    return pl.BlockSpec(shape, lambda *_: (0,) * nd, pipeline_mode=pl.Buffered(1))


def _params(n_grid):
    return pltpu.CompilerParams(dimension_semantics=("arbitrary",) * n_grid, vmem_limit_bytes=VMEM_LIMIT)


def _rms(x):
    return x * lax.rsqrt(jnp.mean(x * x, axis=-1, keepdims=True) + EPS)


def _dot(a, b):
    return jnp.dot(a, b, preferred_element_type=F32)


def _mod_kernel(c_ref, w_ref, b_ref, o_ref):
    c = c_ref[...]
    s = (c * jax.nn.sigmoid(c)).astype(BF16)
    o_ref[...] = _dot(s, w_ref[...].astype(BF16)) + b_ref[...]


def _mod_call(c, w, b):
    rows, d = c.shape
    cols = w.shape[1]
    tile = 1536
    return pl.pallas_call(
        _mod_kernel,
        grid=(cols // tile,),
        in_specs=[pl.BlockSpec((rows, d), lambda j: (0, 0)),
                  pl.BlockSpec((d, tile), lambda j: (0, j)),
                  pl.BlockSpec((1, tile), lambda j: (0, j))],
        out_specs=pl.BlockSpec((rows, tile), lambda j: (0, j)),
        out_shape=jax.ShapeDtypeStruct((rows, cols), F32),
        compiler_params=_params(1),
        name="mod",
    )(c, w, b.reshape(1, cols))


def _head_rms(t, ones_ref):
    width = t.shape[1]
    sq = t * t
    hi = sq.astype(BF16)
    lo = (sq - hi.astype(F32)).astype(BF16)
    parts = []
    for c0 in range(0, width, 256):
        w = min(256, width - c0)
        ones = ones_ref[:w, :w]
        parts.append(_dot(hi[:, c0:c0 + w], ones) + _dot(lo[:, c0:c0 + w], ones))
    ssq = parts[0] if len(parts) == 1 else jnp.concatenate(parts, axis=1)
    return t * lax.rsqrt(ssq * (1.0 / HEAD_DIM) + EPS)


def _rope(t, cos, s1, s2):
    outs = []
    for c0 in range(0, t.shape[1], LANES):
        xb = t[:, c0:c0 + LANES]
        outs.append(xb * cos + pltpu.roll(xb, LANES - HEAD_DIM // 2, 1) * s1 + pltpu.roll(xb, HEAD_DIM // 2, 1) * s2)
    return outs[0] if len(outs) == 1 else jnp.concatenate(outs, axis=1)


def _granule_transpose(arrs):
    gran = lax.broadcasted_iota(jnp.int32, arrs[0].shape, 1) // SSM_CH
    cur = list(arrs)
    for s in (4, 2, 1):
        upper = (gran & s) != 0
        nxt = list(cur)
        for a0 in range(8):
            if a0 & s:
                continue
            lo, hi = cur[a0], cur[a0 + s]
            nxt[a0] = jnp.where(upper, pltpu.roll(hi, SSM_CH * s, 1), lo)
            nxt[a0 + s] = jnp.where(upper, hi, pltpu.roll(lo, LANES - SSM_CH * s, 1))
        cur = nxt
    return cur


def _inproj_kernel(x_ref, sc_ref, sh_ref, ln_ref, w_ref, gq_ref, gk_ref, ones_ref, cos_ref, s1_ref, s2_ref,
                   q_ref, k_ref, v_ref, u_ref, klast_ref, vlast_ref, uscr, *, q_cols, kv_cols, keep):
    x = x_ref[0]
    h = _rms(x) * (ln_ref[...] * (1.0 + sc_ref[0])) + sh_ref[0]
    proj = _dot(h.astype(BF16), w_ref[...])
    q = proj[:, :q_cols]
    k = proj[:, q_cols:q_cols + kv_cols]
    v = proj[:, q_cols + kv_cols:q_cols + 2 * kv_cols]
    u = proj[:, q_cols + 2 * kv_cols:]
    cos, s1, s2 = cos_ref[...], s1_ref[...], s2_ref[...]
    qr = _rope(_head_rms(q, ones_ref) * gq_ref[...], cos, s1, s2)
    kr = _rope(_head_rms(k, ones_ref) * gk_ref[...], cos, s1, s2)
    q_ref[0] = qr.astype(BF16)
    k_ref[0] = kr.astype(BF16)
    v_ref[0] = v.astype(BF16)
    rows = x.shape[0]
    nchunk = rows // SSM_L
    for o in range(u.shape[1] // LANES):
        uscr[o] = u[:, LANES * o:LANES * (o + 1)]
    for o in range(u.shape[1] // LANES):
        for hh in range(SSM_L // 8):
            outs = _granule_transpose([uscr[o, pl.ds(8 * hh + t, nchunk, stride=SSM_L), :] for t in range(8)])
            for g8 in range(8):
                u_ref[8 * o + g8, :, LANES * hh:LANES * (hh + 1)] = outs[g8].astype(BF16)
    klast_ref[0] = kr[rows - keep:, :]
    vlast_ref[0] = v[rows - keep:, :]


def _inproj_call(x, sc, sh, ln_g, w_in, gq, gk, ones, cos, s1, s2, *, tile, keep):
    nb, rows, d = x.shape
    in_cols = w_in.shape[1]
    q_cols = N_HEADS * HEAD_DIM
    kv_cols = N_KV_HEADS * HEAD_DIM
    u_cols = in_cols - q_cols - 2 * kv_cols
    mrows = sc.shape[1]
    mtile = 1 if mrows == 1 else tile
    mod_spec = pl.BlockSpec((1, mtile, d), (lambda b, i: (b, 0, 0)) if mrows == 1 else (lambda b, i: (b, i, 0)))
    row_spec = lambda c: pl.BlockSpec((1, tile, c), lambda b, i: (b, i, 0))
    tab_spec = pl.BlockSpec((tile, LANES), lambda b, i: (i, 0))
    last_spec = pl.BlockSpec((1, keep, kv_cols), lambda b, i: (b, 0, 0))
    groups = u_cols // SSM_CH
    tiles = rows // tile
    chunk_spec = pl.BlockSpec((groups, tile // SSM_L, SSM_L * SSM_CH), lambda b, i: (0, b * tiles + i, 0))
    kern = functools.partial(_inproj_kernel, q_cols=q_cols, kv_cols=kv_cols, keep=keep)
    return pl.pallas_call(
        kern,
        grid=(nb, tiles),
        in_specs=[row_spec(d), mod_spec, mod_spec, _const_spec((1, d)), _const_spec((d, in_cols)),
                  _const_spec((1, q_cols)), _const_spec((1, kv_cols)), _const_spec((256, 256)),
                  tab_spec, tab_spec, tab_spec],
        out_specs=[row_spec(q_cols), row_spec(kv_cols), row_spec(kv_cols), chunk_spec, last_spec, last_spec],
        out_shape=[jax.ShapeDtypeStruct((nb, rows, q_cols), BF16),
                   jax.ShapeDtypeStruct((nb, rows, kv_cols), BF16),
                   jax.ShapeDtypeStruct((nb, rows, kv_cols), BF16),
                   jax.ShapeDtypeStruct((groups, nb * rows // SSM_L, SSM_L * SSM_CH), BF16),
                   jax.ShapeDtypeStruct((nb, keep, kv_cols), F32),
                   jax.ShapeDtypeStruct((nb, keep, kv_cols), F32)],
        scratch_shapes=[pltpu.VMEM((u_cols // LANES, tile, LANES), F32)],
        compiler_params=_params(2),
        name="inproj",
    )(x, sc, sh, ln_g, w_in, gq, gk, ones, cos, s1, s2)


def _ssm_kernel(u_ref, h0_ref, wi_ref, toep_ref, cpow_ref, lam_ref, y_ref, hout_ref,
                s_re, s_im, hp_re, hp_im, *, nseq, nchunk):
    groups = u_ref.shape[0]
    rows = nseq * nchunk
    npair = groups // 2
    for p in range(npair):
        lhs = jnp.concatenate([u_ref[2 * p], u_ref[2 * p + 1]], axis=1)
        s = _dot(lhs, wi_ref[p])
        s_re[pl.ds(p, rows, stride=npair), :] = s[:, :LANES]
        s_im[pl.ds(p, rows, stride=npair), :] = s[:, LANES:]

    a_re = lam_ref[0]
    a_im = lam_ref[1]
    for b in range(nseq):
        def step(n, carry):
            h_re, h_im = carry
            r0 = pl.multiple_of((b * nchunk + n) * npair, npair)
            hp_re[pl.ds(r0, npair), :] = h_re
            hp_im[pl.ds(r0, npair), :] = h_im
            n_re = a_re * h_re - a_im * h_im + s_re[pl.ds(r0, npair), :]
            n_im = a_re * h_im + a_im * h_re + s_im[pl.ds(r0, npair), :]
            return n_re, n_im
        h_re, h_im = lax.fori_loop(0, nchunk, step, (h0_ref[b, 0], h0_ref[b, 1]))
        hout_ref[b, 0] = h_re
        hout_ref[b, 1] = h_im

    for g in range(groups):
        p = g // 2
        hp = jnp.concatenate([hp_re[pl.ds(p, rows, stride=npair), :],
                              hp_im[pl.ds(p, rows, stride=npair), :]], axis=1).astype(BF16)
        y = _dot(u_ref[g], toep_ref[g]) + _dot(hp, cpow_ref[g])
        y_ref[g] = y.astype(BF16)


def _ssm_call(u2, h0, wi, toep, cpow, lam, *, nseq, nchunk):
    groups, total_rows, width = u2.shape
    rows = nseq * nchunk
    npair = groups // 2
    kern = functools.partial(_ssm_kernel, nseq=nseq, nchunk=nchunk)
    return pl.pallas_call(
        kern,
        grid=(total_rows // rows,),
        in_specs=[pl.BlockSpec((groups, rows, width), lambda i: (0, i, 0)),
                  pl.BlockSpec((nseq, 2, npair, LANES), lambda i: (i, 0, 0, 0)),
                  _const_spec(wi.shape), _const_spec(toep.shape), _const_spec(cpow.shape), _const_spec(lam.shape)],
        out_specs=[pl.BlockSpec((groups, rows, width), lambda i: (0, i, 0)),
                   pl.BlockSpec((nseq, 2, npair, LANES), lambda i: (i, 0, 0, 0))],
        out_shape=[jax.ShapeDtypeStruct((groups, total_rows, width), BF16),
                   jax.ShapeDtypeStruct(h0.shape, F32)],
        scratch_shapes=[pltpu.VMEM((rows * npair, LANES), F32)] * 4,
        compiler_params=_params(1),
        name="ssm",
    )(u2, h0, wi, toep, cpow, lam)


def _kv_dup(a):
    lo = lax.broadcasted_iota(jnp.int32, a.shape, 1) < HEAD_DIM
    sw = pltpu.roll(a, HEAD_DIM, 1)
    return jnp.where(lo, a, sw), jnp.where(lo, sw, a)


def _v_ext(v):
    return jnp.concatenate([v, jnp.ones_like(v)], axis=1).astype(BF16)


def _sink_col(sink_ref, g, half_rows):
    rep = N_HEADS // N_KV_HEADS
    row = lax.broadcasted_iota(jnp.int32, (4 * half_rows, 1), 0)
    top = jnp.where(row < half_rows, sink_ref[rep * g], sink_ref[rep * g + 2])
    bot = jnp.where(row < 3 * half_rows, sink_ref[rep * g + 1], sink_ref[rep * g + 3])
    return jnp.where(row < 2 * half_rows, top, bot) * LOG2E


def _sink_attention(problems):
    scores = []
    for qa, qb, kb, vb, sink_col, valid, store in problems:
        qst = jnp.concatenate([qa, qb], axis=0)
        lo = lax.broadcasted_iota(jnp.int32, qst.shape, 1) < HEAD_DIM
        zero = jnp.zeros_like(qst)
        q4 = jnp.concatenate([jnp.where(lo, qst, zero), jnp.where(lo, zero, qst)], axis=0)
        s = lax.dot_general(q4, kb, (((1,), (1,)), ((), ())), preferred_element_type=F32)
        scores.append(s if valid is None else jnp.where(valid, s, -jnp.inf))
    probs = []
    for s, (_, _, _, _, sink_col, _, _) in zip(scores, problems):
        m = jnp.maximum(jnp.max(s, axis=-1, keepdims=True), sink_col)
        probs.append((jnp.exp2(s - m).astype(BF16), jnp.exp2(sink_col - m)))
    for (p, sink_mass), (qa, _, _, vb, _, _, store) in zip(probs, problems):
        o4 = _dot(p, vb)
        o = o4[:, :LANES] / (o4[:, LANES:] + sink_mass)
        half = qa.shape[0]
        lo = lax.broadcasted_iota(jnp.int32, (2 * half, LANES), 1) < HEAD_DIM
        store(jnp.where(lo, o[:2 * half], o[2 * half:]))


def _attn_prompt(q_ref, kc_ref, kp_ref, vc_ref, vp_ref, sink_ref, kz, vz, attn, tile):
    kd = _kv_dup(jnp.concatenate([kp_ref[0], kc_ref[0]], axis=0).astype(F32))
    vd = _kv_dup(jnp.concatenate([vp_ref[0], vc_ref[0]], axis=0).astype(F32))
    for g in range(2):
        kz[g] = kd[g].astype(BF16)
        vz[g] = _v_ext(vd[g])
    band = CHUNK + WINDOW
    col_chunk = lax.broadcasted_iota(jnp.int32, (1, band), 1) // CHUNK
    later_tile = pl.program_id(1) > 0
    problems = []
    for c in range(tile // CHUNK):
        r0 = c * CHUNK
        valid = None if c >= WINDOW // CHUNK else jnp.logical_or(col_chunk + c >= WINDOW // CHUNK, later_tile)
        for g in range(2):
            c0 = 2 * LANES * g

            def store(o, r0=r0, c0=c0):
                attn[r0:r0 + CHUNK, c0:c0 + LANES] = o[:CHUNK]
                attn[r0:r0 + CHUNK, c0 + LANES:c0 + 2 * LANES] = o[CHUNK:]

            problems.append((q_ref[0, r0:r0 + CHUNK, c0:c0 + LANES], q_ref[0, r0:r0 + CHUNK, c0 + LANES:c0 + 2 * LANES],
                             kz[g, r0:r0 + band, :], vz[g, r0:r0 + band, :], _sink_col(sink_ref, g, CHUNK),
                             valid, store))
    _sink_attention(problems)


def _attn_sample(q_ref, kc_ref, kp_ref, vc_ref, vp_ref, sink_ref, attn, nseq, seq):
    problems = []
    for b in range(nseq):
        r0 = b * seq
        kd = _kv_dup(jnp.concatenate([kp_ref[b], kc_ref[0, r0:r0 + seq, :]], axis=0).astype(F32))
        vd = _kv_dup(jnp.concatenate([vp_ref[b], vc_ref[0, r0:r0 + seq, :]], axis=0).astype(F32))
        for g in range(2):
            c0 = 2 * LANES * g

            def store(o, r0=r0, c0=c0):
                attn[r0:r0 + seq, c0:c0 + LANES] = o[:seq]
                attn[r0:r0 + seq, c0 + LANES:c0 + 2 * LANES] = o[seq:]

            problems.append((q_ref[0, r0:r0 + seq, c0:c0 + LANES], q_ref[0, r0:r0 + seq, c0 + LANES:c0 + 2 * LANES],
                             kd[g].astype(BF16), _v_ext(vd[g]), _sink_col(sink_ref, g, seq), None, store))
    _sink_attention(problems)


def _gelu_tanh(x):
    return 0.5 * x * (1.0 + jnp.tanh(math.sqrt(2.0 / math.pi) * (x + 0.044715 * (x * x * x))))


def _mix_kernel(sink_ref, x_ref, g1_ref, q_ref, kc_ref, kp_ref, vc_ref, vp_ref, y_ref,
                gluw_ref, glub_ref, ga_ref, gs_ref, wout_ref, o_ref, *scratch, tile, nseq):
    if nseq is None:
        yscr, kz, vz, attn = scratch
        _attn_prompt(q_ref, kc_ref, kp_ref, vc_ref, vp_ref, sink_ref, kz, vz, attn, tile)
    else:
        yscr, attn = scratch
        _attn_sample(q_ref, kc_ref, kp_ref, vc_ref, vp_ref, sink_ref, attn, nseq, tile // nseq)
    an = _rms(attn[...]) * ga_ref[...]
    nslab = yscr.shape[0]
    for o in range(nslab):
        for hh in range(SSM_L // 8):
            outs = _granule_transpose([y_ref[8 * o + g8, :, LANES * hh:LANES * (hh + 1)].astype(F32)
                                       for g8 in range(8)])
            for t8 in range(8):
                yscr[o, pl.ds(8 * hh + t8, tile // SSM_L, stride=SSM_L), :] = outs[t8]
    gl = _gelu_tanh(jnp.concatenate([yscr[o] for o in range(nslab)], axis=1))
    so = gl * jax.nn.sigmoid(_dot(gl.astype(BF16), gluw_ref[...]) + glub_ref[...])
    sn = _rms(so) * gs_ref[...]
    merged = jnp.concatenate([an, sn], axis=1).astype(BF16)
    o_ref[0] = x_ref[0] + g1_ref[0] * _dot(merged, wout_ref[...])


def _mix_call(sinks, x, g1, q, k, v, k_past, v_past, y, gluw, glub, ga, gs, wout, *, tile, nseq):
    nb, rows, d = x.shape
    aw = q.shape[2]
    kvw = k.shape[2]
    groups, _, cw = y.shape
    tiles = rows // tile
    mrows = g1.shape[1]
    mtile = 1 if mrows == 1 else tile
    mod_spec = pl.BlockSpec((1, mtile, d), (lambda b, i: (b, 0, 0)) if mrows == 1 else (lambda b, i: (b, i, 0)))
    row_spec = lambda c: pl.BlockSpec((1, tile, c), lambda b, i: (b, i, 0))
    chunk_spec = pl.BlockSpec((groups, tile // SSM_L, cw), lambda b, i: (0, b * tiles + i, 0))
    scratch = [pltpu.VMEM((groups * SSM_CH // LANES, tile, LANES), F32)]
    if nseq is None:
        wpt = tile // WINDOW
        past_spec = pl.BlockSpec((1, WINDOW, kvw), lambda b, i: (b, jnp.maximum(i * wpt - 1, 0), 0))
        k_past, v_past = k, v
        scratch += [pltpu.VMEM((2, tile + WINDOW, LANES), BF16), pltpu.VMEM((2, tile + WINDOW, 2 * LANES), BF16),
                    pltpu.VMEM((tile, aw), F32)]
    else:
        past_spec = pl.BlockSpec((nseq, WINDOW, kvw), lambda b, i: (0, 0, 0))
        scratch += [pltpu.VMEM((tile, aw), F32)]
    kern = functools.partial(_mix_kernel, tile=tile, nseq=nseq)
    return pl.pallas_call(
        kern,
        grid=(nb, tiles),
        in_specs=[pl.BlockSpec(memory_space=pltpu.SMEM),
                  row_spec(d), mod_spec, row_spec(aw), row_spec(kvw), past_spec, row_spec(kvw), past_spec,
                  chunk_spec, _const_spec(gluw.shape), _const_spec(glub.shape), _const_spec(ga.shape),
                  _const_spec(gs.shape), _const_spec(wout.shape)],
        out_specs=row_spec(d),
        out_shape=jax.ShapeDtypeStruct((nb, rows, d), F32),
        scratch_shapes=scratch,
        compiler_params=_params(2),
        name="mix",
    )(sinks, x, g1, q, k, k_past, v, v_past, y, gluw, glub, ga, gs, wout)


def _ffn_kernel(x_ref, sc_ref, sh_ref, g2_ref, ln_ref, wg_ref, wu_ref, wd_ref, o_ref, *, ff_bounds):
    x = x_ref[0]
    h = (_rms(x) * (ln_ref[...] * (1.0 + sc_ref[0])) + sh_ref[0]).astype(BF16)
    acc = None
    for c0, c1 in zip(ff_bounds[:-1], ff_bounds[1:]):
        a = _dot(h, wg_ref[:, c0:c1])
        b = _dot(h, wu_ref[:, c0:c1])
        part = _dot((a * jax.nn.sigmoid(a) * b).astype(BF16), wd_ref[c0:c1, :])
        acc = part if acc is None else acc + part
    o_ref[0] = x + g2_ref[0] * acc


def _ffn_call(x, sc, sh, g2, ln_g, wg, wu, wd, *, tile):
    nb, rows, d = x.shape
    dff = wg.shape[1]
    split = -(-dff // (2 * MXU_TILE)) * MXU_TILE
    ff_bounds = (0, split, dff) if split < dff else (0, dff)
    mrows = sc.shape[1]
    mtile = 1 if mrows == 1 else tile
    mod_spec = pl.BlockSpec((1, mtile, d), (lambda b, i: (b, 0, 0)) if mrows == 1 else (lambda b, i: (b, i, 0)))
    row_spec = pl.BlockSpec((1, tile, d), lambda b, i: (b, i, 0))
    kern = functools.partial(_ffn_kernel, ff_bounds=ff_bounds)
    return pl.pallas_call(
        kern,
        grid=(nb, rows // tile),
        in_specs=[row_spec, mod_spec, mod_spec, mod_spec, _const_spec((1, d)),
                  _const_spec(wg.shape), _const_spec(wu.shape), _const_spec(wd.shape)],
        out_specs=row_spec,
        out_shape=jax.ShapeDtypeStruct((nb, rows, d), F32),
        compiler_params=_params(2),
        name="ffn",
    )(x, sc, sh, g2, ln_g, wg, wu, wd)


def _ssm_weights(a_re, a_im, log_dt, b_re, b_im, c_re, c_im, d_skip):
    hp = lax.Precision.HIGHEST
    groups, state = a_re.shape
    ch = b_re.shape[2]
    n = SSM_L
    dt = jnp.exp(log_dt)[:, None]
    z_re, z_im = a_re * dt, a_im * dt
    e_re = jnp.expm1(z_re) * jnp.cos(z_im) - 2.0 * jnp.sin(0.5 * z_im) ** 2
    e_im = jnp.exp(z_re) * jnp.sin(z_im)
    mag = a_re * a_re + a_im * a_im
    coef_re = (e_re * a_re + e_im * a_im) / mag
    coef_im = (e_im * a_re - e_re * a_im) / mag
    bt_re, bt_im = jnp.transpose(b_re, (0, 2, 1)), jnp.transpose(b_im, (0, 2, 1))
    bb_re = coef_re[:, None] * bt_re - coef_im[:, None] * bt_im
    bb_im = coef_re[:, None] * bt_im + coef_im[:, None] * bt_re
    j = jnp.arange(n + 1, dtype=F32)[None, :, None]
    lp_mag = jnp.exp(z_re[:, None] * j)
    lp_re = lp_mag * jnp.cos(z_im[:, None] * j)
    lp_im = lp_mag * jnp.sin(z_im[:, None] * j)
    pr, pi = lp_re[:, :n, None], lp_im[:, :n, None]
    m_re = pr * bb_re[:, None] - pi * bb_im[:, None]
    m_im = pr * bb_im[:, None] + pi * bb_re[:, None]
    kern = (jnp.einsum('gjip,gop->gijo', m_re, c_re, precision=hp)
            - jnp.einsum('gjip,gop->gijo', m_im, c_im, precision=hp))
    kern = kern.at[:, :, 0, :].add(jax.vmap(jnp.diag)(d_skip.reshape(groups, ch)))
    kpad = jnp.pad(kern, ((0, 0), (0, 0), (n - 1, 0), (0, 0)))
    toep = jnp.stack([kpad[:, :, n - 1 - t:2 * n - 1 - t, :] for t in range(n)], axis=1)
    toep = toep.reshape(groups, n * ch, n * ch)
    wi_re = m_re[:, ::-1].reshape(groups // 2, 2, n * ch, state)
    wi_im = m_im[:, ::-1].reshape(groups // 2, 2, n * ch, state)
    zero = jnp.zeros_like(wi_re[:, 0])
    top = jnp.concatenate([wi_re[:, 0], zero, wi_im[:, 0], zero], axis=2)
    bot = jnp.concatenate([zero, wi_re[:, 1], zero, wi_im[:, 1]], axis=2)
    wi_pair = jnp.concatenate([top, bot], axis=1)
    ct_re, ct_im = jnp.transpose(c_re, (0, 2, 1))[:, :, None, :], jnp.transpose(c_im, (0, 2, 1))[:, :, None, :]
    qr = jnp.transpose(lp_re[:, 1:], (0, 2, 1))[:, :, :, None]
    qi = jnp.transpose(lp_im[:, 1:], (0, 2, 1))[:, :, :, None]
    cp_re = (ct_re * qr - ct_im * qi).reshape(groups // 2, 2, state, n * ch)
    cp_im = (ct_re * qi + ct_im * qr).reshape(groups // 2, 2, state, n * ch)
    zc = jnp.zeros_like(cp_re[:, 0])
    even = jnp.concatenate([cp_re[:, 0], zc, -cp_im[:, 0], zc], axis=1)
    odd = jnp.concatenate([zc, cp_re[:, 1], zc, -cp_im[:, 1]], axis=1)
    cpow = jnp.stack([even, odd], axis=1).reshape(groups, 2 * 2 * state, n * ch)
    lam16 = jnp.stack([lp_re[:, n].reshape(groups // 2, 2 * state), lp_im[:, n].reshape(groups // 2, 2 * state)])
    return wi_pair.astype(BF16), toep.astype(BF16), cpow.astype(BF16), lam16


def _rope_tables(pos):
    half = HEAD_DIM // 2
    inv = ROPE_THETA ** (-jnp.arange(half, dtype=F32) * 2.0 / HEAD_DIM)
    lane = jnp.arange(LANES)
    ang = pos.astype(F32)[:, None] * inv[lane % half][None, :]
    first = ((lane % HEAD_DIM) < half)[None, :]
    sin = jnp.sin(ang)
    return jnp.cos(ang), jnp.where(first, -sin, 0.0), jnp.where(first, 0.0, sin)


def _state_in(state):
    b, groups, p, _ = state.shape
    return jnp.transpose(state, (0, 3, 1, 2)).reshape(b, 2, groups // 2, 2 * p)


def _state_out(h, groups):
    b = h.shape[0]
    return jnp.transpose(h.reshape(b, 2, groups, -1), (0, 2, 3, 1))


def _stream(x, mods, tabs, past, h0, lw, *, tile, keep, nseq_tile, seqs, seq_len):
    sh1, sc1, g1, sh2, sc2, g2 = mods
    nb, rows, d = x.shape
    groups = lw['groups']
    q, k, v, u2, k_last, v_last = _inproj_call(x, sc1, sh1, lw['ln1'], lw['w_in'], lw['gq'], lw['gk'], lw['ones'],
                                               *tabs, tile=tile, keep=keep)
    y2, h_last = _ssm_call(u2, h0, lw['wi'], lw['toep'], lw['cpow'], lw['lam'],
                           nseq=(1 if nseq_tile is None else nseq_tile), nchunk=seq_len // SSM_L)
    kp, vp = (None, None) if past is None else past
    x1 = _mix_call(lw['sinks'], x, g1, q, k, v, kp, vp, y2, lw['gluw'], lw['glub'], lw['ga'], lw['gs'],
                   lw['w_out'], tile=tile, nseq=nseq_tile)
    out = _ffn_call(x1, sc2, sh2, g2, lw['ln2'], lw['wg'], lw['wu'], lw['wd'], tile=tile)
    return out, k_last, v_last, h_last


def kernel(x_prompt, x_sample, cache_k, cache_v, state_ssm, c_prompt, c_sample, w_ada, b_ada, ln1_g, w_in, q_norm_g, k_norm_g, attn_sinks, ssm_A_re, ssm_A_im, ssm_log_dt, ssm_B_re, ssm_B_im, ssm_C_re, ssm_C_im, ssm_D, ssm_glu_w, ssm_glu_b, attn_out_g, ssm_out_g, w_out, ln2_g, w_gate, w_up, w_down):
    depth = w_ada.shape[0]
    bp, sp, d = x_prompt.shape
    bs, ss, _ = x_sample.shape
    groups = ssm_A_re.shape[1]
    kvw = N_KV_HEADS * HEAD_DIM
    tile_p = min(512, sp)
    seg = jnp.arange(256) // HEAD_DIM
    ones = (seg[:, None] == seg[None, :]).astype(BF16)
    tabs_p = _rope_tables(jnp.arange(sp))
    tabs_s = tuple(jnp.tile(t, (bs, 1)) for t in _rope_tables(PAST_LEN + jnp.arange(ss)))

    yp = x_prompt
    ys = x_sample.reshape(1, bs * ss, d)
    outs = [[] for _ in range(6)]
    for l in range(depth):
        wi, toep, cpow, lam = _ssm_weights(ssm_A_re[l], ssm_A_im[l], ssm_log_dt[l], ssm_B_re[l], ssm_B_im[l],
                                           ssm_C_re[l], ssm_C_im[l], ssm_D[l])
        lw = dict(groups=groups, ln1=ln1_g[l][None], w_in=w_in[l].astype(BF16),
                  gq=jnp.tile(q_norm_g[l], N_HEADS)[None] * (HEAD_DIM ** -0.5 * LOG2E), gk=jnp.tile(k_norm_g[l], N_KV_HEADS)[None], ones=ones,
                  wi=wi, toep=toep, cpow=cpow, lam=lam, sinks=attn_sinks[l],
                  gluw=ssm_glu_w[l].astype(BF16), glub=ssm_glu_b[l][None], ga=attn_out_g[l][None],
                  gs=ssm_out_g[l][None], w_out=w_out[l].astype(BF16), ln2=ln2_g[l][None],
                  wg=w_gate[l].astype(BF16), wu=w_up[l].astype(BF16), wd=w_down[l].astype(BF16))
        mod = _mod_call(jnp.concatenate([c_prompt, c_sample], axis=0), w_ada[l], b_ada[l])
        mods_p = tuple(m[:, None, :] for m in jnp.split(mod[:bp], 6, axis=-1))
        mods_s = tuple(jnp.repeat(m, ss, axis=0)[None] for m in jnp.split(mod[bp:], 6, axis=-1))

        h0_p = jnp.zeros((bp, 2, groups // 2, 2 * SSM_STATE), F32)
        yp, kpl, vpl, hpl = _stream(yp, mods_p, tabs_p, None, h0_p, lw, tile=tile_p, keep=WINDOW,
                                    nseq_tile=None, seqs=bp, seq_len=sp)
        past = (cache_k[l].reshape(bs, WINDOW, kvw).astype(BF16), cache_v[l].reshape(bs, WINDOW, kvw).astype(BF16))
        ys, ksl, vsl, hsl = _stream(ys, mods_s, tabs_s, past, _state_in(state_ssm[l]), lw, tile=bs * ss,
                                    keep=bs * ss, nseq_tile=bs, seqs=bs, seq_len=ss)
        outs[0].append(kpl.reshape(bp, WINDOW, N_KV_HEADS, HEAD_DIM))
        outs[1].append(vpl.reshape(bp, WINDOW, N_KV_HEADS, HEAD_DIM))
        outs[2].append(_state_out(hpl, groups))
        outs[3].append(ksl.reshape(bs, ss, N_KV_HEADS, HEAD_DIM))
        outs[4].append(vsl.reshape(bs, ss, N_KV_HEADS, HEAD_DIM))
        outs[5].append(_state_out(hsl, groups))
    return (yp, ys.reshape(bs, ss, d)) + tuple(jnp.stack(o) for o in outs)
```

```python
import functools
import math

import jax
import jax.numpy as jnp
from jax import lax
from jax.experimental import pallas as pl
from jax.experimental.pallas import tpu as pltpu

F32 = jnp.float32
BF16 = jnp.bfloat16

CHUNK = 64
WINDOW = 128
HEAD_DIM = 64
N_HEADS = 8
N_KV_HEADS = 2
SSM_CH = 16
SSM_STATE = 64
SSM_L = 16
ROPE_THETA = 10000.0
EPS = 1e-6
PAST_LEN = 1024
LANES = 128
MXU_TILE = 256
KEYS_PAD = MXU_TILE
LOG2E = math.log2(math.e)
VMEM_LIMIT = 56 * 1024 * 1024


def _const_spec(shape):
    nd = len(shape)
    return pl.BlockSpec(shape, lambda *_: (0,) * nd, pipeline_mode=pl.Buffered(1))


def _params(n_grid):
    return pltpu.CompilerParams(dimension_semantics=("arbitrary",) * n_grid, vmem_limit_bytes=VMEM_LIMIT)


def _rms(x):
    return x * lax.rsqrt(jnp.mean(x * x, axis=-1, keepdims=True) + EPS)


def _dot(a, b):
    return jnp.dot(a, b, preferred_element_type=F32)


def _mod_kernel(c_ref, w_ref, b_ref, o_ref):
    c = c_ref[...]
    s = (c * jax.nn.sigmoid(c)).astype(BF16)
    o_ref[...] = _dot(s, w_ref[...].astype(BF16)) + b_ref[...]


def _mod_call(c, w, b):
    rows, d = c.shape
    cols = w.shape[1]
    tile = 1536
    return pl.pallas_call(
        _mod_kernel,
        grid=(cols // tile,),
        in_specs=[pl.BlockSpec((rows, d), lambda j: (0, 0)),
                  pl.BlockSpec((d, tile), lambda j: (0, j)),
                  pl.BlockSpec((1, tile), lambda j: (0, j))],
        out_specs=pl.BlockSpec((rows, tile), lambda j: (0, j)),
        out_shape=jax.ShapeDtypeStruct((rows, cols), F32),
        compiler_params=_params(1),
        name="mod",
    )(c, w, b.reshape(1, cols))


def _head_rms(t, ones_ref):
    width = t.shape[1]
    sq = t * t
    hi = sq.astype(BF16)
    lo = (sq - hi.astype(F32)).astype(BF16)
    parts = []
    for c0 in range(0, width, 256):
        w = min(256, width - c0)
        ones = ones_ref[:w, :w]
        parts.append(_dot(hi[:, c0:c0 + w], ones) + _dot(lo[:, c0:c0 + w], ones))
    ssq = parts[0] if len(parts) == 1 else jnp.concatenate(parts, axis=1)
    return t * lax.rsqrt(ssq * (1.0 / HEAD_DIM) + EPS)


def _rope(t, cos, s1, s2):
    outs = []
    for c0 in range(0, t.shape[1], LANES):
        xb = t[:, c0:c0 + LANES]
        outs.append(xb * cos + pltpu.roll(xb, LANES - HEAD_DIM // 2, 1) * s1 + pltpu.roll(xb, HEAD_DIM // 2, 1) * s2)
    return outs[0] if len(outs) == 1 else jnp.concatenate(outs, axis=1)


def _granule_transpose(arrs):
    gran = lax.broadcasted_iota(jnp.int32, arrs[0].shape, 1) // SSM_CH
    cur = list(arrs)
    for s in (4, 2, 1):
        upper = (gran & s) != 0
        nxt = list(cur)
        for a0 in range(8):
            if a0 & s:
                continue
            lo, hi = cur[a0], cur[a0 + s]
            nxt[a0] = jnp.where(upper, pltpu.roll(hi, SSM_CH * s, 1), lo)
            nxt[a0 + s] = jnp.where(upper, hi, pltpu.roll(lo, LANES - SSM_CH * s, 1))
        cur = nxt
    return cur


def _inproj_kernel(x_ref, sc_ref, sh_ref, ln_ref, w_ref, gq_ref, gk_ref, ones_ref, cos_ref, s1_ref, s2_ref,
                   q_ref, k_ref, v_ref, u_ref, klast_ref, vlast_ref, uscr, *, q_cols, kv_cols, keep):
    x = x_ref[0]
    h = _rms(x) * (ln_ref[...] * (1.0 + sc_ref[0])) + sh_ref[0]
    proj = _dot(h.astype(BF16), w_ref[...])
    q = proj[:, :q_cols]
    k = proj[:, q_cols:q_cols + kv_cols]
    v = proj[:, q_cols + kv_cols:q_cols + 2 * kv_cols]
    u = proj[:, q_cols + 2 * kv_cols:]
    cos, s1, s2 = cos_ref[...], s1_ref[...], s2_ref[...]
    qr = _rope(_head_rms(q, ones_ref) * gq_ref[...], cos, s1, s2)
    kr = _rope(_head_rms(k, ones_ref) * gk_ref[...], cos, s1, s2)
    q_ref[0] = qr.astype(BF16)
    k_ref[0] = kr.astype(BF16)
    v_ref[0] = v.astype(BF16)
    rows = x.shape[0]
    nchunk = rows // SSM_L
    for o in range(u.shape[1] // LANES):
        uscr[o] = u[:, LANES * o:LANES * (o + 1)]
    for o in range(u.shape[1] // LANES):
        for hh in range(SSM_L // 8):
            outs = _granule_transpose([uscr[o, pl.ds(8 * hh + t, nchunk, stride=SSM_L), :].astype(BF16)
                                       for t in range(8)])
            for g8 in range(8):
                u_ref[8 * o + g8, :, LANES * hh:LANES * (hh + 1)] = outs[g8]
    klast_ref[0] = kr[rows - keep:, :]
    vlast_ref[0] = v[rows - keep:, :]


def _inproj_call(x, sc, sh, ln_g, w_in, gq, gk, ones, cos, s1, s2, *, tile, keep):
    nb, rows, d = x.shape
    in_cols = w_in.shape[1]
    q_cols = N_HEADS * HEAD_DIM
    kv_cols = N_KV_HEADS * HEAD_DIM
    u_cols = in_cols - q_cols - 2 * kv_cols
    mrows = sc.shape[1]
    mtile = 1 if mrows == 1 else tile
    mod_spec = pl.BlockSpec((1, mtile, d), (lambda b, i: (b, 0, 0)) if mrows == 1 else (lambda b, i: (b, i, 0)))
    row_spec = lambda c: pl.BlockSpec((1, tile, c), lambda b, i: (b, i, 0))
    tab_spec = pl.BlockSpec((tile, LANES), lambda b, i: (i, 0))
    last_spec = pl.BlockSpec((1, keep, kv_cols), lambda b, i: (b, 0, 0))
    groups = u_cols // SSM_CH
    tiles = rows // tile
    chunk_spec = pl.BlockSpec((groups, tile // SSM_L, SSM_L * SSM_CH), lambda b, i: (0, b * tiles + i, 0))
    kern = functools.partial(_inproj_kernel, q_cols=q_cols, kv_cols=kv_cols, keep=keep)
    return pl.pallas_call(
        kern,
        grid=(nb, tiles),
        in_specs=[row_spec(d), mod_spec, mod_spec, _const_spec((1, d)), _const_spec((d, in_cols)),
                  _const_spec((1, q_cols)), _const_spec((1, kv_cols)), _const_spec((256, 256)),
                  tab_spec, tab_spec, tab_spec],
        out_specs=[row_spec(q_cols), row_spec(kv_cols), row_spec(kv_cols), chunk_spec, last_spec, last_spec],
        out_shape=[jax.ShapeDtypeStruct((nb, rows, q_cols), BF16),
                   jax.ShapeDtypeStruct((nb, rows, kv_cols), BF16),
                   jax.ShapeDtypeStruct((nb, rows, kv_cols), BF16),
                   jax.ShapeDtypeStruct((groups, nb * rows // SSM_L, SSM_L * SSM_CH), BF16),
                   jax.ShapeDtypeStruct((nb, keep, kv_cols), F32),
                   jax.ShapeDtypeStruct((nb, keep, kv_cols), F32)],
        scratch_shapes=[pltpu.VMEM((u_cols // LANES, tile, LANES), F32)],
        compiler_params=_params(2),
        name="inproj",
    )(x, sc, sh, ln_g, w_in, gq, gk, ones, cos, s1, s2)


def _ssm_kernel(u_ref, h0_ref, wi_ref, krow_ref, cpow_ref, lam_ref, y_ref, hout_ref,
                toep, s_re, s_im, hp_re, hp_im, *, nseq, nchunk):
    groups = u_ref.shape[0]
    rows = nseq * nchunk
    npair = groups // 2

    @pl.when(pl.program_id(0) == 0)
    def _():
        lane = lax.broadcasted_iota(jnp.int32, (SSM_CH, SSM_L * SSM_CH), 1)

        def expand(g, carry):
            kr = krow_ref[g]
            for t in range(SSM_L):
                blk = kr if t == 0 else jnp.where(lane >= SSM_CH * t, pltpu.roll(kr, SSM_CH * t, 1), 0.0)
                toep[g, SSM_CH * t:SSM_CH * (t + 1), :] = blk.astype(BF16)
            return carry

        lax.fori_loop(0, groups, expand, 0)

    for p in range(npair):
        lhs = jnp.concatenate([u_ref[2 * p], u_ref[2 * p + 1]], axis=1)
        s = _dot(lhs, wi_ref[p])
        s_re[pl.ds(p, rows, stride=npair), :] = s[:, :LANES]
        s_im[pl.ds(p, rows, stride=npair), :] = s[:, LANES:]

    a_re = lam_ref[0]
    a_im = lam_ref[1]
    for b in range(nseq):
        def step(n, carry):
            h_re, h_im = carry
            r0 = pl.multiple_of((b * nchunk + n) * npair, npair)
            hp_re[pl.ds(r0, npair), :] = h_re
            hp_im[pl.ds(r0, npair), :] = h_im
            n_re = a_re * h_re - a_im * h_im + s_re[pl.ds(r0, npair), :]
            n_im = a_re * h_im + a_im * h_re + s_im[pl.ds(r0, npair), :]
            return n_re, n_im
        h_re, h_im = lax.fori_loop(0, nchunk, step, (h0_ref[b, 0], h0_ref[b, 1]))
        hout_ref[b, 0] = h_re
        hout_ref[b, 1] = h_im

    for g in range(groups):
        p = g // 2
        hp = jnp.concatenate([hp_re[pl.ds(p, rows, stride=npair), :],
                              hp_im[pl.ds(p, rows, stride=npair), :]], axis=1).astype(BF16)
        y = _dot(u_ref[g], toep[g]) + _dot(hp, cpow_ref[g])
        y_ref[g] = y.astype(BF16)


def _ssm_call(u2, h0, wi, krow, cpow, lam, *, nseq, nchunk):
    groups, total_rows, width = u2.shape
    rows = nseq * nchunk
    npair = groups // 2
    kern = functools.partial(_ssm_kernel, nseq=nseq, nchunk=nchunk)
    return pl.pallas_call(
        kern,
        grid=(total_rows // rows,),
        in_specs=[pl.BlockSpec((groups, rows, width), lambda i: (0, i, 0)),
                  pl.BlockSpec((nseq, 2, npair, LANES), lambda i: (i, 0, 0, 0)),
                  _const_spec(wi.shape), _const_spec(krow.shape), _const_spec(cpow.shape), _const_spec(lam.shape)],
        out_specs=[pl.BlockSpec((groups, rows, width), lambda i: (0, i, 0)),
                   pl.BlockSpec((nseq, 2, npair, LANES), lambda i: (i, 0, 0, 0))],
        out_shape=[jax.ShapeDtypeStruct((groups, total_rows, width), BF16),
                   jax.ShapeDtypeStruct(h0.shape, F32)],
        scratch_shapes=[pltpu.VMEM((groups, width, width), BF16)] + [pltpu.VMEM((rows * npair, LANES), F32)] * 4,
        compiler_params=_params(1),
        name="ssm",
    )(u2, h0, wi, krow, cpow, lam)


def _kv_dup(a):
    lo = lax.broadcasted_iota(jnp.int32, a.shape, 1) < HEAD_DIM
    sw = pltpu.roll(a, HEAD_DIM, 1)
    return jnp.where(lo, a, sw), jnp.where(lo, sw, a)


def _v_ext(v):
    return jnp.concatenate([v, jnp.ones_like(v)], axis=1).astype(BF16)


def _sink_tail(rows):
    row = lax.broadcasted_iota(jnp.int32, (rows, 2 * LANES), 0)
    lane = lax.broadcasted_iota(jnp.int32, (rows, 2 * LANES), 1)
    return jnp.where(jnp.logical_and(row == 0, lane >= LANES), 1.0, 0.0).astype(BF16)


def _sink_attention(sink_ref, problems, nkeys):
    rep = N_HEADS // N_KV_HEADS
    col = lax.broadcasted_iota(jnp.int32, (1, KEYS_PAD), 1)
    real = col < nkeys
    scores = []
    for g, qa, qb, kb, vb, valid, store in problems:
        half = qa.shape[0]
        qst = jnp.concatenate([qa, qb], axis=0)
        lo = lax.broadcasted_iota(jnp.int32, qst.shape, 1) < HEAD_DIM
        zero = jnp.zeros_like(qst)
        q4 = jnp.concatenate([jnp.where(lo, qst, zero), jnp.where(lo, zero, qst)], axis=0)
        s = lax.dot_general(q4, kb, (((1,), (1,)), ((), ())), preferred_element_type=F32)
        keep = real if valid is None else jnp.logical_and(real, valid)
        first = 0 if valid is not None else (nkeys // LANES) * LANES
        blocks = []
        for e in range(2):
            for jj in range(2):
                fill = jnp.where(col == nkeys, sink_ref[rep * g + 2 * jj + e] * LOG2E, -jnp.inf)
                sb = s[(2 * e + jj) * half:(2 * e + jj + 1) * half]
                fixed = jnp.where(keep[:, first:], sb[:, first:], fill[:, first:])
                blocks.append(fixed if first == 0 else jnp.concatenate([sb[:, :first], fixed], axis=1))
        scores.append(jnp.concatenate(blocks, axis=0))
    probs = []
    for s in scores:
        probs.append(jnp.exp2(s - jnp.max(s, axis=-1, keepdims=True)).astype(BF16))
    for p, (g, qa, qb, kb, vb, valid, store) in zip(probs, problems):
        o4 = _dot(p, vb)
        h2 = 2 * qa.shape[0]
        lo = lax.broadcasted_iota(jnp.int32, (h2, LANES), 1) < HEAD_DIM
        store(jnp.where(lo, o4[:h2, :LANES], o4[h2:, :LANES]) / jnp.where(lo, o4[:h2, LANES:], o4[h2:, LANES:]))


def _attn_prompt(q_ref, kc_ref, kp_ref, vc_ref, vp_ref, sink_ref, kz, vz, attn, tile):
    band = CHUNK + WINDOW
    kd = _kv_dup(jnp.concatenate([kp_ref[0], kc_ref[0]], axis=0).astype(F32))
    vd = _kv_dup(jnp.concatenate([vp_ref[0], vc_ref[0]], axis=0).astype(F32))
    tail = _sink_tail(KEYS_PAD - band)
    for g in range(2):
        kz[g, :tile + WINDOW] = kd[g].astype(BF16)
        kz[g, tile + WINDOW:] = jnp.zeros((KEYS_PAD - band, LANES), BF16)
        vext = _v_ext(vd[g])
        for c in range(tile // CHUNK):
            vz[g, c, :band] = vext[c * CHUNK:c * CHUNK + band]
            vz[g, c, band:] = tail
    col_chunk = lax.broadcasted_iota(jnp.int32, (1, KEYS_PAD), 1) // CHUNK
    later_tile = pl.program_id(1) > 0
    problems = []
    for c in range(tile // CHUNK):
        r0 = c * CHUNK
        valid = None if c >= WINDOW // CHUNK else jnp.logical_or(col_chunk + c >= WINDOW // CHUNK, later_tile)
        for g in range(2):
            c0 = 2 * LANES * g

            def store(o, r0=r0, c0=c0):
                attn[r0:r0 + CHUNK, c0:c0 + LANES] = o[:CHUNK]
                attn[r0:r0 + CHUNK, c0 + LANES:c0 + 2 * LANES] = o[CHUNK:]

            problems.append((g, q_ref[0, r0:r0 + CHUNK, c0:c0 + LANES],
                             q_ref[0, r0:r0 + CHUNK, c0 + LANES:c0 + 2 * LANES],
                             kz[g, r0:r0 + KEYS_PAD, :], vz[g, c], valid, store))
    _sink_attention(sink_ref, problems, band)


def _attn_sample(q_ref, kc_ref, kp_ref, vc_ref, vp_ref, sink_ref, attn, nseq, seq):
    nkeys = WINDOW + seq
    tail = _sink_tail(KEYS_PAD - nkeys)
    problems = []
    for b in range(nseq):
        r0 = b * seq
        kd = _kv_dup(jnp.concatenate([kp_ref[b], kc_ref[0, r0:r0 + seq, :]], axis=0).astype(F32))
        vd = _kv_dup(jnp.concatenate([vp_ref[b], vc_ref[0, r0:r0 + seq, :]], axis=0).astype(F32))
        for g in range(2):
            c0 = 2 * LANES * g

            def store(o, r0=r0, c0=c0):
                attn[r0:r0 + seq, c0:c0 + LANES] = o[:seq]
                attn[r0:r0 + seq, c0 + LANES:c0 + 2 * LANES] = o[seq:]

            kb = jnp.concatenate([kd[g].astype(BF16), jnp.zeros((KEYS_PAD - nkeys, LANES), BF16)], axis=0)
            vb = jnp.concatenate([_v_ext(vd[g]), tail], axis=0)
            problems.append((g, q_ref[0, r0:r0 + seq, c0:c0 + LANES], q_ref[0, r0:r0 + seq, c0 + LANES:c0 + 2 * LANES],
                             kb, vb, None, store))
    _sink_attention(sink_ref, problems, nkeys)


def _gelu_tanh(x):
    return 0.5 * x * (1.0 + jnp.tanh(math.sqrt(2.0 / math.pi) * (x + 0.044715 * (x * x * x))))


def _mix_kernel(sink_ref, x_ref, g1_ref, q_ref, kc_ref, kp_ref, vc_ref, vp_ref, y_ref,
                gluw_ref, glub_ref, ga_ref, gs_ref, wout_ref, o_ref, *scratch, tile, nseq):
    if nseq is None:
        yscr, kz, vz, attn = scratch
        _attn_prompt(q_ref, kc_ref, kp_ref, vc_ref, vp_ref, sink_ref, kz, vz, attn, tile)
    else:
        yscr, attn = scratch
        _attn_sample(q_ref, kc_ref, kp_ref, vc_ref, vp_ref, sink_ref, attn, nseq, tile // nseq)
    nslab = yscr.shape[0]
    for o in range(nslab):
        for hh in range(SSM_L // 8):
            outs = _granule_transpose([y_ref[8 * o + g8, :, LANES * hh:LANES * (hh + 1)] for g8 in range(8)])
            for t8 in range(8):
                yscr[o, pl.ds(8 * hh + t8, tile // SSM_L, stride=SSM_L), :] = outs[t8].astype(F32)
    an = _rms(attn[...]) * ga_ref[...]
    gl = _gelu_tanh(jnp.concatenate([yscr[o] for o in range(nslab)], axis=1))
    so = gl * jax.nn.sigmoid(_dot(gl.astype(BF16), gluw_ref[...]) + glub_ref[...])
    sn = _rms(so) * gs_ref[...]
    merged = jnp.concatenate([an, sn], axis=1).astype(BF16)
    o_ref[0] = x_ref[0] + g1_ref[0] * _dot(merged, wout_ref[...])


def _mix_call(sinks, x, g1, q, k, v, k_past, v_past, y, gluw, glub, ga, gs, wout, *, tile, nseq):
    nb, rows, d = x.shape
    aw = q.shape[2]
    kvw = k.shape[2]
    groups, _, cw = y.shape
    tiles = rows // tile
    mrows = g1.shape[1]
    mtile = 1 if mrows == 1 else tile
    mod_spec = pl.BlockSpec((1, mtile, d), (lambda b, i: (b, 0, 0)) if mrows == 1 else (lambda b, i: (b, i, 0)))
    row_spec = lambda c: pl.BlockSpec((1, tile, c), lambda b, i: (b, i, 0))
    chunk_spec = pl.BlockSpec((groups, tile // SSM_L, cw), lambda b, i: (0, b * tiles + i, 0))
    scratch = [pltpu.VMEM((groups * SSM_CH // LANES, tile, LANES), F32)]
    if nseq is None:
        wpt = tile // WINDOW
        past_spec = pl.BlockSpec((1, WINDOW, kvw), lambda b, i: (b, jnp.maximum(i * wpt - 1, 0), 0))
        k_past, v_past = k, v
        scratch += [pltpu.VMEM((2, tile + KEYS_PAD - CHUNK, LANES), BF16),
                    pltpu.VMEM((2, tile // CHUNK, KEYS_PAD, 2 * LANES), BF16), pltpu.VMEM((tile, aw), F32)]
    else:
        past_spec = pl.BlockSpec((nseq, WINDOW, kvw), lambda b, i: (0, 0, 0))
        scratch += [pltpu.VMEM((tile, aw), F32)]
    kern = functools.partial(_mix_kernel, tile=tile, nseq=nseq)
    return pl.pallas_call(
        kern,
        grid=(nb, tiles),
        in_specs=[pl.BlockSpec(memory_space=pltpu.SMEM),
                  row_spec(d), mod_spec, row_spec(aw), row_spec(kvw), past_spec, row_spec(kvw), past_spec,
                  chunk_spec, _const_spec(gluw.shape), _const_spec(glub.shape), _const_spec(ga.shape),
                  _const_spec(gs.shape), _const_spec(wout.shape)],
        out_specs=row_spec(d),
        out_shape=jax.ShapeDtypeStruct((nb, rows, d), F32),
        scratch_shapes=scratch,
        compiler_params=_params(2),
        name="mix",
    )(sinks, x, g1, q, k, k_past, v, v_past, y, gluw, glub, ga, gs, wout)


def _ffn_kernel(x_ref, sc_ref, sh_ref, g2_ref, ln_ref, wg_ref, wu_ref, wd_ref, o_ref, *, ff_bounds):
    x = x_ref[0]
    h = (_rms(x) * (ln_ref[...] * (1.0 + sc_ref[0])) + sh_ref[0]).astype(BF16)
    acc = None
    for c0, c1 in zip(ff_bounds[:-1], ff_bounds[1:]):
        a = _dot(h, wg_ref[:, c0:c1])
        b = _dot(h, wu_ref[:, c0:c1])
        part = _dot((a * jax.nn.sigmoid(a) * b).astype(BF16), wd_ref[c0:c1, :])
        acc = part if acc is None else acc + part
    o_ref[0] = x + g2_ref[0] * acc


def _ffn_call(x, sc, sh, g2, ln_g, wg, wu, wd, *, tile):
    nb, rows, d = x.shape
    dff = wg.shape[1]
    split = -(-dff // (2 * MXU_TILE)) * MXU_TILE
    ff_bounds = (0, split, dff) if split < dff else (0, dff)
    mrows = sc.shape[1]
    mtile = 1 if mrows == 1 else tile
    mod_spec = pl.BlockSpec((1, mtile, d), (lambda b, i: (b, 0, 0)) if mrows == 1 else (lambda b, i: (b, i, 0)))
    row_spec = pl.BlockSpec((1, tile, d), lambda b, i: (b, i, 0))
    kern = functools.partial(_ffn_kernel, ff_bounds=ff_bounds)
    return pl.pallas_call(
        kern,
        grid=(nb, rows // tile),
        in_specs=[row_spec, mod_spec, mod_spec, mod_spec, _const_spec((1, d)),
                  _const_spec(wg.shape), _const_spec(wu.shape), _const_spec(wd.shape)],
        out_specs=row_spec,
        out_shape=jax.ShapeDtypeStruct((nb, rows, d), F32),
        compiler_params=_params(2),
        name="ffn",
    )(x, sc, sh, g2, ln_g, wg, wu, wd)


def _ssm_weights(a_re, a_im, log_dt, b_re, b_im, c_re, c_im, d_skip):
    hp = lax.Precision.HIGHEST
    groups, state = a_re.shape
    ch = b_re.shape[2]
    n = SSM_L
    dt = jnp.exp(log_dt)[:, None]
    z_re, z_im = a_re * dt, a_im * dt
    e_re = jnp.expm1(z_re) * jnp.cos(z_im) - 2.0 * jnp.sin(0.5 * z_im) ** 2
    e_im = jnp.exp(z_re) * jnp.sin(z_im)
    mag = a_re * a_re + a_im * a_im
    coef_re = (e_re * a_re + e_im * a_im) / mag
    coef_im = (e_im * a_re - e_re * a_im) / mag
    bt_re, bt_im = jnp.transpose(b_re, (0, 2, 1)), jnp.transpose(b_im, (0, 2, 1))
    bb_re = coef_re[:, None] * bt_re - coef_im[:, None] * bt_im
    bb_im = coef_re[:, None] * bt_im + coef_im[:, None] * bt_re
    j = jnp.arange(n + 1, dtype=F32)[None, :, None]
    lp_mag = jnp.exp(z_re[:, None] * j)
    lp_re = lp_mag * jnp.cos(z_im[:, None] * j)
    lp_im = lp_mag * jnp.sin(z_im[:, None] * j)
    pr, pi = lp_re[:, :n, None], lp_im[:, :n, None]
    m_re = pr * bb_re[:, None] - pi * bb_im[:, None]
    m_im = pr * bb_im[:, None] + pi * bb_re[:, None]
    kern = (jnp.einsum('gjip,gop->gijo', m_re, c_re, precision=hp)
            - jnp.einsum('gjip,gop->gijo', m_im, c_im, precision=hp))
    kern = kern.at[:, :, 0, :].add(jax.vmap(jnp.diag)(d_skip.reshape(groups, ch)))
    krow = kern.reshape(groups, ch, n * ch)
    wi_re = m_re[:, ::-1].reshape(groups // 2, 2, n * ch, state)
    wi_im = m_im[:, ::-1].reshape(groups // 2, 2, n * ch, state)
    zero = jnp.zeros_like(wi_re[:, 0])
    top = jnp.concatenate([wi_re[:, 0], zero, wi_im[:, 0], zero], axis=2)
    bot = jnp.concatenate([zero, wi_re[:, 1], zero, wi_im[:, 1]], axis=2)
    wi_pair = jnp.concatenate([top, bot], axis=1)
    ct_re, ct_im = jnp.transpose(c_re, (0, 2, 1))[:, :, None, :], jnp.transpose(c_im, (0, 2, 1))[:, :, None, :]
    qr = jnp.transpose(lp_re[:, 1:], (0, 2, 1))[:, :, :, None]
    qi = jnp.transpose(lp_im[:, 1:], (0, 2, 1))[:, :, :, None]
    cp_re = (ct_re * qr - ct_im * qi).reshape(groups // 2, 2, state, n * ch)
    cp_im = (ct_re * qi + ct_im * qr).reshape(groups // 2, 2, state, n * ch)
    zc = jnp.zeros_like(cp_re[:, 0])
    even = jnp.concatenate([cp_re[:, 0], zc, -cp_im[:, 0], zc], axis=1)
    odd = jnp.concatenate([zc, cp_re[:, 1], zc, -cp_im[:, 1]], axis=1)
    cpow = jnp.stack([even, odd], axis=1).reshape(groups, 2 * 2 * state, n * ch)
    lam16 = jnp.stack([lp_re[:, n].reshape(groups // 2, 2 * state), lp_im[:, n].reshape(groups // 2, 2 * state)])
    return wi_pair.astype(BF16), krow, cpow.astype(BF16), lam16


def _rope_tables(pos):
    half = HEAD_DIM // 2
    inv = ROPE_THETA ** (-jnp.arange(half, dtype=F32) * 2.0 / HEAD_DIM)
    lane = jnp.arange(LANES)
    ang = pos.astype(F32)[:, None] * inv[lane % half][None, :]
    first = ((lane % HEAD_DIM) < half)[None, :]
    sin = jnp.sin(ang)
    return jnp.cos(ang), jnp.where(first, -sin, 0.0), jnp.where(first, 0.0, sin)


def _state_in(state):
    b, groups, p, _ = state.shape
    return jnp.transpose(state, (0, 3, 1, 2)).reshape(b, 2, groups // 2, 2 * p)


def _state_out(h, groups):
    b = h.shape[0]
    return jnp.transpose(h.reshape(b, 2, groups, -1), (0, 2, 3, 1))


def _stream(x, mods, tabs, past, h0, lw, *, tile, keep, nseq_tile, seqs, seq_len):
    sh1, sc1, g1, sh2, sc2, g2 = mods
    nb, rows, d = x.shape
    groups = lw['groups']
    q, k, v, u2, k_last, v_last = _inproj_call(x, sc1, sh1, lw['ln1'], lw['w_in'], lw['gq'], lw['gk'], lw['ones'],
                                               *tabs, tile=tile, keep=keep)
    y2, h_last = _ssm_call(u2, h0, lw['wi'], lw['krow'], lw['cpow'], lw['lam'],
                           nseq=(1 if nseq_tile is None else nseq_tile), nchunk=seq_len // SSM_L)
    kp, vp = (None, None) if past is None else past
    x1 = _mix_call(lw['sinks'], x, g1, q, k, v, kp, vp, y2, lw['gluw'], lw['glub'], lw['ga'], lw['gs'],
                   lw['w_out'], tile=tile, nseq=nseq_tile)
    out = _ffn_call(x1, sc2, sh2, g2, lw['ln2'], lw['wg'], lw['wu'], lw['wd'], tile=tile)
    return out, k_last, v_last, h_last


def kernel(x_prompt, x_sample, cache_k, cache_v, state_ssm, c_prompt, c_sample, w_ada, b_ada, ln1_g, w_in, q_norm_g, k_norm_g, attn_sinks, ssm_A_re, ssm_A_im, ssm_log_dt, ssm_B_re, ssm_B_im, ssm_C_re, ssm_C_im, ssm_D, ssm_glu_w, ssm_glu_b, attn_out_g, ssm_out_g, w_out, ln2_g, w_gate, w_up, w_down):
    depth = w_ada.shape[0]
    bp, sp, d = x_prompt.shape
    bs, ss, _ = x_sample.shape
    groups = ssm_A_re.shape[1]
    kvw = N_KV_HEADS * HEAD_DIM
    tile_p = min(512, sp)
    seg = jnp.arange(256) // HEAD_DIM
    ones = (seg[:, None] == seg[None, :]).astype(BF16)
    tabs_p = _rope_tables(jnp.arange(sp))
    tabs_s = tuple(jnp.tile(t, (bs, 1)) for t in _rope_tables(PAST_LEN + jnp.arange(ss)))

    yp = x_prompt
    ys = x_sample.reshape(1, bs * ss, d)
    outs = [[] for _ in range(6)]
    for l in range(depth):
        wi, krow, cpow, lam = _ssm_weights(ssm_A_re[l], ssm_A_im[l], ssm_log_dt[l], ssm_B_re[l], ssm_B_im[l],
                                           ssm_C_re[l], ssm_C_im[l], ssm_D[l])
        lw = dict(groups=groups, ln1=ln1_g[l][None], w_in=w_in[l].astype(BF16),
                  gq=jnp.tile(q_norm_g[l], N_HEADS)[None] * (HEAD_DIM ** -0.5 * LOG2E), gk=jnp.tile(k_norm_g[l], N_KV_HEADS)[None], ones=ones,
                  wi=wi, krow=krow, cpow=cpow, lam=lam, sinks=attn_sinks[l],
                  gluw=ssm_glu_w[l].astype(BF16), glub=ssm_glu_b[l][None], ga=attn_out_g[l][None],
                  gs=ssm_out_g[l][None], w_out=w_out[l].astype(BF16), ln2=ln2_g[l][None],
                  wg=w_gate[l].astype(BF16), wu=w_up[l].astype(BF16), wd=w_down[l].astype(BF16))
        mod = _mod_call(jnp.concatenate([c_prompt, c_sample], axis=0), w_ada[l], b_ada[l])
        mods_p = tuple(m[:, None, :] for m in jnp.split(mod[:bp], 6, axis=-1))
        mods_s = tuple(jnp.repeat(m, ss, axis=0)[None] for m in jnp.split(mod[bp:], 6, axis=-1))

        h0_p = jnp.zeros((bp, 2, groups // 2, 2 * SSM_STATE), F32)
        yp, kpl, vpl, hpl = _stream(yp, mods_p, tabs_p, None, h0_p, lw, tile=tile_p, keep=WINDOW,
                                    nseq_tile=None, seqs=bp, seq_len=sp)
        past = (cache_k[l].reshape(bs, WINDOW, kvw).astype(BF16), cache_v[l].reshape(bs, WINDOW, kvw).astype(BF16))
        ys, ksl, vsl, hsl = _stream(ys, mods_s, tabs_s, past, _state_in(state_ssm[l]), lw, tile=bs * ss,
                                    keep=bs * ss, nseq_tile=bs, seqs=bs, seq_len=ss)
        outs[0].append(kpl.reshape(bp, WINDOW, N_KV_HEADS, HEAD_DIM))
        outs[1].append(vpl.reshape(bp, WINDOW, N_KV_HEADS, HEAD_DIM))
        outs[2].append(_state_out(hpl, groups))
        outs[3].append(ksl.reshape(bs, ss, N_KV_HEADS, HEAD_DIM))
        outs[4].append(vsl.reshape(bs, ss, N_KV_HEADS, HEAD_DIM))
        outs[5].append(_state_out(hsl, groups))
    return (yp, ys.reshape(bs, ss, d)) + tuple(jnp.stack(o) for o in outs)
```

```python
import functools
import math

import jax
import jax.numpy as jnp
from jax import lax
from jax.experimental import pallas as pl
from jax.experimental.pallas import tpu as pltpu

F32 = jnp.float32
BF16 = jnp.bfloat16

CHUNK = 64
WINDOW = 128
HEAD_DIM = 64
N_HEADS = 8
N_KV_HEADS = 2
SSM_CH = 16
SSM_STATE = 64
SSM_L = 16
ROPE_THETA = 10000.0
EPS = 1e-6
PAST_LEN = 1024
LANES = 128
MXU_TILE = 256
KEYS_PAD = MXU_TILE
ROW_BLOCK = 256
SCAN_PITCH = 24
LOG2E = math.log2(math.e)
VMEM_LIMIT = 56 * 1024 * 1024


def _const_spec(shape):
    nd = len(shape)
    return pl.BlockSpec(shape, lambda *_: (0,) * nd, pipeline_mode=pl.Buffered(1))


def _params(n_grid):
    return pltpu.CompilerParams(dimension_semantics=("arbitrary",) * n_grid, vmem_limit_bytes=VMEM_LIMIT)


def _rms(x):
    return x * lax.rsqrt(jnp.mean(x * x, axis=-1, keepdims=True) + EPS)


def _dot(a, b):
    return jnp.dot(a, b, preferred_element_type=F32)


def _mod_kernel(c_ref, w_ref, b_ref, o_ref):
    c = c_ref[...]
    s = (c * jax.nn.sigmoid(c)).astype(BF16)
    o_ref[...] = _dot(s, w_ref[...].astype(BF16)) + b_ref[...]


def _mod_call(c, w, b):
    rows, d = c.shape
    cols = w.shape[1]
    tile = 1536
    return pl.pallas_call(
        _mod_kernel,
        grid=(cols // tile,),
        in_specs=[pl.BlockSpec((rows, d), lambda j: (0, 0)),
                  pl.BlockSpec((d, tile), lambda j: (0, j)),
                  pl.BlockSpec((1, tile), lambda j: (0, j))],
        out_specs=pl.BlockSpec((rows, tile), lambda j: (0, j)),
        out_shape=jax.ShapeDtypeStruct((rows, cols), F32),
        compiler_params=_params(1),
        name="mod",
    )(c, w, b.reshape(1, cols))


def _head_rms(t, ones_ref):
    width = t.shape[1]
    sq = t * t
    hi = sq.astype(BF16)
    lo = (sq - hi.astype(F32)).astype(BF16)
    parts = []
    for c0 in range(0, width, 256):
        w = min(256, width - c0)
        ones = ones_ref[:w, :w]
        parts.append(_dot(hi[:, c0:c0 + w], ones) + _dot(lo[:, c0:c0 + w], ones))
    ssq = parts[0] if len(parts) == 1 else jnp.concatenate(parts, axis=1)
    return t * lax.rsqrt(ssq * (1.0 / HEAD_DIM) + EPS)


def _rope(t, cos, s1, s2):
    outs = []
    for c0 in range(0, t.shape[1], LANES):
        xb = t[:, c0:c0 + LANES]
        outs.append(xb * cos + pltpu.roll(xb, LANES - HEAD_DIM // 2, 1) * s1 + pltpu.roll(xb, HEAD_DIM // 2, 1) * s2)
    return outs[0] if len(outs) == 1 else jnp.concatenate(outs, axis=1)


def _granule_transpose(arrs):
    gran = lax.broadcasted_iota(jnp.int32, arrs[0].shape, 1) // SSM_CH
    cur = list(arrs)
    for s in (4, 2, 1):
        upper = (gran & s) != 0
        nxt = list(cur)
        for a0 in range(8):
            if a0 & s:
                continue
            lo, hi = cur[a0], cur[a0 + s]
            nxt[a0] = jnp.where(upper, pltpu.roll(hi, SSM_CH * s, 1), lo)
            nxt[a0 + s] = jnp.where(upper, hi, pltpu.roll(lo, LANES - SSM_CH * s, 1))
        cur = nxt
    return cur


def _inproj_kernel(x_ref, sc_ref, sh_ref, ln_ref, w_ref, gq_ref, gk_ref, ones_ref, cos_ref, s1_ref, s2_ref,
                   q_ref, k_ref, v_ref, u_ref, klast_ref, vlast_ref, uscr, *, q_cols, kv_cols, keep):
    tile = x_ref.shape[1]
    block = min(tile, ROW_BLOCK)
    for r0 in range(0, tile, block):
        rows = slice(r0, r0 + block)
        sc = sc_ref[0] if sc_ref.shape[1] == 1 else sc_ref[0, rows]
        sh = sh_ref[0] if sh_ref.shape[1] == 1 else sh_ref[0, rows]
        h = _rms(x_ref[0, rows]) * (ln_ref[...] * (1.0 + sc)) + sh
        proj = _dot(h.astype(BF16), w_ref[...])
        q = proj[:, :q_cols]
        k = proj[:, q_cols:q_cols + kv_cols]
        v = proj[:, q_cols + kv_cols:q_cols + 2 * kv_cols]
        u = proj[:, q_cols + 2 * kv_cols:]
        cos, s1, s2 = cos_ref[rows], s1_ref[rows], s2_ref[rows]
        qr = _rope(_head_rms(q, ones_ref) * gq_ref[...], cos, s1, s2)
        kr = _rope(_head_rms(k, ones_ref) * gk_ref[...], cos, s1, s2)
        q_ref[0, rows] = qr.astype(BF16)
        k_ref[0, rows] = kr.astype(BF16)
        v_ref[0, rows] = v.astype(BF16)
        nchunk = block // SSM_L
        c0 = r0 // SSM_L
        for o in range(u.shape[1] // LANES):
            uscr[o, rows] = u[:, LANES * o:LANES * (o + 1)]
        for o in range(u.shape[1] // LANES):
            for hh in range(SSM_L // 8):
                outs = _granule_transpose([uscr[o, pl.ds(r0 + 8 * hh + t, nchunk, stride=SSM_L), :].astype(BF16)
                                           for t in range(8)])
                for g8 in range(8):
                    u_ref[8 * o + g8, c0:c0 + nchunk, LANES * hh:LANES * (hh + 1)] = outs[g8]
        first = max(r0, tile - keep)
        if first < r0 + block:
            dst = slice(first - (tile - keep), r0 + block - (tile - keep))
            klast_ref[0, dst] = kr[first - r0:, :]
            vlast_ref[0, dst] = v[first - r0:, :]


def _inproj_call(x, sc, sh, ln_g, w_in, gq, gk, ones, cos, s1, s2, *, tile, keep):
    nb, rows, d = x.shape
    in_cols = w_in.shape[1]
    q_cols = N_HEADS * HEAD_DIM
    kv_cols = N_KV_HEADS * HEAD_DIM
    u_cols = in_cols - q_cols - 2 * kv_cols
    mrows = sc.shape[1]
    mtile = 1 if mrows == 1 else tile
    mod_spec = pl.BlockSpec((1, mtile, d), (lambda b, i: (b, 0, 0)) if mrows == 1 else (lambda b, i: (b, i, 0)))
    row_spec = lambda c: pl.BlockSpec((1, tile, c), lambda b, i: (b, i, 0))
    tab_spec = pl.BlockSpec((tile, LANES), lambda b, i: (i, 0))
    last_spec = pl.BlockSpec((1, keep, kv_cols), lambda b, i: (b, 0, 0))
    groups = u_cols // SSM_CH
    tiles = rows // tile
    chunk_spec = pl.BlockSpec((groups, tile // SSM_L, SSM_L * SSM_CH), lambda b, i: (0, b * tiles + i, 0))
    kern = functools.partial(_inproj_kernel, q_cols=q_cols, kv_cols=kv_cols, keep=keep)
    return pl.pallas_call(
        kern,
        grid=(nb, tiles),
        in_specs=[row_spec(d), mod_spec, mod_spec, _const_spec((1, d)), _const_spec((d, in_cols)),
                  _const_spec((1, q_cols)), _const_spec((1, kv_cols)), _const_spec((256, 256)),
                  tab_spec, tab_spec, tab_spec],
        out_specs=[row_spec(q_cols), row_spec(kv_cols), row_spec(kv_cols), chunk_spec, last_spec, last_spec],
        out_shape=[jax.ShapeDtypeStruct((nb, rows, q_cols), BF16),
                   jax.ShapeDtypeStruct((nb, rows, kv_cols), BF16),
                   jax.ShapeDtypeStruct((nb, rows, kv_cols), BF16),
                   jax.ShapeDtypeStruct((groups, nb * rows // SSM_L, SSM_L * SSM_CH), BF16),
                   jax.ShapeDtypeStruct((nb, keep, kv_cols), F32),
                   jax.ShapeDtypeStruct((nb, keep, kv_cols), F32)],
        scratch_shapes=[pltpu.VMEM((u_cols // LANES, tile, LANES), F32)],
        compiler_params=_params(2),
        name="inproj",
    )(x, sc, sh, ln_g, w_in, gq, gk, ones, cos, s1, s2)


def _ssm_kernel(u_ref, h0_ref, wi_ref, krow_ref, cpow_ref, lam_ref, y_ref, hout_ref,
                toep, s_re, s_im, hp_re, hp_im, *, nseq, nchunk):
    groups = u_ref.shape[0]
    rows = nseq * nchunk
    npair = groups // 2

    @pl.when(pl.program_id(0) == 0)
    def _():
        lane = lax.broadcasted_iota(jnp.int32, (SSM_CH, SSM_L * SSM_CH), 1)

        def expand(g, carry):
            kr = krow_ref[g]
            for t in range(SSM_L):
                blk = kr if t == 0 else jnp.where(lane >= SSM_CH * t, pltpu.roll(kr, SSM_CH * t, 1), 0.0)
                toep[g, SSM_CH * t:SSM_CH * (t + 1), :] = blk.astype(BF16)
            return carry

        lax.fori_loop(0, groups, expand, 0)

    for p in range(npair):
        lhs = jnp.concatenate([u_ref[2 * p], u_ref[2 * p + 1]], axis=1)
        s = _dot(lhs, wi_ref[p])
        s_re[pl.ds(p, rows, stride=SCAN_PITCH), :] = s[:, :LANES]
        s_im[pl.ds(p, rows, stride=SCAN_PITCH), :] = s[:, LANES:]

    a_re = lam_ref[0]
    a_im = lam_ref[1]
    for b in range(nseq):
        def step(n, carry):
            h_re, h_im = carry
            r0 = pl.multiple_of((b * nchunk + n) * SCAN_PITCH, 8)
            hp_re[pl.ds(r0, npair), :] = h_re
            hp_im[pl.ds(r0, npair), :] = h_im
            n_re = a_re * h_re - a_im * h_im + s_re[pl.ds(r0, npair), :]
            n_im = a_re * h_im + a_im * h_re + s_im[pl.ds(r0, npair), :]
            return n_re, n_im
        h_re, h_im = lax.fori_loop(0, nchunk, step, (h0_ref[b, 0], h0_ref[b, 1]))
        hout_ref[b, 0] = h_re
        hout_ref[b, 1] = h_im

    for p in range(npair):
        hp = jnp.concatenate([hp_re[pl.ds(p, rows, stride=SCAN_PITCH), :],
                              hp_im[pl.ds(p, rows, stride=SCAN_PITCH), :]], axis=1).astype(BF16)
        for g in (2 * p, 2 * p + 1):
            y_ref[g] = (_dot(u_ref[g], toep[g]) + _dot(hp, cpow_ref[g])).astype(BF16)


def _ssm_call(u2, h0, wi, krow, cpow, lam, *, nseq, nchunk):
    groups, total_rows, width = u2.shape
    rows = nseq * nchunk
    npair = groups // 2
    kern = functools.partial(_ssm_kernel, nseq=nseq, nchunk=nchunk)
    return pl.pallas_call(
        kern,
        grid=(total_rows // rows,),
        in_specs=[pl.BlockSpec((groups, rows, width), lambda i: (0, i, 0)),
                  pl.BlockSpec((nseq, 2, npair, LANES), lambda i: (i, 0, 0, 0)),
                  _const_spec(wi.shape), _const_spec(krow.shape), _const_spec(cpow.shape), _const_spec(lam.shape)],
        out_specs=[pl.BlockSpec((groups, rows, width), lambda i: (0, i, 0)),
                   pl.BlockSpec((nseq, 2, npair, LANES), lambda i: (i, 0, 0, 0))],
        out_shape=[jax.ShapeDtypeStruct((groups, total_rows, width), BF16),
                   jax.ShapeDtypeStruct(h0.shape, F32)],
        scratch_shapes=[pltpu.VMEM((groups, width, width), BF16)] + [pltpu.VMEM((rows * SCAN_PITCH, LANES), F32)] * 4,
        compiler_params=_params(1),
        name="ssm",
    )(u2, h0, wi, krow, cpow, lam)


def _kv_dup(a):
    lo = lax.broadcasted_iota(jnp.int32, a.shape, 1) < HEAD_DIM
    sw = pltpu.roll(a, HEAD_DIM, 1)
    return jnp.where(lo, a, sw), jnp.where(lo, sw, a)


def _v_ext(v):
    return jnp.concatenate([v, jnp.ones_like(v)], axis=1).astype(BF16)


def _sink_tail(rows):
    row = lax.broadcasted_iota(jnp.int32, (rows, 2 * LANES), 0)
    lane = lax.broadcasted_iota(jnp.int32, (rows, 2 * LANES), 1)
    return jnp.where(jnp.logical_and(row == 0, lane >= LANES), 1.0, 0.0).astype(BF16)


def _sink_attention(sink_ref, problems, nkeys):
    rep = N_HEADS // N_KV_HEADS
    col = lax.broadcasted_iota(jnp.int32, (1, KEYS_PAD), 1)
    real = col < nkeys
    scores = []
    for g, qa, qb, kb, vb, valid, store in problems:
        half = qa.shape[0]
        qst = jnp.concatenate([qa, qb], axis=0)
        lo = lax.broadcasted_iota(jnp.int32, qst.shape, 1) < HEAD_DIM
        zero = jnp.zeros_like(qst)
        q4 = jnp.concatenate([jnp.where(lo, qst, zero), jnp.where(lo, zero, qst)], axis=0)
        s = lax.dot_general(q4, kb, (((1,), (1,)), ((), ())), preferred_element_type=F32)
        keep = real if valid is None else jnp.logical_and(real, valid)
        first = 0 if valid is not None else (nkeys // LANES) * LANES
        blocks = []
        for e in range(2):
            for jj in range(2):
                fill = jnp.where(col == nkeys, sink_ref[rep * g + 2 * jj + e] * LOG2E, -jnp.inf)
                sb = s[(2 * e + jj) * half:(2 * e + jj + 1) * half]
                fixed = jnp.where(keep[:, first:], sb[:, first:], fill[:, first:])
                blocks.append(fixed if first == 0 else jnp.concatenate([sb[:, :first], fixed], axis=1))
        scores.append(jnp.concatenate(blocks, axis=0))
    probs = []
    for s in scores:
        probs.append(jnp.exp2(s - jnp.max(s, axis=-1, keepdims=True)).astype(BF16))
    for p, (g, qa, qb, kb, vb, valid, store) in zip(probs, problems):
        o4 = _dot(p, vb)
        h2 = 2 * qa.shape[0]
        lo = lax.broadcasted_iota(jnp.int32, (h2, LANES), 1) < HEAD_DIM
        store(jnp.where(lo, o4[:h2, :LANES], o4[h2:, :LANES]) / jnp.where(lo, o4[:h2, LANES:], o4[h2:, LANES:]))


def _attn_prompt(q_ref, kc_ref, kp_ref, vc_ref, vp_ref, sink_ref, kz, vz, attn, tile):
    band = CHUNK + WINDOW
    kd = _kv_dup(jnp.concatenate([kp_ref[0], kc_ref[0]], axis=0).astype(F32))
    vd = _kv_dup(jnp.concatenate([vp_ref[0], vc_ref[0]], axis=0).astype(F32))
    tail = _sink_tail(KEYS_PAD - band)
    for g in range(2):
        kz[g, :tile + WINDOW] = kd[g].astype(BF16)
        kz[g, tile + WINDOW:] = jnp.zeros((KEYS_PAD - band, LANES), BF16)
        vext = _v_ext(vd[g])
        for c in range(tile // CHUNK):
            vz[g, c, :band] = vext[c * CHUNK:c * CHUNK + band]
            vz[g, c, band:] = tail
    col_chunk = lax.broadcasted_iota(jnp.int32, (1, KEYS_PAD), 1) // CHUNK
    later_tile = pl.program_id(1) > 0
    problems = []
    for c in range(tile // CHUNK):
        r0 = c * CHUNK
        valid = None if c >= WINDOW // CHUNK else jnp.logical_or(col_chunk + c >= WINDOW // CHUNK, later_tile)
        for g in range(2):
            c0 = 2 * LANES * g

            def store(o, r0=r0, c0=c0):
                attn[r0:r0 + CHUNK, c0:c0 + LANES] = o[:CHUNK]
                attn[r0:r0 + CHUNK, c0 + LANES:c0 + 2 * LANES] = o[CHUNK:]

            problems.append((g, q_ref[0, r0:r0 + CHUNK, c0:c0 + LANES],
                             q_ref[0, r0:r0 + CHUNK, c0 + LANES:c0 + 2 * LANES],
                             kz[g, r0:r0 + KEYS_PAD, :], vz[g, c], valid, store))
    _sink_attention(sink_ref, problems, band)


def _attn_sample(q_ref, kc_ref, kp_ref, vc_ref, vp_ref, sink_ref, attn, nseq, seq):
    nkeys = WINDOW + seq
    tail = _sink_tail(KEYS_PAD - nkeys)
    problems = []
    for b in range(nseq):
        r0 = b * seq
        kd = _kv_dup(jnp.concatenate([kp_ref[b], kc_ref[0, r0:r0 + seq, :]], axis=0).astype(F32))
        vd = _kv_dup(jnp.concatenate([vp_ref[b], vc_ref[0, r0:r0 + seq, :]], axis=0).astype(F32))
        for g in range(2):
            c0 = 2 * LANES * g

            def store(o, r0=r0, c0=c0):
                attn[r0:r0 + seq, c0:c0 + LANES] = o[:seq]
                attn[r0:r0 + seq, c0 + LANES:c0 + 2 * LANES] = o[seq:]

            kb = jnp.concatenate([kd[g].astype(BF16), jnp.zeros((KEYS_PAD - nkeys, LANES), BF16)], axis=0)
            vb = jnp.concatenate([_v_ext(vd[g]), tail], axis=0)
            problems.append((g, q_ref[0, r0:r0 + seq, c0:c0 + LANES], q_ref[0, r0:r0 + seq, c0 + LANES:c0 + 2 * LANES],
                             kb, vb, None, store))
    _sink_attention(sink_ref, problems, nkeys)


def _gelu_tanh(x):
    return 0.5 * x * (1.0 + jnp.tanh(math.sqrt(2.0 / math.pi) * (x + 0.044715 * (x * x * x))))


def _mix_kernel(sink_ref, x_ref, g1_ref, q_ref, kc_ref, kp_ref, vc_ref, vp_ref, y_ref,
                gluw_ref, glub_ref, ga_ref, gs_ref, wout_ref, o_ref, *scratch, tile, nseq):
    if nseq is None:
        yscr, kz, vz, attn = scratch
        _attn_prompt(q_ref, kc_ref, kp_ref, vc_ref, vp_ref, sink_ref, kz, vz, attn, tile)
    else:
        yscr, attn = scratch
        _attn_sample(q_ref, kc_ref, kp_ref, vc_ref, vp_ref, sink_ref, attn, nseq, tile // nseq)
    nslab = yscr.shape[0]
    for o in range(nslab):
        for hh in range(SSM_L // 8):
            outs = _granule_transpose([y_ref[8 * o + g8, :, LANES * hh:LANES * (hh + 1)] for g8 in range(8)])
            for t8 in range(8):
                yscr[o, pl.ds(8 * hh + t8, tile // SSM_L, stride=SSM_L), :] = outs[t8].astype(F32)
    block = min(tile, ROW_BLOCK)
    for r0 in range(0, tile, block):
        rows = slice(r0, r0 + block)
        an = _rms(attn[rows]) * ga_ref[...]
        gl = _gelu_tanh(jnp.concatenate([yscr[o, rows] for o in range(nslab)], axis=1))
        so = gl * jax.nn.sigmoid(_dot(gl.astype(BF16), gluw_ref[...]) + glub_ref[...])
        sn = _rms(so) * gs_ref[...]
        merged = jnp.concatenate([an, sn], axis=1).astype(BF16)
        g1 = g1_ref[0] if g1_ref.shape[1] == 1 else g1_ref[0, rows]
        o_ref[0, rows] = x_ref[0, rows] + g1 * _dot(merged, wout_ref[...])


def _mix_call(sinks, x, g1, q, k, v, k_past, v_past, y, gluw, glub, ga, gs, wout, *, tile, nseq):
    nb, rows, d = x.shape
    aw = q.shape[2]
    kvw = k.shape[2]
    groups, _, cw = y.shape
    tiles = rows // tile
    mrows = g1.shape[1]
    mtile = 1 if mrows == 1 else tile
    mod_spec = pl.BlockSpec((1, mtile, d), (lambda b, i: (b, 0, 0)) if mrows == 1 else (lambda b, i: (b, i, 0)))
    row_spec = lambda c: pl.BlockSpec((1, tile, c), lambda b, i: (b, i, 0))
    chunk_spec = pl.BlockSpec((groups, tile // SSM_L, cw), lambda b, i: (0, b * tiles + i, 0))
    scratch = [pltpu.VMEM((groups * SSM_CH // LANES, tile, LANES), F32)]
    if nseq is None:
        wpt = tile // WINDOW
        past_spec = pl.BlockSpec((1, WINDOW, kvw), lambda b, i: (b, jnp.maximum(i * wpt - 1, 0), 0))
        k_past, v_past = k, v
        scratch += [pltpu.VMEM((2, tile + KEYS_PAD - CHUNK, LANES), BF16),
                    pltpu.VMEM((2, tile // CHUNK, KEYS_PAD, 2 * LANES), BF16), pltpu.VMEM((tile, aw), F32)]
    else:
        past_spec = pl.BlockSpec((nseq, WINDOW, kvw), lambda b, i: (0, 0, 0))
        scratch += [pltpu.VMEM((tile, aw), F32)]
    kern = functools.partial(_mix_kernel, tile=tile, nseq=nseq)
    return pl.pallas_call(
        kern,
        grid=(nb, tiles),
        in_specs=[pl.BlockSpec(memory_space=pltpu.SMEM),
                  row_spec(d), mod_spec, row_spec(aw), row_spec(kvw), past_spec, row_spec(kvw), past_spec,
                  chunk_spec, _const_spec(gluw.shape), _const_spec(glub.shape), _const_spec(ga.shape),
                  _const_spec(gs.shape), _const_spec(wout.shape)],
        out_specs=row_spec(d),
        out_shape=jax.ShapeDtypeStruct((nb, rows, d), F32),
        scratch_shapes=scratch,
        compiler_params=_params(2),
        name="mix",
    )(sinks, x, g1, q, k, k_past, v, v_past, y, gluw, glub, ga, gs, wout)


def _ffn_kernel(x_ref, sc_ref, sh_ref, g2_ref, ln_ref, wg_ref, wu_ref, wd_ref, o_ref, *, ff_bounds):
    x = x_ref[0]
    h = (_rms(x) * (ln_ref[...] * (1.0 + sc_ref[0])) + sh_ref[0]).astype(BF16)
    acc = None
    for c0, c1 in zip(ff_bounds[:-1], ff_bounds[1:]):
        a = _dot(h, wg_ref[:, c0:c1])
        b = _dot(h, wu_ref[:, c0:c1])
        part = _dot((a * jax.nn.sigmoid(a) * b).astype(BF16), wd_ref[c0:c1, :])
        acc = part if acc is None else acc + part
    o_ref[0] = x + g2_ref[0] * acc


def _ffn_call(x, sc, sh, g2, ln_g, wg, wu, wd, *, tile):
    nb, rows, d = x.shape
    dff = wg.shape[1]
    split = -(-dff // (2 * MXU_TILE)) * MXU_TILE
    ff_bounds = (0, split, dff) if split < dff else (0, dff)
    mrows = sc.shape[1]
    mtile = 1 if mrows == 1 else tile
    mod_spec = pl.BlockSpec((1, mtile, d), (lambda b, i: (b, 0, 0)) if mrows == 1 else (lambda b, i: (b, i, 0)))
    row_spec = pl.BlockSpec((1, tile, d), lambda b, i: (b, i, 0))
    kern = functools.partial(_ffn_kernel, ff_bounds=ff_bounds)
    return pl.pallas_call(
        kern,
        grid=(nb, rows // tile),
        in_specs=[row_spec, mod_spec, mod_spec, mod_spec, _const_spec((1, d)),
                  _const_spec(wg.shape), _const_spec(wu.shape), _const_spec(wd.shape)],
        out_specs=row_spec,
        out_shape=jax.ShapeDtypeStruct((nb, rows, d), F32),
        compiler_params=_params(2),
        name="ffn",
    )(x, sc, sh, g2, ln_g, wg, wu, wd)


def _ssm_weights(a_re, a_im, log_dt, b_re, b_im, c_re, c_im, d_skip):
    hp = lax.Precision.HIGHEST
    groups, state = a_re.shape
    ch = b_re.shape[2]
    n = SSM_L
    dt = jnp.exp(log_dt)[:, None]
    z_re, z_im = a_re * dt, a_im * dt
    e_re = jnp.expm1(z_re) * jnp.cos(z_im) - 2.0 * jnp.sin(0.5 * z_im) ** 2
    e_im = jnp.exp(z_re) * jnp.sin(z_im)
    mag = a_re * a_re + a_im * a_im
    coef_re = (e_re * a_re + e_im * a_im) / mag
    coef_im = (e_im * a_re - e_re * a_im) / mag
    bt_re, bt_im = jnp.transpose(b_re, (0, 2, 1)), jnp.transpose(b_im, (0, 2, 1))
    bb_re = coef_re[:, None] * bt_re - coef_im[:, None] * bt_im
    bb_im = coef_re[:, None] * bt_im + coef_im[:, None] * bt_re
    j = jnp.arange(n + 1, dtype=F32)[None, :, None]
    lp_mag = jnp.exp(z_re[:, None] * j)
    lp_re = lp_mag * jnp.cos(z_im[:, None] * j)
    lp_im = lp_mag * jnp.sin(z_im[:, None] * j)
    pr, pi = lp_re[:, :n, None], lp_im[:, :n, None]
    m_re = pr * bb_re[:, None] - pi * bb_im[:, None]
    m_im = pr * bb_im[:, None] + pi * bb_re[:, None]
    kern = (jnp.einsum('gjip,gop->gijo', m_re, c_re, precision=hp)
            - jnp.einsum('gjip,gop->gijo', m_im, c_im, precision=hp))
    kern = kern.at[:, :, 0, :].add(jax.vmap(jnp.diag)(d_skip.reshape(groups, ch)))
    krow = kern.reshape(groups, ch, n * ch)
    wi_re = m_re[:, ::-1].reshape(groups // 2, 2, n * ch, state)
    wi_im = m_im[:, ::-1].reshape(groups // 2, 2, n * ch, state)
    zero = jnp.zeros_like(wi_re[:, 0])
    top = jnp.concatenate([wi_re[:, 0], zero, wi_im[:, 0], zero], axis=2)
    bot = jnp.concatenate([zero, wi_re[:, 1], zero, wi_im[:, 1]], axis=2)
    wi_pair = jnp.concatenate([top, bot], axis=1)
    ct_re, ct_im = jnp.transpose(c_re, (0, 2, 1))[:, :, None, :], jnp.transpose(c_im, (0, 2, 1))[:, :, None, :]
    qr = jnp.transpose(lp_re[:, 1:], (0, 2, 1))[:, :, :, None]
    qi = jnp.transpose(lp_im[:, 1:], (0, 2, 1))[:, :, :, None]
    cp_re = (ct_re * qr - ct_im * qi).reshape(groups // 2, 2, state, n * ch)
    cp_im = (ct_re * qi + ct_im * qr).reshape(groups // 2, 2, state, n * ch)
    zc = jnp.zeros_like(cp_re[:, 0])
    even = jnp.concatenate([cp_re[:, 0], zc, -cp_im[:, 0], zc], axis=1)
    odd = jnp.concatenate([zc, cp_re[:, 1], zc, -cp_im[:, 1]], axis=1)
    cpow = jnp.stack([even, odd], axis=1).reshape(groups, 2 * 2 * state, n * ch)
    lam16 = jnp.stack([lp_re[:, n].reshape(groups // 2, 2 * state), lp_im[:, n].reshape(groups // 2, 2 * state)])
    return wi_pair.astype(BF16), krow, cpow.astype(BF16), lam16


def _rope_tables(pos):
    half = HEAD_DIM // 2
    inv = ROPE_THETA ** (-jnp.arange(half, dtype=F32) * 2.0 / HEAD_DIM)
    lane = jnp.arange(LANES)
    ang = pos.astype(F32)[:, None] * inv[lane % half][None, :]
    first = ((lane % HEAD_DIM) < half)[None, :]
    sin = jnp.sin(ang)
    return jnp.cos(ang), jnp.where(first, -sin, 0.0), jnp.where(first, 0.0, sin)


def _state_in(state):
    b, groups, p, _ = state.shape
    return jnp.transpose(state, (0, 3, 1, 2)).reshape(b, 2, groups // 2, 2 * p)


def _state_out(h, groups):
    b = h.shape[0]
    return jnp.transpose(h.reshape(b, 2, groups, -1), (0, 2, 3, 1))


def _stream(x, mods, tabs, past, h0, lw, *, tile, keep, nseq_tile, seqs, seq_len):
    sh1, sc1, g1, sh2, sc2, g2 = mods
    nb, rows, d = x.shape
    groups = lw['groups']
    q, k, v, u2, k_last, v_last = _inproj_call(x, sc1, sh1, lw['ln1'], lw['w_in'], lw['gq'], lw['gk'], lw['ones'],
                                               *tabs, tile=tile, keep=keep)
    y2, h_last = _ssm_call(u2, h0, lw['wi'], lw['krow'], lw['cpow'], lw['lam'],
                           nseq=(1 if nseq_tile is None else nseq_tile), nchunk=seq_len // SSM_L)
    kp, vp = (None, None) if past is None else past
    x1 = _mix_call(lw['sinks'], x, g1, q, k, v, kp, vp, y2, lw['gluw'], lw['glub'], lw['ga'], lw['gs'],
                   lw['w_out'], tile=tile, nseq=nseq_tile)
    out = _ffn_call(x1, sc2, sh2, g2, lw['ln2'], lw['wg'], lw['wu'], lw['wd'], tile=tile)
    return out, k_last, v_last, h_last


def kernel(x_prompt, x_sample, cache_k, cache_v, state_ssm, c_prompt, c_sample, w_ada, b_ada, ln1_g, w_in, q_norm_g, k_norm_g, attn_sinks, ssm_A_re, ssm_A_im, ssm_log_dt, ssm_B_re, ssm_B_im, ssm_C_re, ssm_C_im, ssm_D, ssm_glu_w, ssm_glu_b, attn_out_g, ssm_out_g, w_out, ln2_g, w_gate, w_up, w_down):
    depth = w_ada.shape[0]
    bp, sp, d = x_prompt.shape
    bs, ss, _ = x_sample.shape
    groups = ssm_A_re.shape[1]
    kvw = N_KV_HEADS * HEAD_DIM
    tile_p = min(512, sp)
    seg = jnp.arange(256) // HEAD_DIM
    ones = (seg[:, None] == seg[None, :]).astype(BF16)
    tabs_p = _rope_tables(jnp.arange(sp))
    tabs_s = tuple(jnp.tile(t, (bs, 1)) for t in _rope_tables(PAST_LEN + jnp.arange(ss)))

    yp = x_prompt
    ys = x_sample.reshape(1, bs * ss, d)
    outs = [[] for _ in range(6)]
    for l in range(depth):
        wi, krow, cpow, lam = _ssm_weights(ssm_A_re[l], ssm_A_im[l], ssm_log_dt[l], ssm_B_re[l], ssm_B_im[l],
                                           ssm_C_re[l], ssm_C_im[l], ssm_D[l])
        lw = dict(groups=groups, ln1=ln1_g[l][None], w_in=w_in[l].astype(BF16),
                  gq=jnp.tile(q_norm_g[l], N_HEADS)[None] * (HEAD_DIM ** -0.5 * LOG2E), gk=jnp.tile(k_norm_g[l], N_KV_HEADS)[None], ones=ones,
                  wi=wi, krow=krow, cpow=cpow, lam=lam, sinks=attn_sinks[l],
                  gluw=ssm_glu_w[l].astype(BF16), glub=ssm_glu_b[l][None], ga=attn_out_g[l][None],
                  gs=ssm_out_g[l][None], w_out=w_out[l].astype(BF16), ln2=ln2_g[l][None],
                  wg=w_gate[l].astype(BF16), wu=w_up[l].astype(BF16), wd=w_down[l].astype(BF16))
        mod = _mod_call(jnp.concatenate([c_prompt, c_sample], axis=0), w_ada[l], b_ada[l])
        mods_p = tuple(m[:, None, :] for m in jnp.split(mod[:bp], 6, axis=-1))
        mods_s = tuple(jnp.repeat(m, ss, axis=0)[None] for m in jnp.split(mod[bp:], 6, axis=-1))

        h0_p = jnp.zeros((bp, 2, groups // 2, 2 * SSM_STATE), F32)
        yp, kpl, vpl, hpl = _stream(yp, mods_p, tabs_p, None, h0_p, lw, tile=tile_p, keep=WINDOW,
                                    nseq_tile=None, seqs=bp, seq_len=sp)
        past = (cache_k[l].reshape(bs, WINDOW, kvw).astype(BF16), cache_v[l].reshape(bs, WINDOW, kvw).astype(BF16))
        ys, ksl, vsl, hsl = _stream(ys, mods_s, tabs_s, past, _state_in(state_ssm[l]), lw, tile=bs * ss,
                                    keep=bs * ss, nseq_tile=bs, seqs=bs, seq_len=ss)
        outs[0].append(kpl.reshape(bp, WINDOW, N_KV_HEADS, HEAD_DIM))
        outs[1].append(vpl.reshape(bp, WINDOW, N_KV_HEADS, HEAD_DIM))
        outs[2].append(_state_out(hpl, groups))
        outs[3].append(ksl.reshape(bs, ss, N_KV_HEADS, HEAD_DIM))
        outs[4].append(vsl.reshape(bs, ss, N_KV_HEADS, HEAD_DIM))
        outs[5].append(_state_out(hsl, groups))
    return (yp, ys.reshape(bs, ss, d)) + tuple(jnp.stack(o) for o in outs)
```

```python
import functools
import math

import jax
import jax.numpy as jnp
from jax import lax
from jax.experimental import pallas as pl
from jax.experimental.pallas import tpu as pltpu

F32 = jnp.float32
BF16 = jnp.bfloat16

CHUNK = 64
WINDOW = 128
HEAD_DIM = 64
N_HEADS = 8
N_KV_HEADS = 2
SSM_CH = 16
SSM_STATE = 64
SSM_L = 16
ROPE_THETA = 10000.0
EPS = 1e-6
PAST_LEN = 1024
LANES = 128
MXU_TILE = 256
KEYS_PAD = MXU_TILE
ROW_BLOCK = 256
SCAN_PITCH = 24
LOG2E = math.log2(math.e)
VMEM_LIMIT = 56 * 1024 * 1024
VMEM_LIMIT_FUSED = 60 * 1024 * 1024


def _const_spec(shape):
    nd = len(shape)
    return pl.BlockSpec(shape, lambda *_: (0,) * nd, pipeline_mode=pl.Buffered(1))


def _params(n_grid):
    return pltpu.CompilerParams(dimension_semantics=("arbitrary",) * n_grid, vmem_limit_bytes=VMEM_LIMIT)


def _rms(x):
    return x * lax.rsqrt(jnp.mean(x * x, axis=-1, keepdims=True) + EPS)


def _dot(a, b):
    return jnp.dot(a, b, preferred_element_type=F32)


def _mod_kernel(c_ref, w_ref, b_ref, o_ref):
    c = c_ref[...]
    s = (c * jax.nn.sigmoid(c)).astype(BF16)
    o_ref[...] = _dot(s, w_ref[...].astype(BF16)) + b_ref[...]


def _mod_call(c, w, b):
    rows, d = c.shape
    cols = w.shape[1]
    tile = 1536
    return pl.pallas_call(
        _mod_kernel,
        grid=(cols // tile,),
        in_specs=[pl.BlockSpec((rows, d), lambda j: (0, 0)),
                  pl.BlockSpec((d, tile), lambda j: (0, j)),
                  pl.BlockSpec((1, tile), lambda j: (0, j))],
        out_specs=pl.BlockSpec((rows, tile), lambda j: (0, j)),
        out_shape=jax.ShapeDtypeStruct((rows, cols), F32),
        compiler_params=_params(1),
        name="mod",
    )(c, w, b.reshape(1, cols))


def _head_rms(t, ones_ref):
    width = t.shape[1]
    sq = t * t
    hi = sq.astype(BF16)
    lo = (sq - hi.astype(F32)).astype(BF16)
    parts = []
    for c0 in range(0, width, 256):
        w = min(256, width - c0)
        ones = ones_ref[:w, :w]
        parts.append(_dot(hi[:, c0:c0 + w], ones) + _dot(lo[:, c0:c0 + w], ones))
    ssq = parts[0] if len(parts) == 1 else jnp.concatenate(parts, axis=1)
    return t * lax.rsqrt(ssq * (1.0 / HEAD_DIM) + EPS)


def _rope(t, cos, s1, s2):
    outs = []
    for c0 in range(0, t.shape[1], LANES):
        xb = t[:, c0:c0 + LANES]
        outs.append(xb * cos + pltpu.roll(xb, LANES - HEAD_DIM // 2, 1) * s1 + pltpu.roll(xb, HEAD_DIM // 2, 1) * s2)
    return outs[0] if len(outs) == 1 else jnp.concatenate(outs, axis=1)


def _granule_transpose(arrs):
    gran = lax.broadcasted_iota(jnp.int32, arrs[0].shape, 1) // SSM_CH
    cur = list(arrs)
    for s in (4, 2, 1):
        upper = (gran & s) != 0
        nxt = list(cur)
        for a0 in range(8):
            if a0 & s:
                continue
            lo, hi = cur[a0], cur[a0 + s]
            nxt[a0] = jnp.where(upper, pltpu.roll(hi, SSM_CH * s, 1), lo)
            nxt[a0 + s] = jnp.where(upper, hi, pltpu.roll(lo, LANES - SSM_CH * s, 1))
        cur = nxt
    return cur


def _inproj_kernel(x_ref, sc_ref, sh_ref, ln_ref, w_ref, gq_ref, gk_ref, ones_ref, cos_ref, s1_ref, s2_ref,
                   q_ref, k_ref, v_ref, u_ref, klast_ref, vlast_ref, uscr, *, q_cols, kv_cols, keep):
    tile = x_ref.shape[1]
    block = min(tile, ROW_BLOCK)
    for r0 in range(0, tile, block):
        rows = slice(r0, r0 + block)
        sc = sc_ref[0] if sc_ref.shape[1] == 1 else sc_ref[0, rows]
        sh = sh_ref[0] if sh_ref.shape[1] == 1 else sh_ref[0, rows]
        h = _rms(x_ref[0, rows]) * (ln_ref[...] * (1.0 + sc)) + sh
        proj = _dot(h.astype(BF16), w_ref[...])
        q = proj[:, :q_cols]
        k = proj[:, q_cols:q_cols + kv_cols]
        v = proj[:, q_cols + kv_cols:q_cols + 2 * kv_cols]
        u = proj[:, q_cols + 2 * kv_cols:]
        cos, s1, s2 = cos_ref[rows], s1_ref[rows], s2_ref[rows]
        qr = _rope(_head_rms(q, ones_ref) * gq_ref[...], cos, s1, s2)
        kr = _rope(_head_rms(k, ones_ref) * gk_ref[...], cos, s1, s2)
        q_ref[0, rows] = qr.astype(BF16)
        k_ref[0, rows] = kr.astype(BF16)
        v_ref[0, rows] = v.astype(BF16)
        nchunk = block // SSM_L
        c0 = r0 // SSM_L
        for o in range(u.shape[1] // LANES):
            uscr[o, rows] = u[:, LANES * o:LANES * (o + 1)]
        for o in range(u.shape[1] // LANES):
            for hh in range(SSM_L // 8):
                outs = _granule_transpose([uscr[o, pl.ds(r0 + 8 * hh + t, nchunk, stride=SSM_L), :].astype(BF16)
                                           for t in range(8)])
                for g8 in range(8):
                    u_ref[8 * o + g8, c0:c0 + nchunk, LANES * hh:LANES * (hh + 1)] = outs[g8]
        first = max(r0, tile - keep)
        if first < r0 + block:
            dst = slice(first - (tile - keep), r0 + block - (tile - keep))
            klast_ref[0, dst] = kr[first - r0:, :]
            vlast_ref[0, dst] = v[first - r0:, :]


def _inproj_call(x, sc, sh, ln_g, w_in, gq, gk, ones, cos, s1, s2, *, tile, keep):
    nb, rows, d = x.shape
    in_cols = w_in.shape[1]
    q_cols = N_HEADS * HEAD_DIM
    kv_cols = N_KV_HEADS * HEAD_DIM
    u_cols = in_cols - q_cols - 2 * kv_cols
    mrows = sc.shape[1]
    mtile = 1 if mrows == 1 else tile
    mod_spec = pl.BlockSpec((1, mtile, d), (lambda b, i: (b, 0, 0)) if mrows == 1 else (lambda b, i: (b, i, 0)))
    row_spec = lambda c: pl.BlockSpec((1, tile, c), lambda b, i: (b, i, 0))
    tab_spec = pl.BlockSpec((tile, LANES), lambda b, i: (i, 0))
    last_spec = pl.BlockSpec((1, keep, kv_cols), lambda b, i: (b, 0, 0))
    groups = u_cols // SSM_CH
    tiles = rows // tile
    chunk_spec = pl.BlockSpec((groups, tile // SSM_L, SSM_L * SSM_CH), lambda b, i: (0, b * tiles + i, 0))
    kern = functools.partial(_inproj_kernel, q_cols=q_cols, kv_cols=kv_cols, keep=keep)
    return pl.pallas_call(
        kern,
        grid=(nb, tiles),
        in_specs=[row_spec(d), mod_spec, mod_spec, _const_spec((1, d)), _const_spec((d, in_cols)),
                  _const_spec((1, q_cols)), _const_spec((1, kv_cols)), _const_spec((256, 256)),
                  tab_spec, tab_spec, tab_spec],
        out_specs=[row_spec(q_cols), row_spec(kv_cols), row_spec(kv_cols), chunk_spec, last_spec, last_spec],
        out_shape=[jax.ShapeDtypeStruct((nb, rows, q_cols), BF16),
                   jax.ShapeDtypeStruct((nb, rows, kv_cols), BF16),
                   jax.ShapeDtypeStruct((nb, rows, kv_cols), BF16),
                   jax.ShapeDtypeStruct((groups, nb * rows // SSM_L, SSM_L * SSM_CH), BF16),
                   jax.ShapeDtypeStruct((nb, keep, kv_cols), F32),
                   jax.ShapeDtypeStruct((nb, keep, kv_cols), F32)],
        scratch_shapes=[pltpu.VMEM((u_cols // LANES, tile, LANES), F32)],
        compiler_params=_params(2),
        name="inproj",
    )(x, sc, sh, ln_g, w_in, gq, gk, ones, cos, s1, s2)


def _ssm_kernel(u_ref, h0_ref, wi_ref, krow_ref, cpow_ref, lam_ref, y_ref, hout_ref,
                toep, s_re, s_im, hp_re, hp_im, *, nseq, nchunk):
    groups = u_ref.shape[0]
    rows = nseq * nchunk
    npair = groups // 2

    @pl.when(pl.program_id(0) == 0)
    def _():
        lane = lax.broadcasted_iota(jnp.int32, (SSM_CH, SSM_L * SSM_CH), 1)

        def expand(g, carry):
            kr = krow_ref[g]
            for t in range(SSM_L):
                blk = kr if t == 0 else jnp.where(lane >= SSM_CH * t, pltpu.roll(kr, SSM_CH * t, 1), 0.0)
                toep[g, SSM_CH * t:SSM_CH * (t + 1), :] = blk.astype(BF16)
            return carry

        lax.fori_loop(0, groups, expand, 0)

    for p in range(npair):
        lhs = jnp.concatenate([u_ref[2 * p], u_ref[2 * p + 1]], axis=1)
        s = _dot(lhs, wi_ref[p])
        s_re[pl.ds(p, rows, stride=SCAN_PITCH), :] = s[:, :LANES]
        s_im[pl.ds(p, rows, stride=SCAN_PITCH), :] = s[:, LANES:]

    a_re = lam_ref[0]
    a_im = lam_ref[1]
    for b in range(nseq):
        def step(n, carry):
            h_re, h_im = carry
            r0 = pl.multiple_of((b * nchunk + n) * SCAN_PITCH, 8)
            hp_re[pl.ds(r0, npair), :] = h_re
            hp_im[pl.ds(r0, npair), :] = h_im
            n_re = a_re * h_re - a_im * h_im + s_re[pl.ds(r0, npair), :]
            n_im = a_re * h_im + a_im * h_re + s_im[pl.ds(r0, npair), :]
            return n_re, n_im
        h_re, h_im = lax.fori_loop(0, nchunk, step, (h0_ref[b, 0], h0_ref[b, 1]))
        hout_ref[b, 0] = h_re
        hout_ref[b, 1] = h_im

    for p in range(npair):
        hp = jnp.concatenate([hp_re[pl.ds(p, rows, stride=SCAN_PITCH), :],
                              hp_im[pl.ds(p, rows, stride=SCAN_PITCH), :]], axis=1).astype(BF16)
        for g in (2 * p, 2 * p + 1):
            y_ref[g] = (_dot(u_ref[g], toep[g]) + _dot(hp, cpow_ref[g])).astype(BF16)


def _ssm_call(u2, h0, wi, krow, cpow, lam, *, nseq, nchunk):
    groups, total_rows, width = u2.shape
    rows = nseq * nchunk
    npair = groups // 2
    kern = functools.partial(_ssm_kernel, nseq=nseq, nchunk=nchunk)
    return pl.pallas_call(
        kern,
        grid=(total_rows // rows,),
        in_specs=[pl.BlockSpec((groups, rows, width), lambda i: (0, i, 0)),
                  pl.BlockSpec((nseq, 2, npair, LANES), lambda i: (i, 0, 0, 0)),
                  _const_spec(wi.shape), _const_spec(krow.shape), _const_spec(cpow.shape), _const_spec(lam.shape)],
        out_specs=[pl.BlockSpec((groups, rows, width), lambda i: (0, i, 0)),
                   pl.BlockSpec((nseq, 2, npair, LANES), lambda i: (i, 0, 0, 0))],
        out_shape=[jax.ShapeDtypeStruct((groups, total_rows, width), BF16),
                   jax.ShapeDtypeStruct(h0.shape, F32)],
        scratch_shapes=[pltpu.VMEM((groups, width, width), BF16)] + [pltpu.VMEM((rows * SCAN_PITCH, LANES), F32)] * 4,
        compiler_params=_params(1),
        name="ssm",
    )(u2, h0, wi, krow, cpow, lam)


def _kv_dup(a):
    lo = lax.broadcasted_iota(jnp.int32, a.shape, 1) < HEAD_DIM
    sw = pltpu.roll(a, HEAD_DIM, 1)
    return jnp.where(lo, a, sw), jnp.where(lo, sw, a)


def _v_ext(v):
    return jnp.concatenate([v, jnp.ones_like(v)], axis=1).astype(BF16)


def _sink_tail(rows):
    row = lax.broadcasted_iota(jnp.int32, (rows, 2 * LANES), 0)
    lane = lax.broadcasted_iota(jnp.int32, (rows, 2 * LANES), 1)
    return jnp.where(jnp.logical_and(row == 0, lane >= LANES), 1.0, 0.0).astype(BF16)


def _sink_attention(sink_ref, problems, nkeys):
    rep = N_HEADS // N_KV_HEADS
    col = lax.broadcasted_iota(jnp.int32, (1, KEYS_PAD), 1)
    real = col < nkeys
    scores = []
    for g, qa, qb, kb, vb, valid, store in problems:
        half = qa.shape[0]
        qst = jnp.concatenate([qa, qb], axis=0)
        lo = lax.broadcasted_iota(jnp.int32, qst.shape, 1) < HEAD_DIM
        zero = jnp.zeros_like(qst)
        q4 = jnp.concatenate([jnp.where(lo, qst, zero), jnp.where(lo, zero, qst)], axis=0)
        s = lax.dot_general(q4, kb, (((1,), (1,)), ((), ())), preferred_element_type=F32)
        keep = real if valid is None else jnp.logical_and(real, valid)
        first = 0 if valid is not None else (nkeys // LANES) * LANES
        blocks = []
        for e in range(2):
            for jj in range(2):
                fill = jnp.where(col == nkeys, sink_ref[rep * g + 2 * jj + e] * LOG2E, -jnp.inf)
                sb = s[(2 * e + jj) * half:(2 * e + jj + 1) * half]
                fixed = jnp.where(keep[:, first:], sb[:, first:], fill[:, first:])
                blocks.append(fixed if first == 0 else jnp.concatenate([sb[:, :first], fixed], axis=1))
        scores.append(jnp.concatenate(blocks, axis=0))
    probs = []
    for s in scores:
        probs.append(jnp.exp2(s - jnp.max(s, axis=-1, keepdims=True)).astype(BF16))
    for p, (g, qa, qb, kb, vb, valid, store) in zip(probs, problems):
        o4 = _dot(p, vb)
        h2 = 2 * qa.shape[0]
        lo = lax.broadcasted_iota(jnp.int32, (h2, LANES), 1) < HEAD_DIM
        store(jnp.where(lo, o4[:h2, :LANES], o4[h2:, :LANES]) / jnp.where(lo, o4[:h2, LANES:], o4[h2:, LANES:]))


def _attn_prompt(q_ref, kc_ref, kp_ref, vc_ref, vp_ref, sink_ref, kz, vz, attn, tile, later_tile):
    band = CHUNK + WINDOW
    kd = _kv_dup(jnp.concatenate([kp_ref[0], kc_ref[0]], axis=0).astype(F32))
    vd = _kv_dup(jnp.concatenate([vp_ref[0], vc_ref[0]], axis=0).astype(F32))
    tail = _sink_tail(KEYS_PAD - band)
    for g in range(2):
        kz[g, :tile + WINDOW] = kd[g].astype(BF16)
        kz[g, tile + WINDOW:] = jnp.zeros((KEYS_PAD - band, LANES), BF16)
        vext = _v_ext(vd[g])
        for c in range(tile // CHUNK):
            vz[g, c, :band] = vext[c * CHUNK:c * CHUNK + band]
            vz[g, c, band:] = tail
    col_chunk = lax.broadcasted_iota(jnp.int32, (1, KEYS_PAD), 1) // CHUNK
    problems = []
    for c in range(tile // CHUNK):
        r0 = c * CHUNK
        valid = None if c >= WINDOW // CHUNK else jnp.logical_or(col_chunk + c >= WINDOW // CHUNK, later_tile)
        for g in range(2):
            c0 = 2 * LANES * g

            def store(o, r0=r0, c0=c0):
                attn[r0:r0 + CHUNK, c0:c0 + LANES] = o[:CHUNK]
                attn[r0:r0 + CHUNK, c0 + LANES:c0 + 2 * LANES] = o[CHUNK:]

            problems.append((g, q_ref[0, r0:r0 + CHUNK, c0:c0 + LANES],
                             q_ref[0, r0:r0 + CHUNK, c0 + LANES:c0 + 2 * LANES],
                             kz[g, r0:r0 + KEYS_PAD, :], vz[g, c], valid, store))
    _sink_attention(sink_ref, problems, band)


def _attn_sample(q_ref, kc_ref, kp_ref, vc_ref, vp_ref, sink_ref, attn, nseq, seq):
    nkeys = WINDOW + seq
    tail = _sink_tail(KEYS_PAD - nkeys)
    problems = []
    for b in range(nseq):
        r0 = b * seq
        kd = _kv_dup(jnp.concatenate([kp_ref[b], kc_ref[0, r0:r0 + seq, :]], axis=0).astype(F32))
        vd = _kv_dup(jnp.concatenate([vp_ref[b], vc_ref[0, r0:r0 + seq, :]], axis=0).astype(F32))
        for g in range(2):
            c0 = 2 * LANES * g

            def store(o, r0=r0, c0=c0):
                attn[r0:r0 + seq, c0:c0 + LANES] = o[:seq]
                attn[r0:r0 + seq, c0 + LANES:c0 + 2 * LANES] = o[seq:]

            kb = jnp.concatenate([kd[g].astype(BF16), jnp.zeros((KEYS_PAD - nkeys, LANES), BF16)], axis=0)
            vb = jnp.concatenate([_v_ext(vd[g]), tail], axis=0)
            problems.append((g, q_ref[0, r0:r0 + seq, c0:c0 + LANES], q_ref[0, r0:r0 + seq, c0 + LANES:c0 + 2 * LANES],
                             kb, vb, None, store))
    _sink_attention(sink_ref, problems, nkeys)


def _gelu_tanh(x):
    return 0.5 * x * (1.0 + jnp.tanh(math.sqrt(2.0 / math.pi) * (x + 0.044715 * (x * x * x))))


def _mix_body(sink_ref, x_ref, g1_ref, q_ref, kc_ref, kp_ref, vc_ref, vp_ref, y_ref,
              gluw_ref, glub_ref, ga_ref, gs_ref, wout_ref, out, scratch, *, tile, nseq, later_tile=None):
    if nseq is None:
        yscr, kz, vz, attn = scratch
        _attn_prompt(q_ref, kc_ref, kp_ref, vc_ref, vp_ref, sink_ref, kz, vz, attn, tile, later_tile)
    else:
        yscr, attn = scratch
        _attn_sample(q_ref, kc_ref, kp_ref, vc_ref, vp_ref, sink_ref, attn, nseq, tile // nseq)
    nslab = yscr.shape[0]
    for o in range(nslab):
        for hh in range(SSM_L // 8):
            outs = _granule_transpose([y_ref[8 * o + g8, :, LANES * hh:LANES * (hh + 1)] for g8 in range(8)])
            for t8 in range(8):
                yscr[o, pl.ds(8 * hh + t8, tile // SSM_L, stride=SSM_L), :] = outs[t8].astype(F32)
    block = min(tile, ROW_BLOCK)
    for r0 in range(0, tile, block):
        rows = slice(r0, r0 + block)
        an = _rms(attn[rows]) * ga_ref[...]
        gl = _gelu_tanh(jnp.concatenate([yscr[o, rows] for o in range(nslab)], axis=1))
        so = gl * jax.nn.sigmoid(_dot(gl.astype(BF16), gluw_ref[...]) + glub_ref[...])
        sn = _rms(so) * gs_ref[...]
        merged = jnp.concatenate([an, sn], axis=1).astype(BF16)
        g1 = g1_ref[0] if g1_ref.shape[1] == 1 else g1_ref[0, rows]
        out[rows] = x_ref[0, rows] + g1 * _dot(merged, wout_ref[...])


def _mix_kernel(*refs, tile, nseq):
    n_in = 14
    later_tile = None if nseq is not None else pl.program_id(1) > 0
    _mix_body(*refs[:n_in], refs[n_in].at[0], refs[n_in + 1:], tile=tile, nseq=nseq, later_tile=later_tile)


def _mix_call(sinks, x, g1, q, k, v, k_past, v_past, y, gluw, glub, ga, gs, wout, *, tile, nseq):
    nb, rows, d = x.shape
    aw = q.shape[2]
    kvw = k.shape[2]
    groups, _, cw = y.shape
    tiles = rows // tile
    mrows = g1.shape[1]
    mtile = 1 if mrows == 1 else tile
    mod_spec = pl.BlockSpec((1, mtile, d), (lambda b, i: (b, 0, 0)) if mrows == 1 else (lambda b, i: (b, i, 0)))
    row_spec = lambda c: pl.BlockSpec((1, tile, c), lambda b, i: (b, i, 0))
    chunk_spec = pl.BlockSpec((groups, tile // SSM_L, cw), lambda b, i: (0, b * tiles + i, 0))
    scratch = [pltpu.VMEM((groups * SSM_CH // LANES, tile, LANES), F32)]
    if nseq is None:
        wpt = tile // WINDOW
        past_spec = pl.BlockSpec((1, WINDOW, kvw), lambda b, i: (b, jnp.maximum(i * wpt - 1, 0), 0))
        k_past, v_past = k, v
        scratch += [pltpu.VMEM((2, tile + KEYS_PAD - CHUNK, LANES), BF16),
                    pltpu.VMEM((2, tile // CHUNK, KEYS_PAD, 2 * LANES), BF16), pltpu.VMEM((tile, aw), F32)]
    else:
        past_spec = pl.BlockSpec((nseq, WINDOW, kvw), lambda b, i: (0, 0, 0))
        scratch += [pltpu.VMEM((tile, aw), F32)]
    kern = functools.partial(_mix_kernel, tile=tile, nseq=nseq)
    return pl.pallas_call(
        kern,
        grid=(nb, tiles),
        in_specs=[pl.BlockSpec(memory_space=pltpu.SMEM),
                  row_spec(d), mod_spec, row_spec(aw), row_spec(kvw), past_spec, row_spec(kvw), past_spec,
                  chunk_spec, _const_spec(gluw.shape), _const_spec(glub.shape), _const_spec(ga.shape),
                  _const_spec(gs.shape), _const_spec(wout.shape)],
        out_specs=row_spec(d),
        out_shape=jax.ShapeDtypeStruct((nb, rows, d), F32),
        scratch_shapes=scratch,
        compiler_params=_params(2),
        name="mix",
    )(sinks, x, g1, q, k, k_past, v, v_past, y, gluw, glub, ga, gs, wout)


def _ffn_body(x, sc_ref, sh_ref, g2_ref, ln_ref, wg_ref, wu_ref, wd_ref, out, ff_bounds):
    h = (_rms(x[...]) * (ln_ref[...] * (1.0 + sc_ref[0])) + sh_ref[0]).astype(BF16)
    acc = None
    for c0, c1 in zip(ff_bounds[:-1], ff_bounds[1:]):
        a = _dot(h, wg_ref[:, c0:c1])
        b = _dot(h, wu_ref[:, c0:c1])
        part = _dot((a * jax.nn.sigmoid(a) * b).astype(BF16), wd_ref[c0:c1, :])
        acc = part if acc is None else acc + part
    out[...] = x[...] + g2_ref[0] * acc


def _ffn_kernel(x_ref, sc_ref, sh_ref, g2_ref, ln_ref, wg_ref, wu_ref, wd_ref, o_ref, *, ff_bounds):
    _ffn_body(x_ref.at[0], sc_ref, sh_ref, g2_ref, ln_ref, wg_ref, wu_ref, wd_ref, o_ref.at[0], ff_bounds)


def _ff_bounds(dff):
    split = -(-dff // (2 * MXU_TILE)) * MXU_TILE
    return (0, split, dff) if split < dff else (0, dff)


def _mixffn_kernel(*refs, tile, tiles, ff_bounds):
    mix_in, ffn_in = refs[:14], refs[14:21]
    o_ref = refs[21]
    x1_new, x1_prev = refs[22:24]
    step = pl.program_id(0)

    @pl.when(step == 0)
    def _():
        x1_new[...] = jnp.zeros(x1_new.shape, F32)

    x1_prev[...] = x1_new[...]
    _ffn_body(x1_prev, *ffn_in, o_ref.at[0], ff_bounds)
    tile_in_seq = jnp.minimum(step, pl.num_programs(0) - 2) % tiles
    _mix_body(*mix_in, x1_new, refs[24:], tile=tile, nseq=None, later_tile=tile_in_seq > 0)


def _mixffn_call(sinks, x, g1, q, k, v, y, gluw, glub, ga, gs, wout, sc, sh, g2, ln_g, wg, wu, wd, *, tile):
    nb, rows, d = x.shape
    aw = q.shape[2]
    kvw = k.shape[2]
    groups, _, cw = y.shape
    tiles = rows // tile
    last = nb * tiles - 1
    wpt = tile // WINDOW

    def mix_bi(s):
        sm = jnp.minimum(s, last)
        return sm // tiles, sm % tiles

    def ffn_bi(s):
        sf = jnp.maximum(s - 1, 0)
        return sf // tiles, sf % tiles

    row_spec = lambda c: pl.BlockSpec((1, tile, c), lambda s: (*mix_bi(s), 0))
    mod_spec = pl.BlockSpec((1, 1, d), lambda s: (mix_bi(s)[0], 0, 0))
    past_spec = pl.BlockSpec((1, WINDOW, kvw),
                             lambda s: (mix_bi(s)[0], jnp.maximum(mix_bi(s)[1] * wpt - 1, 0), 0))
    chunk_spec = pl.BlockSpec((groups, tile // SSM_L, cw), lambda s: (0, jnp.minimum(s, last), 0))
    ffn_mod = pl.BlockSpec((1, 1, d), lambda s: (ffn_bi(s)[0], 0, 0))
    scratch = [pltpu.VMEM((tile, d), F32), pltpu.VMEM((tile, d), F32),
               pltpu.VMEM((groups * SSM_CH // LANES, tile, LANES), F32),
               pltpu.VMEM((2, tile + KEYS_PAD - CHUNK, LANES), BF16),
               pltpu.VMEM((2, tile // CHUNK, KEYS_PAD, 2 * LANES), BF16), pltpu.VMEM((tile, aw), F32)]
    kern = functools.partial(_mixffn_kernel, tile=tile, tiles=tiles, ff_bounds=_ff_bounds(wg.shape[1]))
    return pl.pallas_call(
        kern,
        grid=(nb * tiles + 1,),
        in_specs=[pl.BlockSpec(memory_space=pltpu.SMEM),
                  row_spec(d), mod_spec, row_spec(aw), row_spec(kvw), past_spec, row_spec(kvw), past_spec,
                  chunk_spec, _const_spec(gluw.shape), _const_spec(glub.shape), _const_spec(ga.shape),
                  _const_spec(gs.shape), _const_spec(wout.shape),
                  ffn_mod, ffn_mod, ffn_mod, _const_spec((1, d)),
                  _const_spec(wg.shape), _const_spec(wu.shape), _const_spec(wd.shape)],
        out_specs=pl.BlockSpec((1, tile, d), lambda s: (*ffn_bi(s), 0)),
        out_shape=jax.ShapeDtypeStruct((nb, rows, d), F32),
        scratch_shapes=scratch,
        compiler_params=pltpu.CompilerParams(dimension_semantics=("arbitrary",), vmem_limit_bytes=VMEM_LIMIT_FUSED),
        name="mixffn",
    )(sinks, x, g1, q, k, k, v, v, y, gluw, glub, ga, gs, wout, sc, sh, g2, ln_g, wg, wu, wd)


def _ffn_call(x, sc, sh, g2, ln_g, wg, wu, wd, *, tile):
    nb, rows, d = x.shape
    ff_bounds = _ff_bounds(wg.shape[1])
    mrows = sc.shape[1]
    mtile = 1 if mrows == 1 else tile
    mod_spec = pl.BlockSpec((1, mtile, d), (lambda b, i: (b, 0, 0)) if mrows == 1 else (lambda b, i: (b, i, 0)))
    row_spec = pl.BlockSpec((1, tile, d), lambda b, i: (b, i, 0))
    kern = functools.partial(_ffn_kernel, ff_bounds=ff_bounds)
    return pl.pallas_call(
        kern,
        grid=(nb, rows // tile),
        in_specs=[row_spec, mod_spec, mod_spec, mod_spec, _const_spec((1, d)),
                  _const_spec(wg.shape), _const_spec(wu.shape), _const_spec(wd.shape)],
        out_specs=row_spec,
        out_shape=jax.ShapeDtypeStruct((nb, rows, d), F32),
        compiler_params=_params(2),
        name="ffn",
    )(x, sc, sh, g2, ln_g, wg, wu, wd)


def _ssm_weights(a_re, a_im, log_dt, b_re, b_im, c_re, c_im, d_skip):
    hp = lax.Precision.HIGHEST
    groups, state = a_re.shape
    ch = b_re.shape[2]
    n = SSM_L
    dt = jnp.exp(log_dt)[:, None]
    z_re, z_im = a_re * dt, a_im * dt
    e_re = jnp.expm1(z_re) * jnp.cos(z_im) - 2.0 * jnp.sin(0.5 * z_im) ** 2
    e_im = jnp.exp(z_re) * jnp.sin(z_im)
    mag = a_re * a_re + a_im * a_im
    coef_re = (e_re * a_re + e_im * a_im) / mag
    coef_im = (e_im * a_re - e_re * a_im) / mag
    bt_re, bt_im = jnp.transpose(b_re, (0, 2, 1)), jnp.transpose(b_im, (0, 2, 1))
    bb_re = coef_re[:, None] * bt_re - coef_im[:, None] * bt_im
    bb_im = coef_re[:, None] * bt_im + coef_im[:, None] * bt_re
    j = jnp.arange(n + 1, dtype=F32)[None, :, None]
    lp_mag = jnp.exp(z_re[:, None] * j)
    lp_re = lp_mag * jnp.cos(z_im[:, None] * j)
    lp_im = lp_mag * jnp.sin(z_im[:, None] * j)
    pr, pi = lp_re[:, :n, None], lp_im[:, :n, None]
    m_re = pr * bb_re[:, None] - pi * bb_im[:, None]
    m_im = pr * bb_im[:, None] + pi * bb_re[:, None]
    kern = (jnp.einsum('gjip,gop->gijo', m_re, c_re, precision=hp)
            - jnp.einsum('gjip,gop->gijo', m_im, c_im, precision=hp))
    kern = kern.at[:, :, 0, :].add(jax.vmap(jnp.diag)(d_skip.reshape(groups, ch)))
    krow = kern.reshape(groups, ch, n * ch)
    wi_re = m_re[:, ::-1].reshape(groups // 2, 2, n * ch, state)
    wi_im = m_im[:, ::-1].reshape(groups // 2, 2, n * ch, state)
    zero = jnp.zeros_like(wi_re[:, 0])
    top = jnp.concatenate([wi_re[:, 0], zero, wi_im[:, 0], zero], axis=2)
    bot = jnp.concatenate([zero, wi_re[:, 1], zero, wi_im[:, 1]], axis=2)
    wi_pair = jnp.concatenate([top, bot], axis=1)
    ct_re, ct_im = jnp.transpose(c_re, (0, 2, 1))[:, :, None, :], jnp.transpose(c_im, (0, 2, 1))[:, :, None, :]
    qr = jnp.transpose(lp_re[:, 1:], (0, 2, 1))[:, :, :, None]
    qi = jnp.transpose(lp_im[:, 1:], (0, 2, 1))[:, :, :, None]
    cp_re = (ct_re * qr - ct_im * qi).reshape(groups // 2, 2, state, n * ch)
    cp_im = (ct_re * qi + ct_im * qr).reshape(groups // 2, 2, state, n * ch)
    zc = jnp.zeros_like(cp_re[:, 0])
    even = jnp.concatenate([cp_re[:, 0], zc, -cp_im[:, 0], zc], axis=1)
    odd = jnp.concatenate([zc, cp_re[:, 1], zc, -cp_im[:, 1]], axis=1)
    cpow = jnp.stack([even, odd], axis=1).reshape(groups, 2 * 2 * state, n * ch)
    lam16 = jnp.stack([lp_re[:, n].reshape(groups // 2, 2 * state), lp_im[:, n].reshape(groups // 2, 2 * state)])
    return wi_pair.astype(BF16), krow, cpow.astype(BF16), lam16


def _rope_tables(pos):
    half = HEAD_DIM // 2
    inv = ROPE_THETA ** (-jnp.arange(half, dtype=F32) * 2.0 / HEAD_DIM)
    lane = jnp.arange(LANES)
    ang = pos.astype(F32)[:, None] * inv[lane % half][None, :]
    first = ((lane % HEAD_DIM) < half)[None, :]
    sin = jnp.sin(ang)
    return jnp.cos(ang), jnp.where(first, -sin, 0.0), jnp.where(first, 0.0, sin)


def _state_in(state):
    b, groups, p, _ = state.shape
    return jnp.transpose(state, (0, 3, 1, 2)).reshape(b, 2, groups // 2, 2 * p)


def _state_out(h, groups):
    b = h.shape[0]
    return jnp.transpose(h.reshape(b, 2, groups, -1), (0, 2, 3, 1))


def _stream(x, mods, tabs, past, h0, lw, *, tile, keep, nseq_tile, seqs, seq_len):
    sh1, sc1, g1, sh2, sc2, g2 = mods
    nb, rows, d = x.shape
    groups = lw['groups']
    q, k, v, u2, k_last, v_last = _inproj_call(x, sc1, sh1, lw['ln1'], lw['w_in'], lw['gq'], lw['gk'], lw['ones'],
                                               *tabs, tile=tile, keep=keep)
    y2, h_last = _ssm_call(u2, h0, lw['wi'], lw['krow'], lw['cpow'], lw['lam'],
                           nseq=(1 if nseq_tile is None else nseq_tile), nchunk=seq_len // SSM_L)
    if past is None:
        out = _mixffn_call(lw['sinks'], x, g1, q, k, v, y2, lw['gluw'], lw['glub'], lw['ga'], lw['gs'], lw['w_out'],
                           sc2, sh2, g2, lw['ln2'], lw['wg'], lw['wu'], lw['wd'], tile=tile)
    else:
        x1 = _mix_call(lw['sinks'], x, g1, q, k, v, *past, y2, lw['gluw'], lw['glub'], lw['ga'], lw['gs'],
                       lw['w_out'], tile=tile, nseq=nseq_tile)
        out = _ffn_call(x1, sc2, sh2, g2, lw['ln2'], lw['wg'], lw['wu'], lw['wd'], tile=tile)
    return out, k_last, v_last, h_last


def kernel(x_prompt, x_sample, cache_k, cache_v, state_ssm, c_prompt, c_sample, w_ada, b_ada, ln1_g, w_in, q_norm_g, k_norm_g, attn_sinks, ssm_A_re, ssm_A_im, ssm_log_dt, ssm_B_re, ssm_B_im, ssm_C_re, ssm_C_im, ssm_D, ssm_glu_w, ssm_glu_b, attn_out_g, ssm_out_g, w_out, ln2_g, w_gate, w_up, w_down):
    depth = w_ada.shape[0]
    bp, sp, d = x_prompt.shape
    bs, ss, _ = x_sample.shape
    groups = ssm_A_re.shape[1]
    kvw = N_KV_HEADS * HEAD_DIM
    tile_p = min(512, sp)
    seg = jnp.arange(256) // HEAD_DIM
    ones = (seg[:, None] == seg[None, :]).astype(BF16)
    tabs_p = _rope_tables(jnp.arange(sp))
    tabs_s = tuple(jnp.tile(t, (bs, 1)) for t in _rope_tables(PAST_LEN + jnp.arange(ss)))

    yp = x_prompt
    ys = x_sample.reshape(1, bs * ss, d)
    outs = [[] for _ in range(6)]
    for l in range(depth):
        wi, krow, cpow, lam = _ssm_weights(ssm_A_re[l], ssm_A_im[l], ssm_log_dt[l], ssm_B_re[l], ssm_B_im[l],
                                           ssm_C_re[l], ssm_C_im[l], ssm_D[l])
        lw = dict(groups=groups, ln1=ln1_g[l][None], w_in=w_in[l].astype(BF16),
                  gq=jnp.tile(q_norm_g[l], N_HEADS)[None] * (HEAD_DIM ** -0.5 * LOG2E), gk=jnp.tile(k_norm_g[l], N_KV_HEADS)[None], ones=ones,
                  wi=wi, krow=krow, cpow=cpow, lam=lam, sinks=attn_sinks[l],
                  gluw=ssm_glu_w[l].astype(BF16), glub=ssm_glu_b[l][None], ga=attn_out_g[l][None],
                  gs=ssm_out_g[l][None], w_out=w_out[l].astype(BF16), ln2=ln2_g[l][None],
                  wg=w_gate[l].astype(BF16), wu=w_up[l].astype(BF16), wd=w_down[l].astype(BF16))
        mod = _mod_call(jnp.concatenate([c_prompt, c_sample], axis=0), w_ada[l], b_ada[l])
        mods_p = tuple(m[:, None, :] for m in jnp.split(mod[:bp], 6, axis=-1))
        mods_s = tuple(jnp.repeat(m, ss, axis=0)[None] for m in jnp.split(mod[bp:], 6, axis=-1))

        h0_p = jnp.zeros((bp, 2, groups // 2, 2 * SSM_STATE), F32)
        yp, kpl, vpl, hpl = _stream(yp, mods_p, tabs_p, None, h0_p, lw, tile=tile_p, keep=WINDOW,
                                    nseq_tile=None, seqs=bp, seq_len=sp)
        past = (cache_k[l].reshape(bs, WINDOW, kvw).astype(BF16), cache_v[l].reshape(bs, WINDOW, kvw).astype(BF16))
        ys, ksl, vsl, hsl = _stream(ys, mods_s, tabs_s, past, _state_in(state_ssm[l]), lw, tile=bs * ss,
                                    keep=bs * ss, nseq_tile=bs, seqs=bs, seq_len=ss)
        outs[0].append(kpl.reshape(bp, WINDOW, N_KV_HEADS, HEAD_DIM))
        outs[1].append(vpl.reshape(bp, WINDOW, N_KV_HEADS, HEAD_DIM))
        outs[2].append(_state_out(hpl, groups))
        outs[3].append(ksl.reshape(bs, ss, N_KV_HEADS, HEAD_DIM))
        outs[4].append(vsl.reshape(bs, ss, N_KV_HEADS, HEAD_DIM))
        outs[5].append(_state_out(hsl, groups))
    return (yp, ys.reshape(bs, ss, d)) + tuple(jnp.stack(o) for o in outs)
```

```python
import functools
import math

import jax
import jax.numpy as jnp
from jax import lax
from jax.experimental import pallas as pl
from jax.experimental.pallas import tpu as pltpu

F32 = jnp.float32
BF16 = jnp.bfloat16

CHUNK = 64
WINDOW = 128
HEAD_DIM = 64
N_HEADS = 8
N_KV_HEADS = 2
SSM_CH = 16
SSM_STATE = 64
SSM_L = 16
ROPE_THETA = 10000.0
EPS = 1e-6
PAST_LEN = 1024
LANES = 128
MXU_TILE = 256
KEYS_PAD = MXU_TILE
ROW_BLOCK = 256
SCAN_PITCH = 24
INPROJ_TILE = 2048
MIX_INPUTS = 17
FFN_INPUTS = 7
LOG2E = math.log2(math.e)
VMEM_LIMIT = 56 * 1024 * 1024
VMEM_LIMIT_FUSED = 60 * 1024 * 1024


def _const_spec(shape):
    nd = len(shape)
    return pl.BlockSpec(shape, lambda *_: (0,) * nd, pipeline_mode=pl.Buffered(1))


def _params(n_grid):
    return pltpu.CompilerParams(dimension_semantics=("arbitrary",) * n_grid, vmem_limit_bytes=VMEM_LIMIT)


def _rms(x):
    return x * lax.rsqrt(jnp.mean(x * x, axis=-1, keepdims=True) + EPS)


def _dot(a, b):
    return jnp.dot(a, b, preferred_element_type=F32)


def _mod_kernel(c_ref, w_ref, b_ref, o_ref):
    c = c_ref[...]
    s = (c * jax.nn.sigmoid(c)).astype(BF16)
    o_ref[...] = _dot(s, w_ref[...].astype(BF16)) + b_ref[...]


def _mod_call(c, w, b):
    rows, d = c.shape
    cols = w.shape[1]
    tile = 1536
    return pl.pallas_call(
        _mod_kernel,
        grid=(cols // tile,),
        in_specs=[pl.BlockSpec((rows, d), lambda j: (0, 0)),
                  pl.BlockSpec((d, tile), lambda j: (0, j)),
                  pl.BlockSpec((1, tile), lambda j: (0, j))],
        out_specs=pl.BlockSpec((rows, tile), lambda j: (0, j)),
        out_shape=jax.ShapeDtypeStruct((rows, cols), F32),
        compiler_params=_params(1),
        name="mod",
    )(c, w, b.reshape(1, cols))


def _head_rms(t, ones_ref):
    width = t.shape[1]
    sq = t * t
    hi = sq.astype(BF16)
    lo = (sq - hi.astype(F32)).astype(BF16)
    parts = []
    for c0 in range(0, width, 256):
        w = min(256, width - c0)
        ones = ones_ref[:w, :w]
        parts.append(_dot(hi[:, c0:c0 + w], ones) + _dot(lo[:, c0:c0 + w], ones))
    ssq = parts[0] if len(parts) == 1 else jnp.concatenate(parts, axis=1)
    return t * lax.rsqrt(ssq * (1.0 / HEAD_DIM) + EPS)


def _rope(t, cos, s1, s2):
    outs = []
    for c0 in range(0, t.shape[1], LANES):
        xb = t[:, c0:c0 + LANES]
        outs.append(xb * cos + pltpu.roll(xb, LANES - HEAD_DIM // 2, 1) * s1 + pltpu.roll(xb, HEAD_DIM // 2, 1) * s2)
    return outs[0] if len(outs) == 1 else jnp.concatenate(outs, axis=1)


def _granule_transpose(arrs):
    gran = lax.broadcasted_iota(jnp.int32, arrs[0].shape, 1) // SSM_CH
    cur = list(arrs)
    for s in (4, 2, 1):
        upper = (gran & s) != 0
        nxt = list(cur)
        for a0 in range(8):
            if a0 & s:
                continue
            lo, hi = cur[a0], cur[a0 + s]
            nxt[a0] = jnp.where(upper, pltpu.roll(hi, SSM_CH * s, 1), lo)
            nxt[a0 + s] = jnp.where(upper, hi, pltpu.roll(lo, LANES - SSM_CH * s, 1))
        cur = nxt
    return cur


def _inproj_kernel(x_ref, sc_ref, sh_ref, ln_ref, w_ref, gq_ref, gk_ref, ones_ref, cos_ref, s1_ref, s2_ref,
                   q_ref, k_ref, v_ref, u_ref, klast_ref, vlast_ref, uscr, *, q_cols, kv_cols, keep):
    tile = x_ref.shape[1]
    block = min(tile, ROW_BLOCK)
    for r0 in range(0, tile, block):
        rows = slice(r0, r0 + block)
        sc = sc_ref[0] if sc_ref.shape[1] == 1 else sc_ref[0, rows]
        sh = sh_ref[0] if sh_ref.shape[1] == 1 else sh_ref[0, rows]
        h = _rms(x_ref[0, rows]) * (ln_ref[...] * (1.0 + sc)) + sh
        proj = _dot(h.astype(BF16), w_ref[...])
        q = proj[:, :q_cols]
        k = proj[:, q_cols:q_cols + kv_cols]
        v = proj[:, q_cols + kv_cols:q_cols + 2 * kv_cols]
        u = proj[:, q_cols + 2 * kv_cols:]
        kr = _rope(_head_rms(k, ones_ref) * gk_ref[...], cos_ref[rows], s1_ref[rows], s2_ref[rows])
        q_ref[0, rows] = (_head_rms(q, ones_ref) * gq_ref[...]).astype(BF16)
        k_ref[0, rows] = kr.astype(BF16)
        v_ref[0, rows] = v.astype(BF16)
        nchunk = block // SSM_L
        c0 = r0 // SSM_L
        for o in range(u.shape[1] // LANES):
            uscr[o, rows] = u[:, LANES * o:LANES * (o + 1)]
        for o in range(u.shape[1] // LANES):
            for hh in range(SSM_L // 8):
                outs = _granule_transpose([uscr[o, pl.ds(r0 + 8 * hh + t, nchunk, stride=SSM_L), :].astype(BF16)
                                           for t in range(8)])
                for g8 in range(8):
                    u_ref[8 * o + g8, c0:c0 + nchunk, LANES * hh:LANES * (hh + 1)] = outs[g8]
        first = max(r0, tile - keep)
        if first < r0 + block:
            dst = slice(first - (tile - keep), r0 + block - (tile - keep))
            klast_ref[0, dst] = kr[first - r0:, :]
            vlast_ref[0, dst] = v[first - r0:, :]


def _inproj_call(x, sc, sh, ln_g, w_in, gq, gk, ones, cos, s1, s2, *, tile, keep):
    nb, rows, d = x.shape
    in_cols = w_in.shape[1]
    q_cols = N_HEADS * HEAD_DIM
    kv_cols = N_KV_HEADS * HEAD_DIM
    u_cols = in_cols - q_cols - 2 * kv_cols
    mrows = sc.shape[1]
    mtile = 1 if mrows == 1 else tile
    mod_spec = pl.BlockSpec((1, mtile, d), (lambda b, i: (b, 0, 0)) if mrows == 1 else (lambda b, i: (b, i, 0)))
    row_spec = lambda c: pl.BlockSpec((1, tile, c), lambda b, i: (b, i, 0))
    tab_spec = pl.BlockSpec((tile, LANES), lambda b, i: (i, 0))
    last_spec = pl.BlockSpec((1, keep, kv_cols), lambda b, i: (b, 0, 0))
    groups = u_cols // SSM_CH
    tiles = rows // tile
    chunk_spec = pl.BlockSpec((groups, tile // SSM_L, SSM_L * SSM_CH), lambda b, i: (0, b * tiles + i, 0))
    kern = functools.partial(_inproj_kernel, q_cols=q_cols, kv_cols=kv_cols, keep=keep)
    return pl.pallas_call(
        kern,
        grid=(nb, tiles),
        in_specs=[row_spec(d), mod_spec, mod_spec, _const_spec((1, d)), _const_spec((d, in_cols)),
                  _const_spec((1, q_cols)), _const_spec((1, kv_cols)), _const_spec((256, 256)),
                  tab_spec, tab_spec, tab_spec],
        out_specs=[row_spec(q_cols), row_spec(kv_cols), row_spec(kv_cols), chunk_spec, last_spec, last_spec],
        out_shape=[jax.ShapeDtypeStruct((nb, rows, q_cols), BF16),
                   jax.ShapeDtypeStruct((nb, rows, kv_cols), BF16),
                   jax.ShapeDtypeStruct((nb, rows, kv_cols), BF16),
                   jax.ShapeDtypeStruct((groups, nb * rows // SSM_L, SSM_L * SSM_CH), BF16),
                   jax.ShapeDtypeStruct((nb, keep, kv_cols), F32),
                   jax.ShapeDtypeStruct((nb, keep, kv_cols), F32)],
        scratch_shapes=[pltpu.VMEM((u_cols // LANES, tile, LANES), F32)],
        compiler_params=_params(2),
        name="inproj",
    )(x, sc, sh, ln_g, w_in, gq, gk, ones, cos, s1, s2)


def _ssm_kernel(u_ref, h0_ref, wi_ref, krow_ref, cpow_ref, lam_ref, y_ref, hout_ref,
                toep, s_re, s_im, hp_re, hp_im, *, nseq, nchunk):
    groups = u_ref.shape[0]
    rows = nseq * nchunk
    npair = groups // 2

    @pl.when(pl.program_id(0) == 0)
    def _():
        lane = lax.broadcasted_iota(jnp.int32, (SSM_CH, SSM_L * SSM_CH), 1)

        def expand(g, carry):
            kr = krow_ref[g]
            for t in range(SSM_L):
                blk = kr if t == 0 else jnp.where(lane >= SSM_CH * t, pltpu.roll(kr, SSM_CH * t, 1), 0.0)
                toep[g, SSM_CH * t:SSM_CH * (t + 1), :] = blk.astype(BF16)
            return carry

        lax.fori_loop(0, groups, expand, 0)

    for p in range(npair):
        lhs = jnp.concatenate([u_ref[2 * p], u_ref[2 * p + 1]], axis=1)
        s = _dot(lhs, wi_ref[p])
        s_re[pl.ds(p, rows, stride=SCAN_PITCH), :] = s[:, :LANES]
        s_im[pl.ds(p, rows, stride=SCAN_PITCH), :] = s[:, LANES:]

    a_re = lam_ref[0]
    a_im = lam_ref[1]
    for b in range(nseq):
        def step(n, carry):
            h_re, h_im = carry
            r0 = pl.multiple_of((b * nchunk + n) * SCAN_PITCH, 8)
            hp_re[pl.ds(r0, npair), :] = h_re
            hp_im[pl.ds(r0, npair), :] = h_im
            n_re = a_re * h_re - a_im * h_im + s_re[pl.ds(r0, npair), :]
            n_im = a_re * h_im + a_im * h_re + s_im[pl.ds(r0, npair), :]
            return n_re, n_im
        h_re, h_im = lax.fori_loop(0, nchunk, step, (h0_ref[b, 0], h0_ref[b, 1]))
        hout_ref[b, 0] = h_re
        hout_ref[b, 1] = h_im

    for p in range(npair):
        hp = jnp.concatenate([hp_re[pl.ds(p, rows, stride=SCAN_PITCH), :],
                              hp_im[pl.ds(p, rows, stride=SCAN_PITCH), :]], axis=1).astype(BF16)
        for g in (2 * p, 2 * p + 1):
            y_ref[g] = (_dot(u_ref[g], toep[g]) + _dot(hp, cpow_ref[g])).astype(BF16)


def _ssm_call(u2, h0, wi, krow, cpow, lam, *, nseq, nchunk):
    groups, total_rows, width = u2.shape
    rows = nseq * nchunk
    npair = groups // 2
    kern = functools.partial(_ssm_kernel, nseq=nseq, nchunk=nchunk)
    return pl.pallas_call(
        kern,
        grid=(total_rows // rows,),
        in_specs=[pl.BlockSpec((groups, rows, width), lambda i: (0, i, 0)),
                  pl.BlockSpec((nseq, 2, npair, LANES), lambda i: (i, 0, 0, 0)),
                  _const_spec(wi.shape), _const_spec(krow.shape), _const_spec(cpow.shape), _const_spec(lam.shape)],
        out_specs=[pl.BlockSpec((groups, rows, width), lambda i: (0, i, 0)),
                   pl.BlockSpec((nseq, 2, npair, LANES), lambda i: (i, 0, 0, 0))],
        out_shape=[jax.ShapeDtypeStruct((groups, total_rows, width), BF16),
                   jax.ShapeDtypeStruct(h0.shape, F32)],
        scratch_shapes=[pltpu.VMEM((groups, width, width), BF16)] + [pltpu.VMEM((rows * SCAN_PITCH, LANES), F32)] * 4,
        compiler_params=_params(1),
        name="ssm",
    )(u2, h0, wi, krow, cpow, lam)


def _kv_dup(a):
    lo = lax.broadcasted_iota(jnp.int32, a.shape, 1) < HEAD_DIM
    sw = pltpu.roll(a, HEAD_DIM, 1)
    return jnp.where(lo, a, sw), jnp.where(lo, sw, a)


def _v_ext(v):
    return jnp.concatenate([v, jnp.ones_like(v)], axis=1).astype(BF16)


def _sink_tail(rows):
    row = lax.broadcasted_iota(jnp.int32, (rows, 2 * LANES), 0)
    lane = lax.broadcasted_iota(jnp.int32, (rows, 2 * LANES), 1)
    return jnp.where(jnp.logical_and(row == 0, lane >= LANES), 1.0, 0.0).astype(BF16)


def _sink_attention(sink_ref, problems, nkeys):
    rep = N_HEADS // N_KV_HEADS
    col = lax.broadcasted_iota(jnp.int32, (1, KEYS_PAD), 1)
    real = col < nkeys
    scores = []
    for g, qa, qb, kb, vb, valid, store in problems:
        half = qa.shape[0]
        qst = jnp.concatenate([qa, qb], axis=0)
        lo = lax.broadcasted_iota(jnp.int32, qst.shape, 1) < HEAD_DIM
        zero = jnp.zeros_like(qst)
        q4 = jnp.concatenate([jnp.where(lo, qst, zero), jnp.where(lo, zero, qst)], axis=0)
        s = lax.dot_general(q4, kb, (((1,), (1,)), ((), ())), preferred_element_type=F32)
        keep = real if valid is None else jnp.logical_and(real, valid)
        first = 0 if valid is not None else (nkeys // LANES) * LANES
        blocks = []
        for e in range(2):
            for jj in range(2):
                fill = jnp.where(col == nkeys, sink_ref[rep * g + 2 * jj + e] * LOG2E, -jnp.inf)
                sb = s[(2 * e + jj) * half:(2 * e + jj + 1) * half]
                fixed = jnp.where(keep[:, first:], sb[:, first:], fill[:, first:])
                blocks.append(fixed if first == 0 else jnp.concatenate([sb[:, :first], fixed], axis=1))
        scores.append(jnp.concatenate(blocks, axis=0))
    probs = []
    for s in scores:
        probs.append(jnp.exp2(s - jnp.max(s, axis=-1, keepdims=True)).astype(BF16))
    for p, (g, qa, qb, kb, vb, valid, store) in zip(probs, problems):
        o4 = _dot(p, vb)
        h2 = 2 * qa.shape[0]
        lo = lax.broadcasted_iota(jnp.int32, (h2, LANES), 1) < HEAD_DIM
        store(jnp.where(lo, o4[:h2, :LANES], o4[h2:, :LANES]) / jnp.where(lo, o4[:h2, LANES:], o4[h2:, LANES:]))


def _rotated_q(q_ref, tabs):
    cos_ref, s1_ref, s2_ref = tabs
    return _rope(q_ref[0].astype(F32), cos_ref[...], s1_ref[...], s2_ref[...]).astype(BF16)


def _attn_prompt(q_ref, tabs, kc_ref, kp_ref, vc_ref, vp_ref, sink_ref, kz, vz, attn, tile, later_tile):
    band = CHUNK + WINDOW
    kd = _kv_dup(jnp.concatenate([kp_ref[0], kc_ref[0]], axis=0).astype(F32))
    vd = _kv_dup(jnp.concatenate([vp_ref[0], vc_ref[0]], axis=0).astype(F32))
    tail = _sink_tail(KEYS_PAD - band)
    for g in range(2):
        kz[g, :tile + WINDOW] = kd[g].astype(BF16)
        kz[g, tile + WINDOW:] = jnp.zeros((KEYS_PAD - band, LANES), BF16)
        vext = _v_ext(vd[g])
        for c in range(tile // CHUNK):
            vz[g, c, :band] = vext[c * CHUNK:c * CHUNK + band]
            vz[g, c, band:] = tail
    col_chunk = lax.broadcasted_iota(jnp.int32, (1, KEYS_PAD), 1) // CHUNK
    q = _rotated_q(q_ref, tabs)
    problems = []
    for c in range(tile // CHUNK):
        r0 = c * CHUNK
        valid = None if c >= WINDOW // CHUNK else jnp.logical_or(col_chunk + c >= WINDOW // CHUNK, later_tile)
        for g in range(2):
            c0 = 2 * LANES * g

            def store(o, r0=r0, c0=c0):
                attn[r0:r0 + CHUNK, c0:c0 + LANES] = o[:CHUNK]
                attn[r0:r0 + CHUNK, c0 + LANES:c0 + 2 * LANES] = o[CHUNK:]

            problems.append((g, q[r0:r0 + CHUNK, c0:c0 + LANES], q[r0:r0 + CHUNK, c0 + LANES:c0 + 2 * LANES],
                             kz[g, r0:r0 + KEYS_PAD, :], vz[g, c], valid, store))
    _sink_attention(sink_ref, problems, band)


def _attn_sample(q_ref, tabs, kc_ref, kp_ref, vc_ref, vp_ref, sink_ref, attn, nseq, seq):
    nkeys = WINDOW + seq
    tail = _sink_tail(KEYS_PAD - nkeys)
    q = _rotated_q(q_ref, tabs)
    problems = []
    for b in range(nseq):
        r0 = b * seq
        kd = _kv_dup(jnp.concatenate([kp_ref[b], kc_ref[0, r0:r0 + seq, :]], axis=0).astype(F32))
        vd = _kv_dup(jnp.concatenate([vp_ref[b], vc_ref[0, r0:r0 + seq, :]], axis=0).astype(F32))
        for g in range(2):
            c0 = 2 * LANES * g

            def store(o, r0=r0, c0=c0):
                attn[r0:r0 + seq, c0:c0 + LANES] = o[:seq]
                attn[r0:r0 + seq, c0 + LANES:c0 + 2 * LANES] = o[seq:]

            kb = jnp.concatenate([kd[g].astype(BF16), jnp.zeros((KEYS_PAD - nkeys, LANES), BF16)], axis=0)
            vb = jnp.concatenate([_v_ext(vd[g]), tail], axis=0)
            problems.append((g, q[r0:r0 + seq, c0:c0 + LANES], q[r0:r0 + seq, c0 + LANES:c0 + 2 * LANES],
                             kb, vb, None, store))
    _sink_attention(sink_ref, problems, nkeys)


def _gelu_tanh(x):
    return 0.5 * x * (1.0 + jnp.tanh(math.sqrt(2.0 / math.pi) * (x + 0.044715 * (x * x * x))))


def _mix_body(sink_ref, x_ref, g1_ref, q_ref, kc_ref, kp_ref, vc_ref, vp_ref, y_ref, gluw_ref, glub_ref, ga_ref,
              gs_ref, wout_ref, cos_ref, s1_ref, s2_ref, out, scratch, *, tile, nseq, later_tile=None):
    tabs = (cos_ref, s1_ref, s2_ref)
    if nseq is None:
        yscr, kz, vz, attn = scratch
        _attn_prompt(q_ref, tabs, kc_ref, kp_ref, vc_ref, vp_ref, sink_ref, kz, vz, attn, tile, later_tile)
    else:
        yscr, attn = scratch
        _attn_sample(q_ref, tabs, kc_ref, kp_ref, vc_ref, vp_ref, sink_ref, attn, nseq, tile // nseq)
    nslab = yscr.shape[0]
    for o in range(nslab):
        for hh in range(SSM_L // 8):
            outs = _granule_transpose([y_ref[8 * o + g8, :, LANES * hh:LANES * (hh + 1)] for g8 in range(8)])
            for t8 in range(8):
                yscr[o, pl.ds(8 * hh + t8, tile // SSM_L, stride=SSM_L), :] = outs[t8].astype(F32)
    block = min(tile, ROW_BLOCK)
    for r0 in range(0, tile, block):
        rows = slice(r0, r0 + block)
        an = _rms(attn[rows]) * ga_ref[...]
        gl = _gelu_tanh(jnp.concatenate([yscr[o, rows] for o in range(nslab)], axis=1))
        so = gl * jax.nn.sigmoid(_dot(gl.astype(BF16), gluw_ref[...]) + glub_ref[...])
        sn = _rms(so) * gs_ref[...]
        merged = jnp.concatenate([an, sn], axis=1).astype(BF16)
        g1 = g1_ref[0] if g1_ref.shape[1] == 1 else g1_ref[0, rows]
        out[rows] = x_ref[0, rows] + g1 * _dot(merged, wout_ref[...])


def _mix_kernel(*refs, tile, nseq):
    _mix_body(*refs[:MIX_INPUTS], refs[MIX_INPUTS].at[0], refs[MIX_INPUTS + 1:], tile=tile, nseq=nseq)


def _mix_call(sinks, x, g1, q, k, v, k_past, v_past, y, gluw, glub, ga, gs, wout, tabs, *, tile, nseq):
    nb, rows, d = x.shape
    aw = q.shape[2]
    kvw = k.shape[2]
    groups, _, cw = y.shape
    tiles = rows // tile
    row_spec = lambda c: pl.BlockSpec((1, tile, c), lambda b, i: (b, i, 0))
    chunk_spec = pl.BlockSpec((groups, tile // SSM_L, cw), lambda b, i: (0, b * tiles + i, 0))
    past_spec = pl.BlockSpec((nseq, WINDOW, kvw), lambda b, i: (b * tiles + i, 0, 0))
    tab_spec = pl.BlockSpec((tile, LANES), lambda b, i: (i, 0))
    scratch = [pltpu.VMEM((groups * SSM_CH // LANES, tile, LANES), F32), pltpu.VMEM((tile, aw), F32)]
    kern = functools.partial(_mix_kernel, tile=tile, nseq=nseq)
    return pl.pallas_call(
        kern,
        grid=(nb, tiles),
        in_specs=[pl.BlockSpec(memory_space=pltpu.SMEM),
                  row_spec(d), row_spec(d), row_spec(aw), row_spec(kvw), past_spec, row_spec(kvw), past_spec,
                  chunk_spec, _const_spec(gluw.shape), _const_spec(glub.shape), _const_spec(ga.shape),
                  _const_spec(gs.shape), _const_spec(wout.shape), tab_spec, tab_spec, tab_spec],
        out_specs=row_spec(d),
        out_shape=jax.ShapeDtypeStruct((nb, rows, d), F32),
        scratch_shapes=scratch,
        compiler_params=_params(2),
        name="mix",
    )(sinks, x, g1, q, k, k_past, v, v_past, y, gluw, glub, ga, gs, wout, *tabs)


def _ffn_body(x, sc_ref, sh_ref, g2_ref, ln_ref, wg_ref, wu_ref, wd_ref, out, ff_bounds):
    h = (_rms(x[...]) * (ln_ref[...] * (1.0 + sc_ref[0])) + sh_ref[0]).astype(BF16)
    acc = None
    for c0, c1 in zip(ff_bounds[:-1], ff_bounds[1:]):
        a = _dot(h, wg_ref[:, c0:c1])
        b = _dot(h, wu_ref[:, c0:c1])
        part = _dot((a * jax.nn.sigmoid(a) * b).astype(BF16), wd_ref[c0:c1, :])
        acc = part if acc is None else acc + part
    out[...] = x[...] + g2_ref[0] * acc


def _ffn_kernel(x_ref, sc_ref, sh_ref, g2_ref, ln_ref, wg_ref, wu_ref, wd_ref, o_ref, *, ff_bounds):
    _ffn_body(x_ref.at[0], sc_ref, sh_ref, g2_ref, ln_ref, wg_ref, wu_ref, wd_ref, o_ref.at[0], ff_bounds)


def _ff_bounds(dff):
    split = -(-dff // (2 * MXU_TILE)) * MXU_TILE
    return (0, split, dff) if split < dff else (0, dff)


def _mixffn_kernel(*refs, tile, tiles, ff_bounds):
    mix_in, ffn_in = refs[:MIX_INPUTS], refs[MIX_INPUTS:MIX_INPUTS + FFN_INPUTS]
    o_ref, x1_new, x1_prev, *mix_scratch = refs[MIX_INPUTS + FFN_INPUTS:]
    step = pl.program_id(0)

    @pl.when(step == 0)
    def _():
        x1_new[...] = jnp.zeros(x1_new.shape, F32)

    x1_prev[...] = x1_new[...]
    _ffn_body(x1_prev, *ffn_in, o_ref.at[0], ff_bounds)
    tile_in_seq = jnp.minimum(step, pl.num_programs(0) - 2) % tiles
    _mix_body(*mix_in, x1_new, mix_scratch, tile=tile, nseq=None, later_tile=tile_in_seq > 0)


def _mixffn_call(sinks, x, g1, q, k, v, y, gluw, glub, ga, gs, wout, tabs, sc, sh, g2, ln_g, wg, wu, wd, *, tile):
    nb, rows, d = x.shape
    aw = q.shape[2]
    kvw = k.shape[2]
    groups, _, cw = y.shape
    tiles = rows // tile
    last = nb * tiles - 1
    wpt = tile // WINDOW

    def mix_bi(s):
        sm = jnp.minimum(s, last)
        return sm // tiles, sm % tiles

    def ffn_bi(s):
        sf = jnp.maximum(s - 1, 0)
        return sf // tiles, sf % tiles

    row_spec = lambda c: pl.BlockSpec((1, tile, c), lambda s: (*mix_bi(s), 0))
    mod_spec = pl.BlockSpec((1, 1, d), lambda s: (mix_bi(s)[0], 0, 0))
    past_spec = pl.BlockSpec((1, WINDOW, kvw),
                             lambda s: (mix_bi(s)[0], jnp.maximum(mix_bi(s)[1] * wpt - 1, 0), 0))
    chunk_spec = pl.BlockSpec((groups, tile // SSM_L, cw), lambda s: (0, jnp.minimum(s, last), 0))
    tab_spec = pl.BlockSpec((tile, LANES), lambda s: (mix_bi(s)[1], 0))
    ffn_mod = pl.BlockSpec((1, 1, d), lambda s: (ffn_bi(s)[0], 0, 0))
    scratch = [pltpu.VMEM((tile, d), F32), pltpu.VMEM((tile, d), F32),
               pltpu.VMEM((groups * SSM_CH // LANES, tile, LANES), F32),
               pltpu.VMEM((2, tile + KEYS_PAD - CHUNK, LANES), BF16),
               pltpu.VMEM((2, tile // CHUNK, KEYS_PAD, 2 * LANES), BF16), pltpu.VMEM((tile, aw), F32)]
    kern = functools.partial(_mixffn_kernel, tile=tile, tiles=tiles, ff_bounds=_ff_bounds(wg.shape[1]))
    return pl.pallas_call(
        kern,
        grid=(nb * tiles + 1,),
        in_specs=[pl.BlockSpec(memory_space=pltpu.SMEM),
                  row_spec(d), mod_spec, row_spec(aw), row_spec(kvw), past_spec, row_spec(kvw), past_spec,
                  chunk_spec, _const_spec(gluw.shape), _const_spec(glub.shape), _const_spec(ga.shape),
                  _const_spec(gs.shape), _const_spec(wout.shape), tab_spec, tab_spec, tab_spec,
                  ffn_mod, ffn_mod, ffn_mod, _const_spec((1, d)),
                  _const_spec(wg.shape), _const_spec(wu.shape), _const_spec(wd.shape)],
        out_specs=pl.BlockSpec((1, tile, d), lambda s: (*ffn_bi(s), 0)),
        out_shape=jax.ShapeDtypeStruct((nb, rows, d), F32),
        scratch_shapes=scratch,
        compiler_params=pltpu.CompilerParams(dimension_semantics=("arbitrary",), vmem_limit_bytes=VMEM_LIMIT_FUSED),
        name="mixffn",
    )(sinks, x, g1, q, k, k, v, v, y, gluw, glub, ga, gs, wout, *tabs, sc, sh, g2, ln_g, wg, wu, wd)


def _ffn_call(x, sc, sh, g2, ln_g, wg, wu, wd, *, tile):
    nb, rows, d = x.shape
    ff_bounds = _ff_bounds(wg.shape[1])
    mrows = sc.shape[1]
    mtile = 1 if mrows == 1 else tile
    mod_spec = pl.BlockSpec((1, mtile, d), (lambda b, i: (b, 0, 0)) if mrows == 1 else (lambda b, i: (b, i, 0)))
    row_spec = pl.BlockSpec((1, tile, d), lambda b, i: (b, i, 0))
    kern = functools.partial(_ffn_kernel, ff_bounds=ff_bounds)
    return pl.pallas_call(
        kern,
        grid=(nb, rows // tile),
        in_specs=[row_spec, mod_spec, mod_spec, mod_spec, _const_spec((1, d)),
                  _const_spec(wg.shape), _const_spec(wu.shape), _const_spec(wd.shape)],
        out_specs=row_spec,
        out_shape=jax.ShapeDtypeStruct((nb, rows, d), F32),
        compiler_params=_params(2),
        name="ffn",
    )(x, sc, sh, g2, ln_g, wg, wu, wd)


def _ssm_weights(a_re, a_im, log_dt, b_re, b_im, c_re, c_im, d_skip):
    hp = lax.Precision.HIGHEST
    groups, state = a_re.shape
    ch = b_re.shape[2]
    n = SSM_L
    dt = jnp.exp(log_dt)[:, None]
    z_re, z_im = a_re * dt, a_im * dt
    e_re = jnp.expm1(z_re) * jnp.cos(z_im) - 2.0 * jnp.sin(0.5 * z_im) ** 2
    e_im = jnp.exp(z_re) * jnp.sin(z_im)
    mag = a_re * a_re + a_im * a_im
    coef_re = (e_re * a_re + e_im * a_im) / mag
    coef_im = (e_im * a_re - e_re * a_im) / mag
    bt_re, bt_im = jnp.transpose(b_re, (0, 2, 1)), jnp.transpose(b_im, (0, 2, 1))
    bb_re = coef_re[:, None] * bt_re - coef_im[:, None] * bt_im
    bb_im = coef_re[:, None] * bt_im + coef_im[:, None] * bt_re
    j = jnp.arange(n + 1, dtype=F32)[None, :, None]
    lp_mag = jnp.exp(z_re[:, None] * j)
    lp_re = lp_mag * jnp.cos(z_im[:, None] * j)
    lp_im = lp_mag * jnp.sin(z_im[:, None] * j)
    pr, pi = lp_re[:, :n, None], lp_im[:, :n, None]
    m_re = pr * bb_re[:, None] - pi * bb_im[:, None]
    m_im = pr * bb_im[:, None] + pi * bb_re[:, None]
    kern = (jnp.einsum('gjip,gop->gijo', m_re, c_re, precision=hp)
            - jnp.einsum('gjip,gop->gijo', m_im, c_im, precision=hp))
    kern = kern.at[:, :, 0, :].add(jax.vmap(jnp.diag)(d_skip.reshape(groups, ch)))
    krow = kern.reshape(groups, ch, n * ch)
    wi_re = m_re[:, ::-1].reshape(groups // 2, 2, n * ch, state)
    wi_im = m_im[:, ::-1].reshape(groups // 2, 2, n * ch, state)
    zero = jnp.zeros_like(wi_re[:, 0])
    top = jnp.concatenate([wi_re[:, 0], zero, wi_im[:, 0], zero], axis=2)
    bot = jnp.concatenate([zero, wi_re[:, 1], zero, wi_im[:, 1]], axis=2)
    wi_pair = jnp.concatenate([top, bot], axis=1)
    ct_re, ct_im = jnp.transpose(c_re, (0, 2, 1))[:, :, None, :], jnp.transpose(c_im, (0, 2, 1))[:, :, None, :]
    qr = jnp.transpose(lp_re[:, 1:], (0, 2, 1))[:, :, :, None]
    qi = jnp.transpose(lp_im[:, 1:], (0, 2, 1))[:, :, :, None]
    cp_re = (ct_re * qr - ct_im * qi).reshape(groups // 2, 2, state, n * ch)
    cp_im = (ct_re * qi + ct_im * qr).reshape(groups // 2, 2, state, n * ch)
    zc = jnp.zeros_like(cp_re[:, 0])
    even = jnp.concatenate([cp_re[:, 0], zc, -cp_im[:, 0], zc], axis=1)
    odd = jnp.concatenate([zc, cp_re[:, 1], zc, -cp_im[:, 1]], axis=1)
    cpow = jnp.stack([even, odd], axis=1).reshape(groups, 2 * 2 * state, n * ch)
    lam16 = jnp.stack([lp_re[:, n].reshape(groups // 2, 2 * state), lp_im[:, n].reshape(groups // 2, 2 * state)])
    return wi_pair.astype(BF16), krow, cpow.astype(BF16), lam16


def _rope_tables(pos):
    half = HEAD_DIM // 2
    inv = ROPE_THETA ** (-jnp.arange(half, dtype=F32) * 2.0 / HEAD_DIM)
    lane = jnp.arange(LANES)
    ang = pos.astype(F32)[:, None] * inv[lane % half][None, :]
    first = ((lane % HEAD_DIM) < half)[None, :]
    sin = jnp.sin(ang)
    return jnp.cos(ang), jnp.where(first, -sin, 0.0), jnp.where(first, 0.0, sin)


def _state_in(state):
    b, groups, p, _ = state.shape
    return jnp.transpose(state, (0, 3, 1, 2)).reshape(b, 2, groups // 2, 2 * p)


def _state_out(h, groups):
    b = h.shape[0]
    return jnp.transpose(h.reshape(b, 2, groups, -1), (0, 2, 3, 1))


def _stream(x, mods, tabs, past, h0, lw, *, tile, keep, nseq_tile, seqs, seq_len):
    sh1, sc1, g1, sh2, sc2, g2 = mods
    nb, rows, d = x.shape
    groups = lw['groups']
    q, k, v, u2, k_last, v_last = _inproj_call(x, sc1, sh1, lw['ln1'], lw['w_in'], lw['gq'], lw['gk'], lw['ones'],
                                               *tabs, tile=min(rows, INPROJ_TILE), keep=keep)
    y2, h_last = _ssm_call(u2, h0, lw['wi'], lw['krow'], lw['cpow'], lw['lam'],
                           nseq=(1 if nseq_tile is None else nseq_tile), nchunk=seq_len // SSM_L)
    if past is None:
        out = _mixffn_call(lw['sinks'], x, g1, q, k, v, y2, lw['gluw'], lw['glub'], lw['ga'], lw['gs'], lw['w_out'],
                           tabs, sc2, sh2, g2, lw['ln2'], lw['wg'], lw['wu'], lw['wd'], tile=tile)
    else:
        x1 = _mix_call(lw['sinks'], x, g1, q, k, v, *past, y2, lw['gluw'], lw['glub'], lw['ga'], lw['gs'],
                       lw['w_out'], tabs, tile=tile, nseq=nseq_tile)
        out = _ffn_call(x1, sc2, sh2, g2, lw['ln2'], lw['wg'], lw['wu'], lw['wd'], tile=tile)
    return out, k_last, v_last, h_last


def kernel(x_prompt, x_sample, cache_k, cache_v, state_ssm, c_prompt, c_sample, w_ada, b_ada, ln1_g, w_in, q_norm_g, k_norm_g, attn_sinks, ssm_A_re, ssm_A_im, ssm_log_dt, ssm_B_re, ssm_B_im, ssm_C_re, ssm_C_im, ssm_D, ssm_glu_w, ssm_glu_b, attn_out_g, ssm_out_g, w_out, ln2_g, w_gate, w_up, w_down):
    depth = w_ada.shape[0]
    bp, sp, d = x_prompt.shape
    bs, ss, _ = x_sample.shape
    groups = ssm_A_re.shape[1]
    kvw = N_KV_HEADS * HEAD_DIM
    tile_p = min(512, sp)
    seg = jnp.arange(256) // HEAD_DIM
    ones = (seg[:, None] == seg[None, :]).astype(BF16)
    tabs_p = _rope_tables(jnp.arange(sp))
    tabs_s = tuple(jnp.tile(t, (bs, 1)) for t in _rope_tables(PAST_LEN + jnp.arange(ss)))

    yp = x_prompt
    ys = x_sample.reshape(1, bs * ss, d)
    outs = [[] for _ in range(6)]
    for l in range(depth):
        wi, krow, cpow, lam = _ssm_weights(ssm_A_re[l], ssm_A_im[l], ssm_log_dt[l], ssm_B_re[l], ssm_B_im[l],
                                           ssm_C_re[l], ssm_C_im[l], ssm_D[l])
        lw = dict(groups=groups, ln1=ln1_g[l][None], w_in=w_in[l].astype(BF16),
                  gq=jnp.tile(q_norm_g[l], N_HEADS)[None] * (HEAD_DIM ** -0.5 * LOG2E), gk=jnp.tile(k_norm_g[l], N_KV_HEADS)[None], ones=ones,
                  wi=wi, krow=krow, cpow=cpow, lam=lam, sinks=attn_sinks[l],
                  gluw=ssm_glu_w[l].astype(BF16), glub=ssm_glu_b[l][None], ga=attn_out_g[l][None],
                  gs=ssm_out_g[l][None], w_out=w_out[l].astype(BF16), ln2=ln2_g[l][None],
                  wg=w_gate[l].astype(BF16), wu=w_up[l].astype(BF16), wd=w_down[l].astype(BF16))
        mod = _mod_call(jnp.concatenate([c_prompt, c_sample], axis=0), w_ada[l], b_ada[l])
        mods_p = tuple(m[:, None, :] for m in jnp.split(mod[:bp], 6, axis=-1))
        mods_s = tuple(jnp.repeat(m, ss, axis=0)[None] for m in jnp.split(mod[bp:], 6, axis=-1))

        h0_p = jnp.zeros((bp, 2, groups // 2, 2 * SSM_STATE), F32)
        yp, kpl, vpl, hpl = _stream(yp, mods_p, tabs_p, None, h0_p, lw, tile=tile_p, keep=WINDOW,
                                    nseq_tile=None, seqs=bp, seq_len=sp)
        past = (cache_k[l].reshape(bs, WINDOW, kvw).astype(BF16), cache_v[l].reshape(bs, WINDOW, kvw).astype(BF16))
        ys, ksl, vsl, hsl = _stream(ys, mods_s, tabs_s, past, _state_in(state_ssm[l]), lw, tile=bs * ss,
                                    keep=bs * ss, nseq_tile=bs, seqs=bs, seq_len=ss)
        outs[0].append(kpl.reshape(bp, WINDOW, N_KV_HEADS, HEAD_DIM))
        outs[1].append(vpl.reshape(bp, WINDOW, N_KV_HEADS, HEAD_DIM))
        outs[2].append(_state_out(hpl, groups))
        outs[3].append(ksl.reshape(bs, ss, N_KV_HEADS, HEAD_DIM))
        outs[4].append(vsl.reshape(bs, ss, N_KV_HEADS, HEAD_DIM))
        outs[5].append(_state_out(hsl, groups))
    return (yp, ys.reshape(bs, ss, d)) + tuple(jnp.stack(o) for o in outs)
```

```python
import functools
import math

import jax
import jax.numpy as jnp
from jax import lax
from jax.experimental import pallas as pl
from jax.experimental.pallas import tpu as pltpu

F32 = jnp.float32
BF16 = jnp.bfloat16

CHUNK = 64
WINDOW = 128
HEAD_DIM = 64
N_HEADS = 8
N_KV_HEADS = 2
SSM_CH = 16
SSM_STATE = 64
SSM_L = 16
ROPE_THETA = 10000.0
EPS = 1e-6
PAST_LEN = 1024
LANES = 128
MXU_TILE = 256
KEYS_PAD = MXU_TILE
ROW_BLOCK = 256
SCAN_PITCH = 24
PREP_GROUPS = 4
INPROJ_TILE = 2048
MIX_INPUTS = 17
FFN_INPUTS = 7
LOG2E = math.log2(math.e)
VMEM_LIMIT = 56 * 1024 * 1024
VMEM_LIMIT_FUSED = 60 * 1024 * 1024


def _const_spec(shape):
    nd = len(shape)
    return pl.BlockSpec(shape, lambda *_: (0,) * nd, pipeline_mode=pl.Buffered(1))


def _params(n_grid):
    return pltpu.CompilerParams(dimension_semantics=("arbitrary",) * n_grid, vmem_limit_bytes=VMEM_LIMIT)


def _rms(x):
    return x * lax.rsqrt(jnp.mean(x * x, axis=-1, keepdims=True) + EPS)


def _dot(a, b):
    return jnp.dot(a, b, preferred_element_type=F32)


def _mod_kernel(c_ref, w_ref, b_ref, o_ref):
    c = c_ref[...]
    s = (c * jax.nn.sigmoid(c)).astype(BF16)
    o_ref[...] = _dot(s, w_ref[...].astype(BF16)) + b_ref[...]


def _mod_call(c, w, b):
    rows, d = c.shape
    cols = w.shape[1]
    tile = 1536
    return pl.pallas_call(
        _mod_kernel,
        grid=(cols // tile,),
        in_specs=[pl.BlockSpec((rows, d), lambda j: (0, 0)),
                  pl.BlockSpec((d, tile), lambda j: (0, j)),
                  pl.BlockSpec((1, tile), lambda j: (0, j))],
        out_specs=pl.BlockSpec((rows, tile), lambda j: (0, j)),
        out_shape=jax.ShapeDtypeStruct((rows, cols), F32),
        compiler_params=_params(1),
        name="mod",
    )(c, w, b.reshape(1, cols))


def _head_rms(t, ones_ref):
    width = t.shape[1]
    sq = t * t
    hi = sq.astype(BF16)
    lo = (sq - hi.astype(F32)).astype(BF16)
    parts = []
    for c0 in range(0, width, 256):
        w = min(256, width - c0)
        ones = ones_ref[:w, :w]
        parts.append(_dot(hi[:, c0:c0 + w], ones) + _dot(lo[:, c0:c0 + w], ones))
    ssq = parts[0] if len(parts) == 1 else jnp.concatenate(parts, axis=1)
    return t * lax.rsqrt(ssq * (1.0 / HEAD_DIM) + EPS)


def _rope(t, cos, s1, s2):
    outs = []
    for c0 in range(0, t.shape[1], LANES):
        xb = t[:, c0:c0 + LANES]
        outs.append(xb * cos + pltpu.roll(xb, LANES - HEAD_DIM // 2, 1) * s1 + pltpu.roll(xb, HEAD_DIM // 2, 1) * s2)
    return outs[0] if len(outs) == 1 else jnp.concatenate(outs, axis=1)


def _granule_transpose(arrs):
    gran = lax.broadcasted_iota(jnp.int32, arrs[0].shape, 1) // SSM_CH
    cur = list(arrs)
    for s in (4, 2, 1):
        upper = (gran & s) != 0
        nxt = list(cur)
        for a0 in range(8):
            if a0 & s:
                continue
            lo, hi = cur[a0], cur[a0 + s]
            nxt[a0] = jnp.where(upper, pltpu.roll(hi, SSM_CH * s, 1), lo)
            nxt[a0 + s] = jnp.where(upper, hi, pltpu.roll(lo, LANES - SSM_CH * s, 1))
        cur = nxt
    return cur


def _inproj_kernel(x_ref, sc_ref, sh_ref, ln_ref, w_ref, gq_ref, gk_ref, ones_ref, cos_ref, s1_ref, s2_ref,
                   q_ref, k_ref, v_ref, u_ref, klast_ref, vlast_ref, uscr, *, q_cols, kv_cols, keep):
    tile = x_ref.shape[1]
    block = min(tile, ROW_BLOCK)
    for r0 in range(0, tile, block):
        rows = slice(r0, r0 + block)
        sc = sc_ref[0] if sc_ref.shape[1] == 1 else sc_ref[0, rows]
        sh = sh_ref[0] if sh_ref.shape[1] == 1 else sh_ref[0, rows]
        h = _rms(x_ref[0, rows]) * (ln_ref[...] * (1.0 + sc)) + sh
        proj = _dot(h.astype(BF16), w_ref[...])
        q = proj[:, :q_cols]
        k = proj[:, q_cols:q_cols + kv_cols]
        v = proj[:, q_cols + kv_cols:q_cols + 2 * kv_cols]
        u = proj[:, q_cols + 2 * kv_cols:]
        kr = _rope(_head_rms(k, ones_ref) * gk_ref[...], cos_ref[rows], s1_ref[rows], s2_ref[rows])
        q_ref[0, rows] = (_head_rms(q, ones_ref) * gq_ref[...]).astype(BF16)
        k_ref[0, rows] = kr.astype(BF16)
        v_ref[0, rows] = v.astype(BF16)
        nchunk = block // SSM_L
        c0 = r0 // SSM_L
        for o in range(u.shape[1] // LANES):
            uscr[o, rows] = u[:, LANES * o:LANES * (o + 1)]
        for o in range(u.shape[1] // LANES):
            for hh in range(SSM_L // 8):
                outs = _granule_transpose([uscr[o, pl.ds(r0 + 8 * hh + t, nchunk, stride=SSM_L), :].astype(BF16)
                                           for t in range(8)])
                for g8 in range(8):
                    u_ref[8 * o + g8, c0:c0 + nchunk, LANES * hh:LANES * (hh + 1)] = outs[g8]
        first = max(r0, tile - keep)
        if first < r0 + block:
            dst = slice(first - (tile - keep), r0 + block - (tile - keep))
            klast_ref[0, dst] = kr[first - r0:, :]
            vlast_ref[0, dst] = v[first - r0:, :]


def _inproj_call(x, sc, sh, ln_g, w_in, gq, gk, ones, cos, s1, s2, *, tile, keep):
    nb, rows, d = x.shape
    in_cols = w_in.shape[1]
    q_cols = N_HEADS * HEAD_DIM
    kv_cols = N_KV_HEADS * HEAD_DIM
    u_cols = in_cols - q_cols - 2 * kv_cols
    mrows = sc.shape[1]
    mtile = 1 if mrows == 1 else tile
    mod_spec = pl.BlockSpec((1, mtile, d), (lambda b, i: (b, 0, 0)) if mrows == 1 else (lambda b, i: (b, i, 0)))
    row_spec = lambda c: pl.BlockSpec((1, tile, c), lambda b, i: (b, i, 0))
    tab_spec = pl.BlockSpec((tile, LANES), lambda b, i: (i, 0))
    last_spec = pl.BlockSpec((1, keep, kv_cols), lambda b, i: (b, 0, 0))
    groups = u_cols // SSM_CH
    tiles = rows // tile
    chunk_spec = pl.BlockSpec((groups, tile // SSM_L, SSM_L * SSM_CH), lambda b, i: (0, b * tiles + i, 0))
    kern = functools.partial(_inproj_kernel, q_cols=q_cols, kv_cols=kv_cols, keep=keep)
    return pl.pallas_call(
        kern,
        grid=(nb, tiles),
        in_specs=[row_spec(d), mod_spec, mod_spec, _const_spec((1, d)), _const_spec((d, in_cols)),
                  _const_spec((1, q_cols)), _const_spec((1, kv_cols)), _const_spec((256, 256)),
                  tab_spec, tab_spec, tab_spec],
        out_specs=[row_spec(q_cols), row_spec(kv_cols), row_spec(kv_cols), chunk_spec, last_spec, last_spec],
        out_shape=[jax.ShapeDtypeStruct((nb, rows, q_cols), BF16),
                   jax.ShapeDtypeStruct((nb, rows, kv_cols), BF16),
                   jax.ShapeDtypeStruct((nb, rows, kv_cols), BF16),
                   jax.ShapeDtypeStruct((groups, nb * rows // SSM_L, SSM_L * SSM_CH), BF16),
                   jax.ShapeDtypeStruct((nb, keep, kv_cols), F32),
                   jax.ShapeDtypeStruct((nb, keep, kv_cols), F32)],
        scratch_shapes=[pltpu.VMEM((u_cols // LANES, tile, LANES), F32)],
        compiler_params=_params(2),
        name="inproj",
    )(x, sc, sh, ln_g, w_in, gq, gk, ones, cos, s1, s2)


def _ssm_kernel(u_ref, h0_ref, wi_ref, krow_ref, cpow_ref, lam_ref, y_ref, hout_ref,
                toep, s_re, s_im, hp_re, hp_im, *, nseq, nchunk):
    groups = u_ref.shape[0]
    rows = nseq * nchunk
    npair = groups // 2

    @pl.when(pl.program_id(0) == 0)
    def _():
        lane = lax.broadcasted_iota(jnp.int32, (SSM_CH, SSM_L * SSM_CH), 1)

        def expand(g, carry):
            kr = krow_ref[g]
            for t in range(SSM_L):
                blk = kr if t == 0 else jnp.where(lane >= SSM_CH * t, pltpu.roll(kr, SSM_CH * t, 1), 0.0)
                toep[g, SSM_CH * t:SSM_CH * (t + 1), :] = blk.astype(BF16)
            return carry

        lax.fori_loop(0, groups, expand, 0)

    for p in range(npair):
        lhs = jnp.concatenate([u_ref[2 * p], u_ref[2 * p + 1]], axis=1)
        s = _dot(lhs, wi_ref[p])
        s_re[pl.ds(p, rows, stride=SCAN_PITCH), :] = s[:, :LANES]
        s_im[pl.ds(p, rows, stride=SCAN_PITCH), :] = s[:, LANES:]

    a_re = lam_ref[0]
    a_im = lam_ref[1]
    for b in range(nseq):
        def step(n, carry):
            h_re, h_im = carry
            r0 = pl.multiple_of((b * nchunk + n) * SCAN_PITCH, 8)
            hp_re[pl.ds(r0, npair), :] = h_re
            hp_im[pl.ds(r0, npair), :] = h_im
            n_re = a_re * h_re - a_im * h_im + s_re[pl.ds(r0, npair), :]
            n_im = a_re * h_im + a_im * h_re + s_im[pl.ds(r0, npair), :]
            return n_re, n_im
        h_re, h_im = lax.fori_loop(0, nchunk, step, (h0_ref[b, 0], h0_ref[b, 1]))
        hout_ref[b, 0] = h_re
        hout_ref[b, 1] = h_im

    for p in range(npair):
        hp = jnp.concatenate([hp_re[pl.ds(p, rows, stride=SCAN_PITCH), :],
                              hp_im[pl.ds(p, rows, stride=SCAN_PITCH), :]], axis=1).astype(BF16)
        for g in (2 * p, 2 * p + 1):
            y_ref[g] = (_dot(u_ref[g], toep[g]) + _dot(hp, cpow_ref[g])).astype(BF16)


def _ssm_call(u2, h0, wi, krow, cpow, lam, *, nseq, nchunk):
    groups, total_rows, width = u2.shape
    rows = nseq * nchunk
    npair = groups // 2
    kern = functools.partial(_ssm_kernel, nseq=nseq, nchunk=nchunk)
    return pl.pallas_call(
        kern,
        grid=(total_rows // rows,),
        in_specs=[pl.BlockSpec((groups, rows, width), lambda i: (0, i, 0)),
                  pl.BlockSpec((nseq, 2, npair, LANES), lambda i: (i, 0, 0, 0)),
                  _const_spec(wi.shape), _const_spec(krow.shape), _const_spec(cpow.shape), _const_spec(lam.shape)],
        out_specs=[pl.BlockSpec((groups, rows, width), lambda i: (0, i, 0)),
                   pl.BlockSpec((nseq, 2, npair, LANES), lambda i: (i, 0, 0, 0))],
        out_shape=[jax.ShapeDtypeStruct((groups, total_rows, width), BF16),
                   jax.ShapeDtypeStruct(h0.shape, F32)],
        scratch_shapes=[pltpu.VMEM((groups, width, width), BF16)] + [pltpu.VMEM((rows * SCAN_PITCH, LANES), F32)] * 4,
        compiler_params=_params(1),
        name="ssm",
    )(u2, h0, wi, krow, cpow, lam)


def _kv_dup(a):
    lo = lax.broadcasted_iota(jnp.int32, a.shape, 1) < HEAD_DIM
    sw = pltpu.roll(a, HEAD_DIM, 1)
    return jnp.where(lo, a, sw), jnp.where(lo, sw, a)


def _v_ext(v):
    return jnp.concatenate([v, jnp.ones_like(v)], axis=1).astype(BF16)


def _sink_tail(rows):
    row = lax.broadcasted_iota(jnp.int32, (rows, 2 * LANES), 0)
    lane = lax.broadcasted_iota(jnp.int32, (rows, 2 * LANES), 1)
    return jnp.where(jnp.logical_and(row == 0, lane >= LANES), 1.0, 0.0).astype(BF16)


def _sink_attention(sink_ref, problems, nkeys):
    rep = N_HEADS // N_KV_HEADS
    col = lax.broadcasted_iota(jnp.int32, (1, KEYS_PAD), 1)
    real = col < nkeys
    scores = []
    for g, qa, qb, kb, vb, valid, store in problems:
        half = qa.shape[0]
        qst = jnp.concatenate([qa, qb], axis=0)
        lo = lax.broadcasted_iota(jnp.int32, qst.shape, 1) < HEAD_DIM
        zero = jnp.zeros_like(qst)
        q4 = jnp.concatenate([jnp.where(lo, qst, zero), jnp.where(lo, zero, qst)], axis=0)
        s = lax.dot_general(q4, kb, (((1,), (1,)), ((), ())), preferred_element_type=F32)
        keep = real if valid is None else jnp.logical_and(real, valid)
        first = 0 if valid is not None else (nkeys // LANES) * LANES
        blocks = []
        for e in range(2):
            for jj in range(2):
                fill = jnp.where(col == nkeys, sink_ref[rep * g + 2 * jj + e] * LOG2E, -jnp.inf)
                sb = s[(2 * e + jj) * half:(2 * e + jj + 1) * half]
                fixed = jnp.where(keep[:, first:], sb[:, first:], fill[:, first:])
                blocks.append(fixed if first == 0 else jnp.concatenate([sb[:, :first], fixed], axis=1))
        scores.append(jnp.concatenate(blocks, axis=0))
    probs = []
    for s in scores:
        probs.append(jnp.exp2(s - jnp.max(s, axis=-1, keepdims=True)).astype(BF16))
    for p, (g, qa, qb, kb, vb, valid, store) in zip(probs, problems):
        o4 = _dot(p, vb)
        h2 = 2 * qa.shape[0]
        lo = lax.broadcasted_iota(jnp.int32, (h2, LANES), 1) < HEAD_DIM
        store(jnp.where(lo, o4[:h2, :LANES], o4[h2:, :LANES]) / jnp.where(lo, o4[:h2, LANES:], o4[h2:, LANES:]))


def _rotated_q(q_ref, tabs):
    cos_ref, s1_ref, s2_ref = tabs
    return _rope(q_ref[0].astype(F32), cos_ref[...], s1_ref[...], s2_ref[...]).astype(BF16)


def _attn_prompt(q_ref, tabs, kc_ref, kp_ref, vc_ref, vp_ref, sink_ref, kz, vz, attn, tile, later_tile):
    band = CHUNK + WINDOW
    kd = _kv_dup(jnp.concatenate([kp_ref[0], kc_ref[0]], axis=0).astype(F32))
    vd = _kv_dup(jnp.concatenate([vp_ref[0], vc_ref[0]], axis=0).astype(F32))
    tail = _sink_tail(KEYS_PAD - band)
    for g in range(2):
        kz[g, :tile + WINDOW] = kd[g].astype(BF16)
        kz[g, tile + WINDOW:] = jnp.zeros((KEYS_PAD - band, LANES), BF16)
        vext = _v_ext(vd[g])
        for c in range(tile // CHUNK):
            vz[g, c, :band] = vext[c * CHUNK:c * CHUNK + band]
            vz[g, c, band:] = tail
    col_chunk = lax.broadcasted_iota(jnp.int32, (1, KEYS_PAD), 1) // CHUNK
    q = _rotated_q(q_ref, tabs)
    problems = []
    for c in range(tile // CHUNK):
        r0 = c * CHUNK
        valid = None if c >= WINDOW // CHUNK else jnp.logical_or(col_chunk + c >= WINDOW // CHUNK, later_tile)
        for g in range(2):
            c0 = 2 * LANES * g

            def store(o, r0=r0, c0=c0):
                attn[r0:r0 + CHUNK, c0:c0 + LANES] = o[:CHUNK]
                attn[r0:r0 + CHUNK, c0 + LANES:c0 + 2 * LANES] = o[CHUNK:]

            problems.append((g, q[r0:r0 + CHUNK, c0:c0 + LANES], q[r0:r0 + CHUNK, c0 + LANES:c0 + 2 * LANES],
                             kz[g, r0:r0 + KEYS_PAD, :], vz[g, c], valid, store))
    _sink_attention(sink_ref, problems, band)


def _attn_sample(q_ref, tabs, kc_ref, kp_ref, vc_ref, vp_ref, sink_ref, attn, nseq, seq):
    nkeys = WINDOW + seq
    tail = _sink_tail(KEYS_PAD - nkeys)
    q = _rotated_q(q_ref, tabs)
    problems = []
    for b in range(nseq):
        r0 = b * seq
        kd = _kv_dup(jnp.concatenate([kp_ref[b], kc_ref[0, r0:r0 + seq, :]], axis=0).astype(F32))
        vd = _kv_dup(jnp.concatenate([vp_ref[b], vc_ref[0, r0:r0 + seq, :]], axis=0).astype(F32))
        for g in range(2):
            c0 = 2 * LANES * g

            def store(o, r0=r0, c0=c0):
                attn[r0:r0 + seq, c0:c0 + LANES] = o[:seq]
                attn[r0:r0 + seq, c0 + LANES:c0 + 2 * LANES] = o[seq:]

            kb = jnp.concatenate([kd[g].astype(BF16), jnp.zeros((KEYS_PAD - nkeys, LANES), BF16)], axis=0)
            vb = jnp.concatenate([_v_ext(vd[g]), tail], axis=0)
            problems.append((g, q[r0:r0 + seq, c0:c0 + LANES], q[r0:r0 + seq, c0 + LANES:c0 + 2 * LANES],
                             kb, vb, None, store))
    _sink_attention(sink_ref, problems, nkeys)


def _gelu_tanh(x):
    return 0.5 * x * (1.0 + jnp.tanh(math.sqrt(2.0 / math.pi) * (x + 0.044715 * (x * x * x))))


def _mix_body(sink_ref, x_ref, g1_ref, q_ref, kc_ref, kp_ref, vc_ref, vp_ref, y_ref, gluw_ref, glub_ref, ga_ref,
              gs_ref, wout_ref, cos_ref, s1_ref, s2_ref, out, scratch, *, tile, nseq, later_tile=None):
    tabs = (cos_ref, s1_ref, s2_ref)
    if nseq is None:
        yscr, kz, vz, attn = scratch
        _attn_prompt(q_ref, tabs, kc_ref, kp_ref, vc_ref, vp_ref, sink_ref, kz, vz, attn, tile, later_tile)
    else:
        yscr, attn = scratch
        _attn_sample(q_ref, tabs, kc_ref, kp_ref, vc_ref, vp_ref, sink_ref, attn, nseq, tile // nseq)
    nslab = yscr.shape[0]
    for o in range(nslab):
        for hh in range(SSM_L // 8):
            outs = _granule_transpose([y_ref[8 * o + g8, :, LANES * hh:LANES * (hh + 1)] for g8 in range(8)])
            for t8 in range(8):
                yscr[o, pl.ds(8 * hh + t8, tile // SSM_L, stride=SSM_L), :] = outs[t8].astype(F32)
    block = min(tile, ROW_BLOCK)
    for r0 in range(0, tile, block):
        rows = slice(r0, r0 + block)
        an = _rms(attn[rows]) * ga_ref[...]
        gl = _gelu_tanh(jnp.concatenate([yscr[o, rows] for o in range(nslab)], axis=1))
        so = gl * jax.nn.sigmoid(_dot(gl.astype(BF16), gluw_ref[...]) + glub_ref[...])
        sn = _rms(so) * gs_ref[...]
        merged = jnp.concatenate([an, sn], axis=1).astype(BF16)
        g1 = g1_ref[0] if g1_ref.shape[1] == 1 else g1_ref[0, rows]
        out[rows] = x_ref[0, rows] + g1 * _dot(merged, wout_ref[...])


def _mix_kernel(*refs, tile, nseq):
    _mix_body(*refs[:MIX_INPUTS], refs[MIX_INPUTS].at[0], refs[MIX_INPUTS + 1:], tile=tile, nseq=nseq)


def _mix_call(sinks, x, g1, q, k, v, k_past, v_past, y, gluw, glub, ga, gs, wout, tabs, *, tile, nseq):
    nb, rows, d = x.shape
    aw = q.shape[2]
    kvw = k.shape[2]
    groups, _, cw = y.shape
    tiles = rows // tile
    row_spec = lambda c: pl.BlockSpec((1, tile, c), lambda b, i: (b, i, 0))
    chunk_spec = pl.BlockSpec((groups, tile // SSM_L, cw), lambda b, i: (0, b * tiles + i, 0))
    past_spec = pl.BlockSpec((nseq, WINDOW, kvw), lambda b, i: (b * tiles + i, 0, 0))
    tab_spec = pl.BlockSpec((tile, LANES), lambda b, i: (i, 0))
    scratch = [pltpu.VMEM((groups * SSM_CH // LANES, tile, LANES), F32), pltpu.VMEM((tile, aw), F32)]
    kern = functools.partial(_mix_kernel, tile=tile, nseq=nseq)
    return pl.pallas_call(
        kern,
        grid=(nb, tiles),
        in_specs=[pl.BlockSpec(memory_space=pltpu.SMEM),
                  row_spec(d), row_spec(d), row_spec(aw), row_spec(kvw), past_spec, row_spec(kvw), past_spec,
                  chunk_spec, _const_spec(gluw.shape), _const_spec(glub.shape), _const_spec(ga.shape),
                  _const_spec(gs.shape), _const_spec(wout.shape), tab_spec, tab_spec, tab_spec],
        out_specs=row_spec(d),
        out_shape=jax.ShapeDtypeStruct((nb, rows, d), F32),
        scratch_shapes=scratch,
        compiler_params=_params(2),
        name="mix",
    )(sinks, x, g1, q, k, k_past, v, v_past, y, gluw, glub, ga, gs, wout, *tabs)


def _ffn_body(x, sc_ref, sh_ref, g2_ref, ln_ref, wg_ref, wu_ref, wd_ref, out, ff_bounds):
    h = (_rms(x[...]) * (ln_ref[...] * (1.0 + sc_ref[0])) + sh_ref[0]).astype(BF16)
    acc = None
    for c0, c1 in zip(ff_bounds[:-1], ff_bounds[1:]):
        a = _dot(h, wg_ref[:, c0:c1])
        b = _dot(h, wu_ref[:, c0:c1])
        part = _dot((a * jax.nn.sigmoid(a) * b).astype(BF16), wd_ref[c0:c1, :])
        acc = part if acc is None else acc + part
    out[...] = x[...] + g2_ref[0] * acc


def _ffn_kernel(x_ref, sc_ref, sh_ref, g2_ref, ln_ref, wg_ref, wu_ref, wd_ref, o_ref, *, ff_bounds):
    _ffn_body(x_ref.at[0], sc_ref, sh_ref, g2_ref, ln_ref, wg_ref, wu_ref, wd_ref, o_ref.at[0], ff_bounds)


def _ff_bounds(dff):
    split = -(-dff // (2 * MXU_TILE)) * MXU_TILE
    return (0, split, dff) if split < dff else (0, dff)


def _mixffn_kernel(*refs, tile, tiles, ff_bounds):
    mix_in, ffn_in = refs[:MIX_INPUTS], refs[MIX_INPUTS:MIX_INPUTS + FFN_INPUTS]
    o_ref, x1_new, x1_prev, *mix_scratch = refs[MIX_INPUTS + FFN_INPUTS:]
    step = pl.program_id(0)

    @pl.when(step == 0)
    def _():
        x1_new[...] = jnp.zeros(x1_new.shape, F32)

    x1_prev[...] = x1_new[...]
    _ffn_body(x1_prev, *ffn_in, o_ref.at[0], ff_bounds)
    tile_in_seq = jnp.minimum(step, pl.num_programs(0) - 2) % tiles
    _mix_body(*mix_in, x1_new, mix_scratch, tile=tile, nseq=None, later_tile=tile_in_seq > 0)


def _mixffn_call(sinks, x, g1, q, k, v, y, gluw, glub, ga, gs, wout, tabs, sc, sh, g2, ln_g, wg, wu, wd, *, tile):
    nb, rows, d = x.shape
    aw = q.shape[2]
    kvw = k.shape[2]
    groups, _, cw = y.shape
    tiles = rows // tile
    last = nb * tiles - 1
    wpt = tile // WINDOW

    def mix_bi(s):
        sm = jnp.minimum(s, last)
        return sm // tiles, sm % tiles

    def ffn_bi(s):
        sf = jnp.maximum(s - 1, 0)
        return sf // tiles, sf % tiles

    row_spec = lambda c: pl.BlockSpec((1, tile, c), lambda s: (*mix_bi(s), 0))
    mod_spec = pl.BlockSpec((1, 1, d), lambda s: (mix_bi(s)[0], 0, 0))
    past_spec = pl.BlockSpec((1, WINDOW, kvw),
                             lambda s: (mix_bi(s)[0], jnp.maximum(mix_bi(s)[1] * wpt - 1, 0), 0))
    chunk_spec = pl.BlockSpec((groups, tile // SSM_L, cw), lambda s: (0, jnp.minimum(s, last), 0))
    tab_spec = pl.BlockSpec((tile, LANES), lambda s: (mix_bi(s)[1], 0))
    ffn_mod = pl.BlockSpec((1, 1, d), lambda s: (ffn_bi(s)[0], 0, 0))
    scratch = [pltpu.VMEM((tile, d), F32), pltpu.VMEM((tile, d), F32),
               pltpu.VMEM((groups * SSM_CH // LANES, tile, LANES), F32),
               pltpu.VMEM((2, tile + KEYS_PAD - CHUNK, LANES), BF16),
               pltpu.VMEM((2, tile // CHUNK, KEYS_PAD, 2 * LANES), BF16), pltpu.VMEM((tile, aw), F32)]
    kern = functools.partial(_mixffn_kernel, tile=tile, tiles=tiles, ff_bounds=_ff_bounds(wg.shape[1]))
    return pl.pallas_call(
        kern,
        grid=(nb * tiles + 1,),
        in_specs=[pl.BlockSpec(memory_space=pltpu.SMEM),
                  row_spec(d), mod_spec, row_spec(aw), row_spec(kvw), past_spec, row_spec(kvw), past_spec,
                  chunk_spec, _const_spec(gluw.shape), _const_spec(glub.shape), _const_spec(ga.shape),
                  _const_spec(gs.shape), _const_spec(wout.shape), tab_spec, tab_spec, tab_spec,
                  ffn_mod, ffn_mod, ffn_mod, _const_spec((1, d)),
                  _const_spec(wg.shape), _const_spec(wu.shape), _const_spec(wd.shape)],
        out_specs=pl.BlockSpec((1, tile, d), lambda s: (*ffn_bi(s), 0)),
        out_shape=jax.ShapeDtypeStruct((nb, rows, d), F32),
        scratch_shapes=scratch,
        compiler_params=pltpu.CompilerParams(dimension_semantics=("arbitrary",), vmem_limit_bytes=VMEM_LIMIT_FUSED),
        name="mixffn",
    )(sinks, x, g1, q, k, k, v, v, y, gluw, glub, ga, gs, wout, *tabs, sc, sh, g2, ln_g, wg, wu, wd)


def _ffn_call(x, sc, sh, g2, ln_g, wg, wu, wd, *, tile):
    nb, rows, d = x.shape
    ff_bounds = _ff_bounds(wg.shape[1])
    mrows = sc.shape[1]
    mtile = 1 if mrows == 1 else tile
    mod_spec = pl.BlockSpec((1, mtile, d), (lambda b, i: (b, 0, 0)) if mrows == 1 else (lambda b, i: (b, i, 0)))
    row_spec = pl.BlockSpec((1, tile, d), lambda b, i: (b, i, 0))
    kern = functools.partial(_ffn_kernel, ff_bounds=ff_bounds)
    return pl.pallas_call(
        kern,
        grid=(nb, rows // tile),
        in_specs=[row_spec, mod_spec, mod_spec, mod_spec, _const_spec((1, d)),
                  _const_spec(wg.shape), _const_spec(wu.shape), _const_spec(wd.shape)],
        out_specs=row_spec,
        out_shape=jax.ShapeDtypeStruct((nb, rows, d), F32),
        compiler_params=_params(2),
        name="ffn",
    )(x, sc, sh, g2, ln_g, wg, wu, wd)


def _split_bf16(a):
    hi = a.astype(BF16)
    return hi, (a - hi.astype(F32)).astype(BF16)


def _dot3(a, b):
    ah, al = _split_bf16(a)
    bh, bl = _split_bf16(b)
    return _dot(ah, bh) + _dot(ah, bl) + _dot(al, bh)


def _ssm_prep_kernel(*refs):
    for gi in range(refs[0].shape[0]):
        _ssm_prep_group(*[r.at[pl.ds(gi, 1)] for r in refs])


def _ssm_prep_group(logdt_ref, ar_ref, ai_ref, bt_re_ref, bt_im_ref, ct_re_ref, ct_im_ref, d_ref,
                     krow_ref, y_re_ref, y_im_ref, wi_re_ref, wi_im_ref, lam_ref):
    n, ch = SSM_L, SSM_CH
    dt = jnp.exp(logdt_ref[0])
    ar, ai = ar_ref[0], ai_ref[0]
    zr, zi = ar * dt, ai * dt
    ez = jnp.exp(zr)
    e_re = jnp.tanh(0.5 * zr) * (ez + 1.0) * jnp.cos(zi) - 2.0 * jnp.sin(0.5 * zi) ** 2
    e_im = ez * jnp.sin(zi)
    mag = ar * ar + ai * ai
    coef_re = (e_re * ar + e_im * ai) / mag
    coef_im = (e_im * ar - e_re * ai) / mag
    bt_re, bt_im = bt_re_ref[0], bt_im_ref[0]
    bb_re = coef_re * bt_re - coef_im * bt_im
    bb_im = coef_re * bt_im + coef_im * bt_re
    rows = 2 * n
    jc = lax.broadcasted_iota(jnp.int32, (rows, 1), 0).astype(F32)
    mg = jnp.exp(jc * zr)
    pr, pi = mg * jnp.cos(jc * zi), mg * jnp.sin(jc * zi)
    for t in range(n):
        r_re, r_im = pr[n - 1 - t:n - t], pi[n - 1 - t:n - t]
        wi_re_ref[0, ch * t:ch * (t + 1), :] = bb_re * r_re - bb_im * r_im
        wi_im_ref[0, ch * t:ch * (t + 1), :] = bb_re * r_im + bb_im * r_re
    lam_ref[0, 0:1, :] = pr[n:n + 1]
    lam_ref[0, 1:2, :] = pi[n:n + 1]
    lag = lax.broadcasted_iota(jnp.int32, (rows, n * ch), 1) // ch
    jrow = lax.broadcasted_iota(jnp.int32, (rows, n * ch), 0)
    chan = lax.broadcasted_iota(jnp.int32, (ch, n * ch), 1) % ch
    spread_chan = (chan == lax.broadcasted_iota(jnp.int32, (ch, n * ch), 0)).astype(BF16)
    nn, tn = (((1,), (0,)), ((), ())), (((0,), (0,)), ((), ()))

    def spread(a, onehot, dims):
        hi, lo = _split_bf16(a)
        return (lax.dot_general(hi, onehot, dims, preferred_element_type=F32)
                + lax.dot_general(lo, onehot, dims, preferred_element_type=F32))

    def powers(shift):
        onehot = (lag + shift == jrow).astype(BF16)
        return spread(pr, onehot, tn), spread(pi, onehot, tn)

    c_re = spread(ct_re_ref[0], spread_chan, nn)
    c_im = spread(ct_im_ref[0], spread_chan, nn)
    l1_re, l1_im = powers(1)
    y_re_ref[0] = c_re * l1_re - c_im * l1_im
    y_im_ref[0] = c_re * l1_im + c_im * l1_re
    l0_re, l0_im = powers(0)
    k_re = c_re * l0_re - c_im * l0_im
    k_im = c_re * l0_im + c_im * l0_re
    lane = lax.broadcasted_iota(jnp.int32, (ch, n * ch), 1)
    d_lag0 = jnp.where(lane == lax.broadcasted_iota(jnp.int32, (ch, n * ch), 0), d_ref[0], 0.0)
    krow_ref[0] = _dot3(bb_re, k_re) - _dot3(bb_im, k_im) + d_lag0


def _ssm_weights(a_re, a_im, log_dt, b_re, b_im, c_re, c_im, d_skip):
    groups, state = a_re.shape
    ch = b_re.shape[2]
    n = SSM_L
    row = lambda a: a.reshape(groups, 1, state)
    tr =lambda a: jnp.transpose(a, (0, 2, 1))
    d_pad = jnp.pad(d_skip.reshape(groups, 1, ch), ((0, 0), (0, 0), (0, (n - 1) * ch)))
    per_step = math.gcd(groups, PREP_GROUPS)
    blk = lambda *shape: pl.BlockSpec((per_step,) + shape, lambda g: (g, 0, 0))
    krow, y_re, y_im, wi_re, wi_im, lam = pl.pallas_call(
        _ssm_prep_kernel,
        grid=(groups // per_step,),
        in_specs=[blk(1, 1), blk(1, state), blk(1, state),
                  blk(ch, state), blk(ch, state), blk(state, ch), blk(state, ch), blk(1, n * ch)],
        out_specs=[blk(ch, n * ch), blk(state, n * ch), blk(state, n * ch), blk(n * ch, state),
                   blk(n * ch, state), blk(2, state)],
        out_shape=[jax.ShapeDtypeStruct((groups, ch, n * ch), F32),
                   jax.ShapeDtypeStruct((groups, state, n * ch), F32),
                   jax.ShapeDtypeStruct((groups, state, n * ch), F32),
                   jax.ShapeDtypeStruct((groups, n * ch, state), F32),
                   jax.ShapeDtypeStruct((groups, n * ch, state), F32),
                   jax.ShapeDtypeStruct((groups, 2, state), F32)],
        compiler_params=_params(1),
        name="ssm_prep",
    )(log_dt.reshape(groups, 1, 1), row(a_re), row(a_im), tr(b_re), tr(b_im), tr(c_re), tr(c_im), d_pad)
    pair = lambda a: a.reshape((groups // 2, 2) + a.shape[1:])
    wr, wm = pair(wi_re), pair(wi_im)
    zero = jnp.zeros_like(wr[:, 0])
    top = jnp.concatenate([wr[:, 0], zero, wm[:, 0], zero], axis=2)
    bot = jnp.concatenate([zero, wr[:, 1], zero, wm[:, 1]], axis=2)
    wi_pair = jnp.concatenate([top, bot], axis=1).astype(BF16)
    yr, ym = pair(y_re), pair(y_im)
    zc = jnp.zeros_like(yr[:, 0])
    even = jnp.concatenate([yr[:, 0], zc, -ym[:, 0], zc], axis=1)
    odd = jnp.concatenate([zc, yr[:, 1], zc, -ym[:, 1]], axis=1)
    cpow = jnp.stack([even, odd], axis=1).reshape(groups, 4 * state, n * ch).astype(BF16)
    lam16 = jnp.transpose(lam, (1, 0, 2)).reshape(2, groups // 2, 2 * state)
    return wi_pair, krow, cpow, lam16


def _rope_tables(pos):
    half = HEAD_DIM // 2
    inv = ROPE_THETA ** (-jnp.arange(half, dtype=F32) * 2.0 / HEAD_DIM)
    lane = jnp.arange(LANES)
    ang = pos.astype(F32)[:, None] * inv[lane % half][None, :]
    first = ((lane % HEAD_DIM) < half)[None, :]
    sin = jnp.sin(ang)
    return jnp.cos(ang), jnp.where(first, -sin, 0.0), jnp.where(first, 0.0, sin)


def _state_in(state):
    b, groups, p, _ = state.shape
    return jnp.transpose(state, (0, 3, 1, 2)).reshape(b, 2, groups // 2, 2 * p)


def _state_out(h, groups):
    b = h.shape[0]
    return jnp.transpose(h.reshape(b, 2, groups, -1), (0, 2, 3, 1))


def _stream(x, mods, tabs, past, h0, lw, *, tile, keep, nseq_tile, seqs, seq_len):
    sh1, sc1, g1, sh2, sc2, g2 = mods
    nb, rows, d = x.shape
    groups = lw['groups']
    q, k, v, u2, k_last, v_last = _inproj_call(x, sc1, sh1, lw['ln1'], lw['w_in'], lw['gq'], lw['gk'], lw['ones'],
                                               *tabs, tile=min(rows, INPROJ_TILE), keep=keep)
    y2, h_last = _ssm_call(u2, h0, lw['wi'], lw['krow'], lw['cpow'], lw['lam'],
                           nseq=(1 if nseq_tile is None else nseq_tile), nchunk=seq_len // SSM_L)
    if past is None:
        out = _mixffn_call(lw['sinks'], x, g1, q, k, v, y2, lw['gluw'], lw['glub'], lw['ga'], lw['gs'], lw['w_out'],
                           tabs, sc2, sh2, g2, lw['ln2'], lw['wg'], lw['wu'], lw['wd'], tile=tile)
    else:
        x1 = _mix_call(lw['sinks'], x, g1, q, k, v, *past, y2, lw['gluw'], lw['glub'], lw['ga'], lw['gs'],
                       lw['w_out'], tabs, tile=tile, nseq=nseq_tile)
        out = _ffn_call(x1, sc2, sh2, g2, lw['ln2'], lw['wg'], lw['wu'], lw['wd'], tile=tile)
    return out, k_last, v_last, h_last


def kernel(x_prompt, x_sample, cache_k, cache_v, state_ssm, c_prompt, c_sample, w_ada, b_ada, ln1_g, w_in, q_norm_g, k_norm_g, attn_sinks, ssm_A_re, ssm_A_im, ssm_log_dt, ssm_B_re, ssm_B_im, ssm_C_re, ssm_C_im, ssm_D, ssm_glu_w, ssm_glu_b, attn_out_g, ssm_out_g, w_out, ln2_g, w_gate, w_up, w_down):
    depth = w_ada.shape[0]
    bp, sp, d = x_prompt.shape
    bs, ss, _ = x_sample.shape
    groups = ssm_A_re.shape[1]
    kvw = N_KV_HEADS * HEAD_DIM
    tile_p = min(512, sp)
    seg = jnp.arange(256) // HEAD_DIM
    ones = (seg[:, None] == seg[None, :]).astype(BF16)
    tabs_p = _rope_tables(jnp.arange(sp))
    tabs_s = tuple(jnp.tile(t, (bs, 1)) for t in _rope_tables(PAST_LEN + jnp.arange(ss)))

    yp = x_prompt
    ys = x_sample.reshape(1, bs * ss, d)
    outs = [[] for _ in range(6)]
    for l in range(depth):
        wi, krow, cpow, lam = _ssm_weights(ssm_A_re[l], ssm_A_im[l], ssm_log_dt[l], ssm_B_re[l], ssm_B_im[l],
                                           ssm_C_re[l], ssm_C_im[l], ssm_D[l])
        lw = dict(groups=groups, ln1=ln1_g[l][None], w_in=w_in[l].astype(BF16),
                  gq=jnp.tile(q_norm_g[l], N_HEADS)[None] * (HEAD_DIM ** -0.5 * LOG2E), gk=jnp.tile(k_norm_g[l], N_KV_HEADS)[None], ones=ones,
                  wi=wi, krow=krow, cpow=cpow, lam=lam, sinks=attn_sinks[l],
                  gluw=ssm_glu_w[l].astype(BF16), glub=ssm_glu_b[l][None], ga=attn_out_g[l][None],
                  gs=ssm_out_g[l][None], w_out=w_out[l].astype(BF16), ln2=ln2_g[l][None],
                  wg=w_gate[l].astype(BF16), wu=w_up[l].astype(BF16), wd=w_down[l].astype(BF16))
        mod = _mod_call(jnp.concatenate([c_prompt, c_sample], axis=0), w_ada[l], b_ada[l])
        mods_p = tuple(m[:, None, :] for m in jnp.split(mod[:bp], 6, axis=-1))
        mods_s = tuple(jnp.repeat(m, ss, axis=0)[None] for m in jnp.split(mod[bp:], 6, axis=-1))

        h0_p = jnp.zeros((bp, 2, groups // 2, 2 * SSM_STATE), F32)
        yp, kpl, vpl, hpl = _stream(yp, mods_p, tabs_p, None, h0_p, lw, tile=tile_p, keep=WINDOW,
                                    nseq_tile=None, seqs=bp, seq_len=sp)
        past = (cache_k[l].reshape(bs, WINDOW, kvw).astype(BF16), cache_v[l].reshape(bs, WINDOW, kvw).astype(BF16))
        ys, ksl, vsl, hsl = _stream(ys, mods_s, tabs_s, past, _state_in(state_ssm[l]), lw, tile=bs * ss,
                                    keep=bs * ss, nseq_tile=bs, seqs=bs, seq_len=ss)
        outs[0].append(kpl.reshape(bp, WINDOW, N_KV_HEADS, HEAD_DIM))
        outs[1].append(vpl.reshape(bp, WINDOW, N_KV_HEADS, HEAD_DIM))
        outs[2].append(_state_out(hpl, groups))
        outs[3].append(ksl.reshape(bs, ss, N_KV_HEADS, HEAD_DIM))
        outs[4].append(vsl.reshape(bs, ss, N_KV_HEADS, HEAD_DIM))
        outs[5].append(_state_out(hsl, groups))
    return (yp, ys.reshape(bs, ss, d)) + tuple(jnp.stack(o) for o in outs)
```

```python
import functools
import math

import jax
import jax.numpy as jnp
from jax import lax
from jax.experimental import pallas as pl
from jax.experimental.pallas import tpu as pltpu

F32 = jnp.float32
BF16 = jnp.bfloat16

CHUNK = 64
WINDOW = 128
HEAD_DIM = 64
N_HEADS = 8
N_KV_HEADS = 2
SSM_CH = 16
SSM_STATE = 64
SSM_L = 16
ROPE_THETA = 10000.0
EPS = 1e-6
PAST_LEN = 1024
LANES = 128
MXU_TILE = 256
KEYS_PAD = MXU_TILE
ROW_BLOCK = 256
SCAN_PITCH = 24
PREP_GROUPS = 4
INPROJ_TILE = 2048
MIX_INPUTS = 17
FFN_INPUTS = 7
LOG2E = math.log2(math.e)
VMEM_LIMIT = 56 * 1024 * 1024
VMEM_LIMIT_FUSED = 60 * 1024 * 1024


def _const_spec(shape):
    nd = len(shape)
    return pl.BlockSpec(shape, lambda *_: (0,) * nd, pipeline_mode=pl.Buffered(1))


def _params(n_grid):
    return pltpu.CompilerParams(dimension_semantics=("arbitrary",) * n_grid, vmem_limit_bytes=VMEM_LIMIT)


def _rms(x):
    return x * lax.rsqrt(jnp.mean(x * x, axis=-1, keepdims=True) + EPS)


def _dot(a, b):
    return jnp.dot(a, b, preferred_element_type=F32)


def _mod_kernel(c_ref, w_ref, b_ref, o_ref):
    c = c_ref[...]
    s = (c * jax.nn.sigmoid(c)).astype(BF16)
    o_ref[...] = _dot(s, w_ref[...].astype(BF16)) + b_ref[...]


def _mod_call(c, w, b):
    rows, d = c.shape
    cols = w.shape[1]
    tile = 1536
    return pl.pallas_call(
        _mod_kernel,
        grid=(cols // tile,),
        in_specs=[pl.BlockSpec((rows, d), lambda j: (0, 0)),
                  pl.BlockSpec((d, tile), lambda j: (0, j)),
                  pl.BlockSpec((1, tile), lambda j: (0, j))],
        out_specs=pl.BlockSpec((rows, tile), lambda j: (0, j)),
        out_shape=jax.ShapeDtypeStruct((rows, cols), F32),
        compiler_params=_params(1),
        name="mod",
    )(c, w, b.reshape(1, cols))


def _head_rms(t, ones_ref):
    width = t.shape[1]
    sq = t * t
    hi = sq.astype(BF16)
    lo = (sq - hi.astype(F32)).astype(BF16)
    parts = []
    for c0 in range(0, width, 256):
        w = min(256, width - c0)
        ones = ones_ref[:w, :w]
        parts.append(_dot(hi[:, c0:c0 + w], ones) + _dot(lo[:, c0:c0 + w], ones))
    ssq = parts[0] if len(parts) == 1 else jnp.concatenate(parts, axis=1)
    return t * lax.rsqrt(ssq * (1.0 / HEAD_DIM) + EPS)


def _rope(t, cos, s1, s2):
    outs = []
    for c0 in range(0, t.shape[1], LANES):
        xb = t[:, c0:c0 + LANES]
        outs.append(xb * cos + pltpu.roll(xb, LANES - HEAD_DIM // 2, 1) * s1 + pltpu.roll(xb, HEAD_DIM // 2, 1) * s2)
    return outs[0] if len(outs) == 1 else jnp.concatenate(outs, axis=1)


def _granule_transpose(arrs):
    gran = lax.broadcasted_iota(jnp.int32, arrs[0].shape, 1) // SSM_CH
    cur = list(arrs)
    for s in (4, 2, 1):
        upper = (gran & s) != 0
        nxt = list(cur)
        for a0 in range(8):
            if a0 & s:
                continue
            lo, hi = cur[a0], cur[a0 + s]
            nxt[a0] = jnp.where(upper, pltpu.roll(hi, SSM_CH * s, 1), lo)
            nxt[a0 + s] = jnp.where(upper, hi, pltpu.roll(lo, LANES - SSM_CH * s, 1))
        cur = nxt
    return cur


def _inproj_kernel(x_ref, sc_ref, sh_ref, ln_ref, w_ref, gq_ref, gk_ref, ones_ref, cos_ref, s1_ref, s2_ref,
                   q_ref, k_ref, v_ref, u_ref, klast_ref, vlast_ref, uscr, *, q_cols, kv_cols, keep):
    tile = x_ref.shape[1]
    block = min(tile, ROW_BLOCK)
    for r0 in range(0, tile, block):
        rows = slice(r0, r0 + block)
        sc = sc_ref[0] if sc_ref.shape[1] == 1 else sc_ref[0, rows]
        sh = sh_ref[0] if sh_ref.shape[1] == 1 else sh_ref[0, rows]
        h = _rms(x_ref[0, rows]) * (ln_ref[...] * (1.0 + sc)) + sh
        proj = _dot(h.astype(BF16), w_ref[...])
        q = proj[:, :q_cols]
        k = proj[:, q_cols:q_cols + kv_cols]
        v = proj[:, q_cols + kv_cols:q_cols + 2 * kv_cols]
        u = proj[:, q_cols + 2 * kv_cols:]
        kr = _rope(_head_rms(k, ones_ref) * gk_ref[...], cos_ref[rows], s1_ref[rows], s2_ref[rows])
        q_ref[0, rows] = (_head_rms(q, ones_ref) * gq_ref[...]).astype(BF16)
        k_ref[0, rows] = kr.astype(BF16)
        v_ref[0, rows] = v.astype(BF16)
        nchunk = block // SSM_L
        c0 = r0 // SSM_L
        for o in range(u.shape[1] // LANES):
            uscr[o, rows] = u[:, LANES * o:LANES * (o + 1)]
        for o in range(u.shape[1] // LANES):
            for hh in range(SSM_L // 8):
                outs = _granule_transpose([uscr[o, pl.ds(r0 + 8 * hh + t, nchunk, stride=SSM_L), :].astype(BF16)
                                           for t in range(8)])
                for g8 in range(8):
                    u_ref[8 * o + g8, c0:c0 + nchunk, LANES * hh:LANES * (hh + 1)] = outs[g8]
        first = max(r0, tile - keep)
        if first < r0 + block:
            dst = slice(first - (tile - keep), r0 + block - (tile - keep))
            klast_ref[0, dst] = kr[first - r0:, :]
            vlast_ref[0, dst] = v[first - r0:, :]


def _inproj_call(x, sc, sh, ln_g, w_in, gq, gk, ones, cos, s1, s2, *, tile, keep):
    nb, rows, d = x.shape
    in_cols = w_in.shape[1]
    q_cols = N_HEADS * HEAD_DIM
    kv_cols = N_KV_HEADS * HEAD_DIM
    u_cols = in_cols - q_cols - 2 * kv_cols
    mrows = sc.shape[1]
    mtile = 1 if mrows == 1 else tile
    mod_spec = pl.BlockSpec((1, mtile, d), (lambda b, i: (b, 0, 0)) if mrows == 1 else (lambda b, i: (b, i, 0)))
    row_spec = lambda c: pl.BlockSpec((1, tile, c), lambda b, i: (b, i, 0))
    tab_spec = pl.BlockSpec((tile, LANES), lambda b, i: (i, 0))
    last_spec = pl.BlockSpec((1, keep, kv_cols), lambda b, i: (b, 0, 0))
    groups = u_cols // SSM_CH
    tiles = rows // tile
    chunk_spec = pl.BlockSpec((groups, tile // SSM_L, SSM_L * SSM_CH), lambda b, i: (0, b * tiles + i, 0))
    kern = functools.partial(_inproj_kernel, q_cols=q_cols, kv_cols=kv_cols, keep=keep)
    return pl.pallas_call(
        kern,
        grid=(nb, tiles),
        in_specs=[row_spec(d), mod_spec, mod_spec, _const_spec((1, d)), _const_spec((d, in_cols)),
                  _const_spec((1, q_cols)), _const_spec((1, kv_cols)), _const_spec((256, 256)),
                  tab_spec, tab_spec, tab_spec],
        out_specs=[row_spec(q_cols), row_spec(kv_cols), row_spec(kv_cols), chunk_spec, last_spec, last_spec],
        out_shape=[jax.ShapeDtypeStruct((nb, rows, q_cols), BF16),
                   jax.ShapeDtypeStruct((nb, rows, kv_cols), BF16),
                   jax.ShapeDtypeStruct((nb, rows, kv_cols), BF16),
                   jax.ShapeDtypeStruct((groups, nb * rows // SSM_L, SSM_L * SSM_CH), BF16),
                   jax.ShapeDtypeStruct((nb, keep, kv_cols), F32),
                   jax.ShapeDtypeStruct((nb, keep, kv_cols), F32)],
        scratch_shapes=[pltpu.VMEM((u_cols // LANES, tile, LANES), F32)],
        compiler_params=_params(2),
        name="inproj",
    )(x, sc, sh, ln_g, w_in, gq, gk, ones, cos, s1, s2)


def _ssm_kernel(u_ref, h0_ref, wi_ref, krow_ref, cpow_ref, lam_ref, y_ref, hout_ref,
                toep, s_re, s_im, hp_re, hp_im, *, nseq, nchunk):
    groups = u_ref.shape[0]
    rows = nseq * nchunk
    npair = groups // 2

    @pl.when(pl.program_id(0) == 0)
    def _():
        lane = lax.broadcasted_iota(jnp.int32, (SSM_CH, SSM_L * SSM_CH), 1)

        def expand(g, carry):
            kr = krow_ref[g]
            for t in range(SSM_L):
                blk = kr if t == 0 else jnp.where(lane >= SSM_CH * t, pltpu.roll(kr, SSM_CH * t, 1), 0.0)
                toep[g, SSM_CH * t:SSM_CH * (t + 1), :] = blk.astype(BF16)
            return carry

        lax.fori_loop(0, groups, expand, 0)

    for p in range(npair):
        lhs = jnp.concatenate([u_ref[2 * p], u_ref[2 * p + 1]], axis=1)
        s = _dot(lhs, wi_ref[p])
        s_re[pl.ds(p, rows, stride=SCAN_PITCH), :] = s[:, :LANES]
        s_im[pl.ds(p, rows, stride=SCAN_PITCH), :] = s[:, LANES:]

    a_re = lam_ref[0]
    a_im = lam_ref[1]
    for b in range(nseq):
        def step(n, carry):
            h_re, h_im = carry
            r0 = pl.multiple_of((b * nchunk + n) * SCAN_PITCH, 8)
            hp_re[pl.ds(r0, npair), :] = h_re
            hp_im[pl.ds(r0, npair), :] = h_im
            n_re = a_re * h_re - a_im * h_im + s_re[pl.ds(r0, npair), :]
            n_im = a_re * h_im + a_im * h_re + s_im[pl.ds(r0, npair), :]
            return n_re, n_im
        h_re, h_im = lax.fori_loop(0, nchunk, step, (h0_ref[b, 0], h0_ref[b, 1]), unroll=True)
        hout_ref[b, 0] = h_re
        hout_ref[b, 1] = h_im

    for p in range(npair):
        hp = jnp.concatenate([hp_re[pl.ds(p, rows, stride=SCAN_PITCH), :],
                              hp_im[pl.ds(p, rows, stride=SCAN_PITCH), :]], axis=1).astype(BF16)
        for g in (2 * p, 2 * p + 1):
            y_ref[g] = (_dot(u_ref[g], toep[g]) + _dot(hp, cpow_ref[g])).astype(BF16)


def _ssm_call(u2, h0, wi, krow, cpow, lam, *, nseq, nchunk):
    groups, total_rows, width = u2.shape
    rows = nseq * nchunk
    npair = groups // 2
    kern = functools.partial(_ssm_kernel, nseq=nseq, nchunk=nchunk)
    return pl.pallas_call(
        kern,
        grid=(total_rows // rows,),
        in_specs=[pl.BlockSpec((groups, rows, width), lambda i: (0, i, 0)),
                  pl.BlockSpec((nseq, 2, npair, LANES), lambda i: (i, 0, 0, 0)),
                  _const_spec(wi.shape), _const_spec(krow.shape), _const_spec(cpow.shape), _const_spec(lam.shape)],
        out_specs=[pl.BlockSpec((groups, rows, width), lambda i: (0, i, 0)),
                   pl.BlockSpec((nseq, 2, npair, LANES), lambda i: (i, 0, 0, 0))],
        out_shape=[jax.ShapeDtypeStruct((groups, total_rows, width), BF16),
                   jax.ShapeDtypeStruct(h0.shape, F32)],
        scratch_shapes=[pltpu.VMEM((groups, width, width), BF16)] + [pltpu.VMEM((rows * SCAN_PITCH, LANES), F32)] * 4,
        compiler_params=_params(1),
        name="ssm",
    )(u2, h0, wi, krow, cpow, lam)


def _kv_dup(a):
    lo = lax.broadcasted_iota(jnp.int32, a.shape, 1) < HEAD_DIM
    sw = pltpu.roll(a, HEAD_DIM, 1)
    return jnp.where(lo, a, sw), jnp.where(lo, sw, a)


def _v_ext(v):
    return jnp.concatenate([v, jnp.ones_like(v)], axis=1).astype(BF16)


def _sink_tail(rows):
    row = lax.broadcasted_iota(jnp.int32, (rows, 2 * LANES), 0)
    lane = lax.broadcasted_iota(jnp.int32, (rows, 2 * LANES), 1)
    return jnp.where(jnp.logical_and(row == 0, lane >= LANES), 1.0, 0.0).astype(BF16)


def _sink_attention(sink_ref, problems, nkeys):
    rep = N_HEADS // N_KV_HEADS
    col = lax.broadcasted_iota(jnp.int32, (1, KEYS_PAD), 1)
    real = col < nkeys
    scores = []
    for g, qa, qb, kb, vb, valid, store in problems:
        half = qa.shape[0]
        qst = jnp.concatenate([qa, qb], axis=0)
        lo = lax.broadcasted_iota(jnp.int32, qst.shape, 1) < HEAD_DIM
        zero = jnp.zeros_like(qst)
        q4 = jnp.concatenate([jnp.where(lo, qst, zero), jnp.where(lo, zero, qst)], axis=0)
        s = lax.dot_general(q4, kb, (((1,), (1,)), ((), ())), preferred_element_type=F32)
        keep = real if valid is None else jnp.logical_and(real, valid)
        first = 0 if valid is not None else (nkeys // LANES) * LANES
        blocks = []
        for e in range(2):
            for jj in range(2):
                fill = jnp.where(col == nkeys, sink_ref[rep * g + 2 * jj + e] * LOG2E, -jnp.inf)
                sb = s[(2 * e + jj) * half:(2 * e + jj + 1) * half]
                fixed = jnp.where(keep[:, first:], sb[:, first:], fill[:, first:])
                blocks.append(fixed if first == 0 else jnp.concatenate([sb[:, :first], fixed], axis=1))
        scores.append(jnp.concatenate(blocks, axis=0))
    probs = []
    for s in scores:
        probs.append(jnp.exp2(s - jnp.max(s, axis=-1, keepdims=True)).astype(BF16))
    for p, (g, qa, qb, kb, vb, valid, store) in zip(probs, problems):
        o4 = _dot(p, vb)
        h2 = 2 * qa.shape[0]
        lo = lax.broadcasted_iota(jnp.int32, (h2, LANES), 1) < HEAD_DIM
        store(jnp.where(lo, o4[:h2, :LANES], o4[h2:, :LANES]) / jnp.where(lo, o4[:h2, LANES:], o4[h2:, LANES:]))


def _rotated_q(q_ref, tabs):
    cos_ref, s1_ref, s2_ref = tabs
    return _rope(q_ref[0].astype(F32), cos_ref[...], s1_ref[...], s2_ref[...]).astype(BF16)


def _attn_prompt(q_ref, tabs, kc_ref, kp_ref, vc_ref, vp_ref, sink_ref, kz, vz, attn, tile, later_tile):
    band = CHUNK + WINDOW
    kd = _kv_dup(jnp.concatenate([kp_ref[0], kc_ref[0]], axis=0).astype(F32))
    vd = _kv_dup(jnp.concatenate([vp_ref[0], vc_ref[0]], axis=0).astype(F32))
    tail = _sink_tail(KEYS_PAD - band)
    for g in range(2):
        kz[g, :tile + WINDOW] = kd[g].astype(BF16)
        kz[g, tile + WINDOW:] = jnp.zeros((KEYS_PAD - band, LANES), BF16)
        vext = _v_ext(vd[g])
        for c in range(tile // CHUNK):
            vz[g, c, :band] = vext[c * CHUNK:c * CHUNK + band]
            vz[g, c, band:] = tail
    col_chunk = lax.broadcasted_iota(jnp.int32, (1, KEYS_PAD), 1) // CHUNK
    q = _rotated_q(q_ref, tabs)
    problems = []
    for c in range(tile // CHUNK):
        r0 = c * CHUNK
        valid = None if c >= WINDOW // CHUNK else jnp.logical_or(col_chunk + c >= WINDOW // CHUNK, later_tile)
        for g in range(2):
            c0 = 2 * LANES * g

            def store(o, r0=r0, c0=c0):
                attn[r0:r0 + CHUNK, c0:c0 + LANES] = o[:CHUNK]
                attn[r0:r0 + CHUNK, c0 + LANES:c0 + 2 * LANES] = o[CHUNK:]

            problems.append((g, q[r0:r0 + CHUNK, c0:c0 + LANES], q[r0:r0 + CHUNK, c0 + LANES:c0 + 2 * LANES],
                             kz[g, r0:r0 + KEYS_PAD, :], vz[g, c], valid, store))
    _sink_attention(sink_ref, problems, band)


def _attn_sample(q_ref, tabs, kc_ref, kp_ref, vc_ref, vp_ref, sink_ref, attn, nseq, seq):
    nkeys = WINDOW + seq
    tail = _sink_tail(KEYS_PAD - nkeys)
    q = _rotated_q(q_ref, tabs)
    problems = []
    for b in range(nseq):
        r0 = b * seq
        kd = _kv_dup(jnp.concatenate([kp_ref[b], kc_ref[0, r0:r0 + seq, :]], axis=0).astype(F32))
        vd = _kv_dup(jnp.concatenate([vp_ref[b], vc_ref[0, r0:r0 + seq, :]], axis=0).astype(F32))
        for g in range(2):
            c0 = 2 * LANES * g

            def store(o, r0=r0, c0=c0):
                attn[r0:r0 + seq, c0:c0 + LANES] = o[:seq]
                attn[r0:r0 + seq, c0 + LANES:c0 + 2 * LANES] = o[seq:]

            kb = jnp.concatenate([kd[g].astype(BF16), jnp.zeros((KEYS_PAD - nkeys, LANES), BF16)], axis=0)
            vb = jnp.concatenate([_v_ext(vd[g]), tail], axis=0)
            problems.append((g, q[r0:r0 + seq, c0:c0 + LANES], q[r0:r0 + seq, c0 + LANES:c0 + 2 * LANES],
                             kb, vb, None, store))
    _sink_attention(sink_ref, problems, nkeys)


def _gelu_tanh(x):
    return 0.5 * x * (1.0 + jnp.tanh(math.sqrt(2.0 / math.pi) * (x + 0.044715 * (x * x * x))))


def _mix_body(sink_ref, x_ref, g1_ref, q_ref, kc_ref, kp_ref, vc_ref, vp_ref, y_ref, gluw_ref, glub_ref, ga_ref,
              gs_ref, wout_ref, cos_ref, s1_ref, s2_ref, out, scratch, *, tile, nseq, later_tile=None):
    tabs = (cos_ref, s1_ref, s2_ref)
    if nseq is None:
        yscr, kz, vz, attn = scratch
        _attn_prompt(q_ref, tabs, kc_ref, kp_ref, vc_ref, vp_ref, sink_ref, kz, vz, attn, tile, later_tile)
    else:
        yscr, attn = scratch
        _attn_sample(q_ref, tabs, kc_ref, kp_ref, vc_ref, vp_ref, sink_ref, attn, nseq, tile // nseq)
    nslab = yscr.shape[0]
    for o in range(nslab):
        for hh in range(SSM_L // 8):
            outs = _granule_transpose([y_ref[8 * o + g8, :, LANES * hh:LANES * (hh + 1)] for g8 in range(8)])
            for t8 in range(8):
                yscr[o, pl.ds(8 * hh + t8, tile // SSM_L, stride=SSM_L), :] = outs[t8].astype(F32)
    block = min(tile, ROW_BLOCK)
    for r0 in range(0, tile, block):
        rows = slice(r0, r0 + block)
        an = _rms(attn[rows]) * ga_ref[...]
        gl = _gelu_tanh(jnp.concatenate([yscr[o, rows] for o in range(nslab)], axis=1))
        so = gl * jax.nn.sigmoid(_dot(gl.astype(BF16), gluw_ref[...]) + glub_ref[...])
        sn = _rms(so) * gs_ref[...]
        merged = jnp.concatenate([an, sn], axis=1).astype(BF16)
        g1 = g1_ref[0] if g1_ref.shape[1] == 1 else g1_ref[0, rows]
        out[rows] = x_ref[0, rows] + g1 * _dot(merged, wout_ref[...])


def _mix_kernel(*refs, tile, nseq):
    _mix_body(*refs[:MIX_INPUTS], refs[MIX_INPUTS].at[0], refs[MIX_INPUTS + 1:], tile=tile, nseq=nseq)


def _mix_call(sinks, x, g1, q, k, v, k_past, v_past, y, gluw, glub, ga, gs, wout, tabs, *, tile, nseq):
    nb, rows, d = x.shape
    aw = q.shape[2]
    kvw = k.shape[2]
    groups, _, cw = y.shape
    tiles = rows // tile
    row_spec = lambda c: pl.BlockSpec((1, tile, c), lambda b, i: (b, i, 0))
    chunk_spec = pl.BlockSpec((groups, tile // SSM_L, cw), lambda b, i: (0, b * tiles + i, 0))
    past_spec = pl.BlockSpec((nseq, WINDOW, kvw), lambda b, i: (b * tiles + i, 0, 0))
    tab_spec = pl.BlockSpec((tile, LANES), lambda b, i: (i, 0))
    scratch = [pltpu.VMEM((groups * SSM_CH // LANES, tile, LANES), F32), pltpu.VMEM((tile, aw), F32)]
    kern = functools.partial(_mix_kernel, tile=tile, nseq=nseq)
    return pl.pallas_call(
        kern,
        grid=(nb, tiles),
        in_specs=[pl.BlockSpec(memory_space=pltpu.SMEM),
                  row_spec(d), row_spec(d), row_spec(aw), row_spec(kvw), past_spec, row_spec(kvw), past_spec,
                  chunk_spec, _const_spec(gluw.shape), _const_spec(glub.shape), _const_spec(ga.shape),
                  _const_spec(gs.shape), _const_spec(wout.shape), tab_spec, tab_spec, tab_spec],
        out_specs=row_spec(d),
        out_shape=jax.ShapeDtypeStruct((nb, rows, d), F32),
        scratch_shapes=scratch,
        compiler_params=_params(2),
        name="mix",
    )(sinks, x, g1, q, k, k_past, v, v_past, y, gluw, glub, ga, gs, wout, *tabs)


def _ffn_body(x, sc_ref, sh_ref, g2_ref, ln_ref, wg_ref, wu_ref, wd_ref, out, ff_bounds):
    h = (_rms(x[...]) * (ln_ref[...] * (1.0 + sc_ref[0])) + sh_ref[0]).astype(BF16)
    acc = None
    for c0, c1 in zip(ff_bounds[:-1], ff_bounds[1:]):
        a = _dot(h, wg_ref[:, c0:c1])
        b = _dot(h, wu_ref[:, c0:c1])
        part = _dot((a * jax.nn.sigmoid(a) * b).astype(BF16), wd_ref[c0:c1, :])
        acc = part if acc is None else acc + part
    out[...] = x[...] + g2_ref[0] * acc


def _ffn_kernel(x_ref, sc_ref, sh_ref, g2_ref, ln_ref, wg_ref, wu_ref, wd_ref, o_ref, *, ff_bounds):
    _ffn_body(x_ref.at[0], sc_ref, sh_ref, g2_ref, ln_ref, wg_ref, wu_ref, wd_ref, o_ref.at[0], ff_bounds)


def _ff_bounds(dff):
    split = -(-dff // (2 * MXU_TILE)) * MXU_TILE
    return (0, split, dff) if split < dff else (0, dff)


def _mixffn_kernel(*refs, tile, tiles, ff_bounds):
    mix_in, ffn_in = refs[:MIX_INPUTS], refs[MIX_INPUTS:MIX_INPUTS + FFN_INPUTS]
    o_ref, x1_buf, *mix_scratch = refs[MIX_INPUTS + FFN_INPUTS:]
    step = pl.program_id(0)

    @pl.when(step == 0)
    def _():
        x1_buf[...] = jnp.zeros(x1_buf.shape, F32)

    _ffn_body(x1_buf, *ffn_in, o_ref.at[0], ff_bounds)
    tile_in_seq = jnp.minimum(step, pl.num_programs(0) - 2) % tiles
    _mix_body(*mix_in, x1_buf, mix_scratch, tile=tile, nseq=None, later_tile=tile_in_seq > 0)


def _mixffn_call(sinks, x, g1, q, k, v, y, gluw, glub, ga, gs, wout, tabs, sc, sh, g2, ln_g, wg, wu, wd, *, tile):
    nb, rows, d = x.shape
    aw = q.shape[2]
    kvw = k.shape[2]
    groups, _, cw = y.shape
    tiles = rows // tile
    last = nb * tiles - 1
    wpt = tile // WINDOW

    def mix_bi(s):
        sm = jnp.minimum(s, last)
        return sm // tiles, sm % tiles

    def ffn_bi(s):
        sf = jnp.maximum(s - 1, 0)
        return sf // tiles, sf % tiles

    row_spec = lambda c: pl.BlockSpec((1, tile, c), lambda s: (*mix_bi(s), 0))
    mod_spec = pl.BlockSpec((1, 1, d), lambda s: (mix_bi(s)[0], 0, 0))
    past_spec = pl.BlockSpec((1, WINDOW, kvw),
                             lambda s: (mix_bi(s)[0], jnp.maximum(mix_bi(s)[1] * wpt - 1, 0), 0))
    chunk_spec = pl.BlockSpec((groups, tile // SSM_L, cw), lambda s: (0, jnp.minimum(s, last), 0))
    tab_spec = pl.BlockSpec((tile, LANES), lambda s: (mix_bi(s)[1], 0))
    ffn_mod = pl.BlockSpec((1, 1, d), lambda s: (ffn_bi(s)[0], 0, 0))
    scratch = [pltpu.VMEM((tile, d), F32),
               pltpu.VMEM((groups * SSM_CH // LANES, tile, LANES), F32),
               pltpu.VMEM((2, tile + KEYS_PAD - CHUNK, LANES), BF16),
               pltpu.VMEM((2, tile // CHUNK, KEYS_PAD, 2 * LANES), BF16), pltpu.VMEM((tile, aw), F32)]
    kern = functools.partial(_mixffn_kernel, tile=tile, tiles=tiles, ff_bounds=_ff_bounds(wg.shape[1]))
    return pl.pallas_call(
        kern,
        grid=(nb * tiles + 1,),
        in_specs=[pl.BlockSpec(memory_space=pltpu.SMEM),
                  row_spec(d), mod_spec, row_spec(aw), row_spec(kvw), past_spec, row_spec(kvw), past_spec,
                  chunk_spec, _const_spec(gluw.shape), _const_spec(glub.shape), _const_spec(ga.shape),
                  _const_spec(gs.shape), _const_spec(wout.shape), tab_spec, tab_spec, tab_spec,
                  ffn_mod, ffn_mod, ffn_mod, _const_spec((1, d)),
                  _const_spec(wg.shape), _const_spec(wu.shape), _const_spec(wd.shape)],
        out_specs=pl.BlockSpec((1, tile, d), lambda s: (*ffn_bi(s), 0)),
        out_shape=jax.ShapeDtypeStruct((nb, rows, d), F32),
        scratch_shapes=scratch,
        compiler_params=pltpu.CompilerParams(dimension_semantics=("arbitrary",), vmem_limit_bytes=VMEM_LIMIT_FUSED),
        name="mixffn",
    )(sinks, x, g1, q, k, k, v, v, y, gluw, glub, ga, gs, wout, *tabs, sc, sh, g2, ln_g, wg, wu, wd)


def _ffn_call(x, sc, sh, g2, ln_g, wg, wu, wd, *, tile):
    nb, rows, d = x.shape
    ff_bounds = _ff_bounds(wg.shape[1])
    mrows = sc.shape[1]
    mtile = 1 if mrows == 1 else tile
    mod_spec = pl.BlockSpec((1, mtile, d), (lambda b, i: (b, 0, 0)) if mrows == 1 else (lambda b, i: (b, i, 0)))
    row_spec = pl.BlockSpec((1, tile, d), lambda b, i: (b, i, 0))
    kern = functools.partial(_ffn_kernel, ff_bounds=ff_bounds)
    return pl.pallas_call(
        kern,
        grid=(nb, rows // tile),
        in_specs=[row_spec, mod_spec, mod_spec, mod_spec, _const_spec((1, d)),
                  _const_spec(wg.shape), _const_spec(wu.shape), _const_spec(wd.shape)],
        out_specs=row_spec,
        out_shape=jax.ShapeDtypeStruct((nb, rows, d), F32),
        compiler_params=_params(2),
        name="ffn",
    )(x, sc, sh, g2, ln_g, wg, wu, wd)


def _split_bf16(a):
    hi = a.astype(BF16)
    return hi, (a - hi.astype(F32)).astype(BF16)


def _dot3(a, b):
    ah, al = _split_bf16(a)
    bh, bl = _split_bf16(b)
    return _dot(ah, bh) + _dot(ah, bl) + _dot(al, bh)


def _ssm_prep_kernel(*refs):
    for gi in range(refs[0].shape[0]):
        _ssm_prep_group(*[r.at[pl.ds(gi, 1)] for r in refs])


def _ssm_prep_group(logdt_ref, ar_ref, ai_ref, bt_re_ref, bt_im_ref, ct_re_ref, ct_im_ref, d_ref,
                     krow_ref, y_re_ref, y_im_ref, wi_re_ref, wi_im_ref, lam_ref):
    n, ch = SSM_L, SSM_CH
    dt = jnp.exp(logdt_ref[0])
    ar, ai = ar_ref[0], ai_ref[0]
    zr, zi = ar * dt, ai * dt
    ez = jnp.exp(zr)
    e_re = jnp.tanh(0.5 * zr) * (ez + 1.0) * jnp.cos(zi) - 2.0 * jnp.sin(0.5 * zi) ** 2
    e_im = ez * jnp.sin(zi)
    mag = ar * ar + ai * ai
    coef_re = (e_re * ar + e_im * ai) / mag
    coef_im = (e_im * ar - e_re * ai) / mag
    bt_re, bt_im = bt_re_ref[0], bt_im_ref[0]
    bb_re = coef_re * bt_re - coef_im * bt_im
    bb_im = coef_re * bt_im + coef_im * bt_re
    rows = 2 * n
    jc = lax.broadcasted_iota(jnp.int32, (rows, 1), 0).astype(F32)
    mg = jnp.exp(jc * zr)
    pr, pi = mg * jnp.cos(jc * zi), mg * jnp.sin(jc * zi)
    for t in range(n):
        r_re, r_im = pr[n - 1 - t:n - t], pi[n - 1 - t:n - t]
        wi_re_ref[0, ch * t:ch * (t + 1), :] = bb_re * r_re - bb_im * r_im
        wi_im_ref[0, ch * t:ch * (t + 1), :] = bb_re * r_im + bb_im * r_re
    lam_ref[0, 0:1, :] = pr[n:n + 1]
    lam_ref[0, 1:2, :] = pi[n:n + 1]
    lag = lax.broadcasted_iota(jnp.int32, (rows, n * ch), 1) // ch
    jrow = lax.broadcasted_iota(jnp.int32, (rows, n * ch), 0)
    chan = lax.broadcasted_iota(jnp.int32, (ch, n * ch), 1) % ch
    spread_chan = (chan == lax.broadcasted_iota(jnp.int32, (ch, n * ch), 0)).astype(BF16)
    nn, tn = (((1,), (0,)), ((), ())), (((0,), (0,)), ((), ()))

    def spread(a, onehot, dims):
        hi, lo = _split_bf16(a)
        return (lax.dot_general(hi, onehot, dims, preferred_element_type=F32)
                + lax.dot_general(lo, onehot, dims, preferred_element_type=F32))

    def powers(shift):
        onehot = (lag + shift == jrow).astype(BF16)
        return spread(pr, onehot, tn), spread(pi, onehot, tn)

    c_re = spread(ct_re_ref[0], spread_chan, nn)
    c_im = spread(ct_im_ref[0], spread_chan, nn)
    l1_re, l1_im = powers(1)
    y_re_ref[0] = c_re * l1_re - c_im * l1_im
    y_im_ref[0] = c_re * l1_im + c_im * l1_re
    l0_re, l0_im = powers(0)
    k_re = c_re * l0_re - c_im * l0_im
    k_im = c_re * l0_im + c_im * l0_re
    lane = lax.broadcasted_iota(jnp.int32, (ch, n * ch), 1)
    d_lag0 = jnp.where(lane == lax.broadcasted_iota(jnp.int32, (ch, n * ch), 0), d_ref[0], 0.0)
    krow_ref[0] = _dot3(bb_re, k_re) - _dot3(bb_im, k_im) + d_lag0


def _ssm_weights(a_re, a_im, log_dt, b_re, b_im, c_re, c_im, d_skip):
    groups, state = a_re.shape
    ch = b_re.shape[2]
    n = SSM_L
    row = lambda a: a.reshape(groups, 1, state)
    tr =lambda a: jnp.transpose(a, (0, 2, 1))
    d_pad = jnp.pad(d_skip.reshape(groups, 1, ch), ((0, 0), (0, 0), (0, (n - 1) * ch)))
    per_step = math.gcd(groups, PREP_GROUPS)
    blk = lambda *shape: pl.BlockSpec((per_step,) + shape, lambda g: (g, 0, 0))
    krow, y_re, y_im, wi_re, wi_im, lam = pl.pallas_call(
        _ssm_prep_kernel,
        grid=(groups // per_step,),
        in_specs=[blk(1, 1), blk(1, state), blk(1, state),
                  blk(ch, state), blk(ch, state), blk(state, ch), blk(state, ch), blk(1, n * ch)],
        out_specs=[blk(ch, n * ch), blk(state, n * ch), blk(state, n * ch), blk(n * ch, state),
                   blk(n * ch, state), blk(2, state)],
        out_shape=[jax.ShapeDtypeStruct((groups, ch, n * ch), F32),
                   jax.ShapeDtypeStruct((groups, state, n * ch), F32),
                   jax.ShapeDtypeStruct((groups, state, n * ch), F32),
                   jax.ShapeDtypeStruct((groups, n * ch, state), F32),
                   jax.ShapeDtypeStruct((groups, n * ch, state), F32),
                   jax.ShapeDtypeStruct((groups, 2, state), F32)],
        compiler_params=_params(1),
        name="ssm_prep",
    )(log_dt.reshape(groups, 1, 1), row(a_re), row(a_im), tr(b_re), tr(b_im), tr(c_re), tr(c_im), d_pad)
    pair = lambda a: a.reshape((groups // 2, 2) + a.shape[1:])
    wr, wm = pair(wi_re), pair(wi_im)
    zero = jnp.zeros_like(wr[:, 0])
    top = jnp.concatenate([wr[:, 0], zero, wm[:, 0], zero], axis=2)
    bot = jnp.concatenate([zero, wr[:, 1], zero, wm[:, 1]], axis=2)
    wi_pair = jnp.concatenate([top, bot], axis=1).astype(BF16)
    yr, ym = pair(y_re), pair(y_im)
    zc = jnp.zeros_like(yr[:, 0])
    even = jnp.concatenate([yr[:, 0], zc, -ym[:, 0], zc], axis=1)
    odd = jnp.concatenate([zc, yr[:, 1], zc, -ym[:, 1]], axis=1)
    cpow = jnp.stack([even, odd], axis=1).reshape(groups, 4 * state, n * ch).astype(BF16)
    lam16 = jnp.transpose(lam, (1, 0, 2)).reshape(2, groups // 2, 2 * state)
    return wi_pair, krow, cpow, lam16


def _rope_tables(pos):
    half = HEAD_DIM // 2
    inv = ROPE_THETA ** (-jnp.arange(half, dtype=F32) * 2.0 / HEAD_DIM)
    lane = jnp.arange(LANES)
    ang = pos.astype(F32)[:, None] * inv[lane % half][None, :]
    first = ((lane % HEAD_DIM) < half)[None, :]
    sin = jnp.sin(ang)
    return jnp.cos(ang), jnp.where(first, -sin, 0.0), jnp.where(first, 0.0, sin)


def _state_in(state):
    b, groups, p, _ = state.shape
    return jnp.transpose(state, (0, 3, 1, 2)).reshape(b, 2, groups // 2, 2 * p)


def _state_out(h, groups):
    b = h.shape[0]
    return jnp.transpose(h.reshape(b, 2, groups, -1), (0, 2, 3, 1))


def _stream(x, mods, tabs, past, h0, lw, *, tile, keep, nseq_tile, seqs, seq_len):
    sh1, sc1, g1, sh2, sc2, g2 = mods
    nb, rows, d = x.shape
    groups = lw['groups']
    q, k, v, u2, k_last, v_last = _inproj_call(x, sc1, sh1, lw['ln1'], lw['w_in'], lw['gq'], lw['gk'], lw['ones'],
                                               *tabs, tile=min(rows, INPROJ_TILE), keep=keep)
    y2, h_last = _ssm_call(u2, h0, lw['wi'], lw['krow'], lw['cpow'], lw['lam'],
                           nseq=(1 if nseq_tile is None else nseq_tile), nchunk=seq_len // SSM_L)
    if past is None:
        out = _mixffn_call(lw['sinks'], x, g1, q, k, v, y2, lw['gluw'], lw['glub'], lw['ga'], lw['gs'], lw['w_out'],
                           tabs, sc2, sh2, g2, lw['ln2'], lw['wg'], lw['wu'], lw['wd'], tile=tile)
    else:
        x1 = _mix_call(lw['sinks'], x, g1, q, k, v, *past, y2, lw['gluw'], lw['glub'], lw['ga'], lw['gs'],
                       lw['w_out'], tabs, tile=tile, nseq=nseq_tile)
        out = _ffn_call(x1, sc2, sh2, g2, lw['ln2'], lw['wg'], lw['wu'], lw['wd'], tile=tile)
    return out, k_last, v_last, h_last


def kernel(x_prompt, x_sample, cache_k, cache_v, state_ssm, c_prompt, c_sample, w_ada, b_ada, ln1_g, w_in, q_norm_g, k_norm_g, attn_sinks, ssm_A_re, ssm_A_im, ssm_log_dt, ssm_B_re, ssm_B_im, ssm_C_re, ssm_C_im, ssm_D, ssm_glu_w, ssm_glu_b, attn_out_g, ssm_out_g, w_out, ln2_g, w_gate, w_up, w_down):
    depth = w_ada.shape[0]
    bp, sp, d = x_prompt.shape
    bs, ss, _ = x_sample.shape
    groups = ssm_A_re.shape[1]
    kvw = N_KV_HEADS * HEAD_DIM
    tile_p = min(512, sp)
    seg = jnp.arange(256) // HEAD_DIM
    ones = (seg[:, None] == seg[None, :]).astype(BF16)
    tabs_p = _rope_tables(jnp.arange(sp))
    tabs_s = tuple(jnp.tile(t, (bs, 1)) for t in _rope_tables(PAST_LEN + jnp.arange(ss)))

    yp = x_prompt
    ys = x_sample.reshape(1, bs * ss, d)
    outs = [[] for _ in range(6)]
    for l in range(depth):
        wi, krow, cpow, lam = _ssm_weights(ssm_A_re[l], ssm_A_im[l], ssm_log_dt[l], ssm_B_re[l], ssm_B_im[l],
                                           ssm_C_re[l], ssm_C_im[l], ssm_D[l])
        lw = dict(groups=groups, ln1=ln1_g[l][None], w_in=w_in[l].astype(BF16),
                  gq=jnp.tile(q_norm_g[l], N_HEADS)[None] * (HEAD_DIM ** -0.5 * LOG2E), gk=jnp.tile(k_norm_g[l], N_KV_HEADS)[None], ones=ones,
                  wi=wi, krow=krow, cpow=cpow, lam=lam, sinks=attn_sinks[l],
                  gluw=ssm_glu_w[l].astype(BF16), glub=ssm_glu_b[l][None], ga=attn_out_g[l][None],
                  gs=ssm_out_g[l][None], w_out=w_out[l].astype(BF16), ln2=ln2_g[l][None],
                  wg=w_gate[l].astype(BF16), wu=w_up[l].astype(BF16), wd=w_down[l].astype(BF16))
        mod = _mod_call(jnp.concatenate([c_prompt, c_sample], axis=0), w_ada[l], b_ada[l])
        mods_p = tuple(m[:, None, :] for m in jnp.split(mod[:bp], 6, axis=-1))
        mods_s = tuple(jnp.repeat(m, ss, axis=0)[None] for m in jnp.split(mod[bp:], 6, axis=-1))

        h0_p = jnp.zeros((bp, 2, groups // 2, 2 * SSM_STATE), F32)
        yp, kpl, vpl, hpl = _stream(yp, mods_p, tabs_p, None, h0_p, lw, tile=tile_p, keep=WINDOW,
                                    nseq_tile=None, seqs=bp, seq_len=sp)
        past = (cache_k[l].reshape(bs, WINDOW, kvw).astype(BF16), cache_v[l].reshape(bs, WINDOW, kvw).astype(BF16))
        ys, ksl, vsl, hsl = _stream(ys, mods_s, tabs_s, past, _state_in(state_ssm[l]), lw, tile=bs * ss,
                                    keep=bs * ss, nseq_tile=bs, seqs=bs, seq_len=ss)
        outs[0].append(kpl.reshape(bp, WINDOW, N_KV_HEADS, HEAD_DIM))
        outs[1].append(vpl.reshape(bp, WINDOW, N_KV_HEADS, HEAD_DIM))
        outs[2].append(_state_out(hpl, groups))
        outs[3].append(ksl.reshape(bs, ss, N_KV_HEADS, HEAD_DIM))
        outs[4].append(vsl.reshape(bs, ss, N_KV_HEADS, HEAD_DIM))
        outs[5].append(_state_out(hsl, groups))
    return (yp, ys.reshape(bs, ss, d)) + tuple(jnp.stack(o) for o in outs)
```

```python
import functools
import math

import jax
import jax.numpy as jnp
from jax import lax
from jax.experimental import pallas as pl
from jax.experimental.pallas import tpu as pltpu

F32 = jnp.float32
BF16 = jnp.bfloat16

CHUNK = 64
WINDOW = 128
HEAD_DIM = 64
N_HEADS = 8
N_KV_HEADS = 2
SSM_CH = 16
SSM_STATE = 64
SSM_L = 16
ROPE_THETA = 10000.0
EPS = 1e-6
PAST_LEN = 1024
LANES = 128
MXU_TILE = 256
KEYS_PAD = MXU_TILE
ROW_BLOCK = 256
SCAN_PITCH = 24
PREP_GROUPS = 4
INPROJ_TILE = 2048
MIX_INPUTS = 17
FFN_INPUTS = 7
LOG2E = math.log2(math.e)
VMEM_LIMIT = 56 * 1024 * 1024
VMEM_LIMIT_FUSED = 60 * 1024 * 1024


def _const_spec(shape):
    nd = len(shape)
    return pl.BlockSpec(shape, lambda *_: (0,) * nd, pipeline_mode=pl.Buffered(1))


def _params(n_grid):
    return pltpu.CompilerParams(dimension_semantics=("arbitrary",) * n_grid, vmem_limit_bytes=VMEM_LIMIT)


def _rms(x):
    return x * lax.rsqrt(jnp.mean(x * x, axis=-1, keepdims=True) + EPS)


def _dot(a, b):
    return jnp.dot(a, b, preferred_element_type=F32)


def _mod_kernel(c_ref, w_ref, b_ref, o_ref):
    c = c_ref[...]
    s = (c * jax.nn.sigmoid(c)).astype(BF16)
    o_ref[...] = _dot(s, w_ref[...].astype(BF16)) + b_ref[...]


def _mod_call(c, w, b):
    rows, d = c.shape
    cols = w.shape[1]
    tile = 1536
    return pl.pallas_call(
        _mod_kernel,
        grid=(cols // tile,),
        in_specs=[pl.BlockSpec((rows, d), lambda j: (0, 0)),
                  pl.BlockSpec((d, tile), lambda j: (0, j)),
                  pl.BlockSpec((1, tile), lambda j: (0, j))],
        out_specs=pl.BlockSpec((rows, tile), lambda j: (0, j)),
        out_shape=jax.ShapeDtypeStruct((rows, cols), F32),
        compiler_params=_params(1),
        name="mod",
    )(c, w, b.reshape(1, cols))


def _head_rms(t, ones_ref, split):
    width = t.shape[1]
    sq = t * t
    hi = sq.astype(BF16)
    lo = (sq - hi.astype(F32)).astype(BF16) if split else None
    parts = []
    for c0 in range(0, width, 256):
        w = min(256, width - c0)
        ones = ones_ref[:w, :w]
        part = _dot(hi[:, c0:c0 + w], ones)
        parts.append(part + _dot(lo[:, c0:c0 + w], ones) if split else part)
    ssq = parts[0] if len(parts) == 1 else jnp.concatenate(parts, axis=1)
    return t * lax.rsqrt(ssq * (1.0 / HEAD_DIM) + EPS)


def _rope(t, cos, s1, s2):
    outs = []
    for c0 in range(0, t.shape[1], LANES):
        xb = t[:, c0:c0 + LANES]
        outs.append(xb * cos + pltpu.roll(xb, LANES - HEAD_DIM // 2, 1) * s1 + pltpu.roll(xb, HEAD_DIM // 2, 1) * s2)
    return outs[0] if len(outs) == 1 else jnp.concatenate(outs, axis=1)


def _granule_transpose(arrs):
    gran = lax.broadcasted_iota(jnp.int32, arrs[0].shape, 1) // SSM_CH
    cur = list(arrs)
    for s in (4, 2, 1):
        upper = (gran & s) != 0
        nxt = list(cur)
        for a0 in range(8):
            if a0 & s:
                continue
            lo, hi = cur[a0], cur[a0 + s]
            nxt[a0] = jnp.where(upper, pltpu.roll(hi, SSM_CH * s, 1), lo)
            nxt[a0 + s] = jnp.where(upper, hi, pltpu.roll(lo, LANES - SSM_CH * s, 1))
        cur = nxt
    return cur


def _inproj_kernel(x_ref, sc_ref, sh_ref, ln_ref, w_ref, gq_ref, gk_ref, ones_ref, cos_ref, s1_ref, s2_ref,
                   q_ref, k_ref, v_ref, u_ref, klast_ref, vlast_ref, uscr, *, q_cols, kv_cols, keep):
    tile = x_ref.shape[1]
    block = min(tile, ROW_BLOCK)
    for r0 in range(0, tile, block):
        rows = slice(r0, r0 + block)
        sc = sc_ref[0] if sc_ref.shape[1] == 1 else sc_ref[0, rows]
        sh = sh_ref[0] if sh_ref.shape[1] == 1 else sh_ref[0, rows]
        h = _rms(x_ref[0, rows]) * (ln_ref[...] * (1.0 + sc)) + sh
        proj = _dot(h.astype(BF16), w_ref[...])
        q = proj[:, :q_cols]
        k = proj[:, q_cols:q_cols + kv_cols]
        v = proj[:, q_cols + kv_cols:q_cols + 2 * kv_cols]
        u = proj[:, q_cols + 2 * kv_cols:]
        kr = _rope(_head_rms(k, ones_ref, True) * gk_ref[...], cos_ref[rows], s1_ref[rows], s2_ref[rows])
        q_ref[0, rows] = (_head_rms(q, ones_ref, False) * gq_ref[...]).astype(BF16)
        k_ref[0, rows] = kr.astype(BF16)
        v_ref[0, rows] = v.astype(BF16)
        nchunk = block // SSM_L
        c0 = r0 // SSM_L
        for o in range(u.shape[1] // LANES):
            uscr[o, rows] = u[:, LANES * o:LANES * (o + 1)]
        for o in range(u.shape[1] // LANES):
            for hh in range(SSM_L // 8):
                outs = _granule_transpose([uscr[o, pl.ds(r0 + 8 * hh + t, nchunk, stride=SSM_L), :].astype(BF16)
                                           for t in range(8)])
                for g8 in range(8):
                    u_ref[8 * o + g8, c0:c0 + nchunk, LANES * hh:LANES * (hh + 1)] = outs[g8]
        first = max(r0, tile - keep)
        if first < r0 + block:
            dst = slice(first - (tile - keep), r0 + block - (tile - keep))
            klast_ref[0, dst] = kr[first - r0:, :]
            vlast_ref[0, dst] = v[first - r0:, :]


def _inproj_call(x, sc, sh, ln_g, w_in, gq, gk, ones, cos, s1, s2, *, tile, keep):
    nb, rows, d = x.shape
    in_cols = w_in.shape[1]
    q_cols = N_HEADS * HEAD_DIM
    kv_cols = N_KV_HEADS * HEAD_DIM
    u_cols = in_cols - q_cols - 2 * kv_cols
    mrows = sc.shape[1]
    mtile = 1 if mrows == 1 else tile
    mod_spec = pl.BlockSpec((1, mtile, d), (lambda b, i: (b, 0, 0)) if mrows == 1 else (lambda b, i: (b, i, 0)))
    row_spec = lambda c: pl.BlockSpec((1, tile, c), lambda b, i: (b, i, 0))
    tab_spec = pl.BlockSpec((tile, LANES), lambda b, i: (i, 0))
    last_spec = pl.BlockSpec((1, keep, kv_cols), lambda b, i: (b, 0, 0))
    groups = u_cols // SSM_CH
    tiles = rows // tile
    chunk_spec = pl.BlockSpec((groups, tile // SSM_L, SSM_L * SSM_CH), lambda b, i: (0, b * tiles + i, 0))
    kern = functools.partial(_inproj_kernel, q_cols=q_cols, kv_cols=kv_cols, keep=keep)
    return pl.pallas_call(
        kern,
        grid=(nb, tiles),
        in_specs=[row_spec(d), mod_spec, mod_spec, _const_spec((1, d)), _const_spec((d, in_cols)),
                  _const_spec((1, q_cols)), _const_spec((1, kv_cols)), _const_spec((256, 256)),
                  tab_spec, tab_spec, tab_spec],
        out_specs=[row_spec(q_cols), row_spec(kv_cols), row_spec(kv_cols), chunk_spec, last_spec, last_spec],
        out_shape=[jax.ShapeDtypeStruct((nb, rows, q_cols), BF16),
                   jax.ShapeDtypeStruct((nb, rows, kv_cols), BF16),
                   jax.ShapeDtypeStruct((nb, rows, kv_cols), BF16),
                   jax.ShapeDtypeStruct((groups, nb * rows // SSM_L, SSM_L * SSM_CH), BF16),
                   jax.ShapeDtypeStruct((nb, keep, kv_cols), F32),
                   jax.ShapeDtypeStruct((nb, keep, kv_cols), F32)],
        scratch_shapes=[pltpu.VMEM((u_cols // LANES, tile, LANES), F32)],
        compiler_params=_params(2),
        name="inproj",
    )(x, sc, sh, ln_g, w_in, gq, gk, ones, cos, s1, s2)


def _ssm_kernel(u_ref, h0_ref, wi_ref, krow_ref, cpow_ref, lam_ref, y_ref, hout_ref,
                toep, s_re, s_im, hp_re, hp_im, *, nseq, nchunk):
    groups = u_ref.shape[0]
    rows = nseq * nchunk
    npair = groups // 2

    @pl.when(pl.program_id(0) == 0)
    def _():
        lane = lax.broadcasted_iota(jnp.int32, (SSM_CH, SSM_L * SSM_CH), 1)

        def expand(g, carry):
            kr = krow_ref[g]
            for t in range(SSM_L):
                blk = kr if t == 0 else jnp.where(lane >= SSM_CH * t, pltpu.roll(kr, SSM_CH * t, 1), 0.0)
                toep[g, SSM_CH * t:SSM_CH * (t + 1), :] = blk.astype(BF16)
            return carry

        lax.fori_loop(0, groups, expand, 0)

    for p in range(npair):
        lhs = jnp.concatenate([u_ref[2 * p], u_ref[2 * p + 1]], axis=1)
        s = _dot(lhs, wi_ref[p])
        s_re[pl.ds(p, rows, stride=SCAN_PITCH), :] = s[:, :LANES]
        s_im[pl.ds(p, rows, stride=SCAN_PITCH), :] = s[:, LANES:]

    a_re = lam_ref[0]
    a_im = lam_ref[1]
    for b in range(nseq):
        def step(n, carry):
            h_re, h_im = carry
            r0 = pl.multiple_of((b * nchunk + n) * SCAN_PITCH, 8)
            hp_re[pl.ds(r0, npair), :] = h_re
            hp_im[pl.ds(r0, npair), :] = h_im
            n_re = a_re * h_re - a_im * h_im + s_re[pl.ds(r0, npair), :]
            n_im = a_re * h_im + a_im * h_re + s_im[pl.ds(r0, npair), :]
            return n_re, n_im
        h_re, h_im = lax.fori_loop(0, nchunk, step, (h0_ref[b, 0], h0_ref[b, 1]), unroll=True)
        hout_ref[b, 0] = h_re
        hout_ref[b, 1] = h_im

    for p in range(npair):
        hp = jnp.concatenate([hp_re[pl.ds(p, rows, stride=SCAN_PITCH), :],
                              hp_im[pl.ds(p, rows, stride=SCAN_PITCH), :]], axis=1).astype(BF16)
        for g in (2 * p, 2 * p + 1):
            y_ref[g] = (_dot(u_ref[g], toep[g]) + _dot(hp, cpow_ref[g])).astype(BF16)


def _ssm_call(u2, h0, wi, krow, cpow, lam, *, nseq, nchunk):
    groups, total_rows, width = u2.shape
    rows = nseq * nchunk
    npair = groups // 2
    kern = functools.partial(_ssm_kernel, nseq=nseq, nchunk=nchunk)
    return pl.pallas_call(
        kern,
        grid=(total_rows // rows,),
        in_specs=[pl.BlockSpec((groups, rows, width), lambda i: (0, i, 0)),
                  pl.BlockSpec((nseq, 2, npair, LANES), lambda i: (i, 0, 0, 0)),
                  _const_spec(wi.shape), _const_spec(krow.shape), _const_spec(cpow.shape), _const_spec(lam.shape)],
        out_specs=[pl.BlockSpec((groups, rows, width), lambda i: (0, i, 0)),
                   pl.BlockSpec((nseq, 2, npair, LANES), lambda i: (i, 0, 0, 0))],
        out_shape=[jax.ShapeDtypeStruct((groups, total_rows, width), BF16),
                   jax.ShapeDtypeStruct(h0.shape, F32)],
        scratch_shapes=[pltpu.VMEM((groups, width, width), BF16)] + [pltpu.VMEM((rows * SCAN_PITCH, LANES), F32)] * 4,
        compiler_params=_params(1),
        name="ssm",
    )(u2, h0, wi, krow, cpow, lam)


def _kv_dup(a):
    lo = lax.broadcasted_iota(jnp.int32, a.shape, 1) < HEAD_DIM
    sw = pltpu.roll(a, HEAD_DIM, 1)
    return jnp.where(lo, a, sw), jnp.where(lo, sw, a)


def _v_ext(v):
    return jnp.concatenate([v, jnp.ones_like(v)], axis=1).astype(BF16)


def _sink_tail(rows):
    row = lax.broadcasted_iota(jnp.int32, (rows, 2 * LANES), 0)
    lane = lax.broadcasted_iota(jnp.int32, (rows, 2 * LANES), 1)
    return jnp.where(jnp.logical_and(row == 0, lane >= LANES), 1.0, 0.0).astype(BF16)


def _sink_attention(sink_ref, problems, nkeys):
    rep = N_HEADS // N_KV_HEADS
    col = lax.broadcasted_iota(jnp.int32, (1, KEYS_PAD), 1)
    real = col < nkeys
    scores = []
    for g, qa, qb, kb, vb, valid, store in problems:
        half = qa.shape[0]
        qst = jnp.concatenate([qa, qb], axis=0)
        lo = lax.broadcasted_iota(jnp.int32, qst.shape, 1) < HEAD_DIM
        zero = jnp.zeros_like(qst)
        q4 = jnp.concatenate([jnp.where(lo, qst, zero), jnp.where(lo, zero, qst)], axis=0)
        s = lax.dot_general(q4, kb, (((1,), (1,)), ((), ())), preferred_element_type=F32)
        keep = real if valid is None else jnp.logical_and(real, valid)
        first = 0 if valid is not None else (nkeys // LANES) * LANES
        blocks = []
        for e in range(2):
            for jj in range(2):
                fill = jnp.where(col == nkeys, sink_ref[rep * g + 2 * jj + e] * LOG2E, -jnp.inf)
                sb = s[(2 * e + jj) * half:(2 * e + jj + 1) * half]
                fixed = jnp.where(keep[:, first:], sb[:, first:], fill[:, first:])
                blocks.append(fixed if first == 0 else jnp.concatenate([sb[:, :first], fixed], axis=1))
        scores.append(jnp.concatenate(blocks, axis=0))
    probs = []
    for s in scores:
        probs.append(jnp.exp2(s - jnp.max(s, axis=-1, keepdims=True)).astype(BF16))
    for p, (g, qa, qb, kb, vb, valid, store) in zip(probs, problems):
        o4 = _dot(p, vb)
        h2 = 2 * qa.shape[0]
        lo = lax.broadcasted_iota(jnp.int32, (h2, LANES), 1) < HEAD_DIM
        store(jnp.where(lo, o4[:h2, :LANES], o4[h2:, :LANES]) / jnp.where(lo, o4[:h2, LANES:], o4[h2:, LANES:]))


def _rotated_q(q_ref, tabs):
    cos_ref, s1_ref, s2_ref = tabs
    return _rope(q_ref[0].astype(F32), cos_ref[...], s1_ref[...], s2_ref[...]).astype(BF16)


def _attn_prompt(q_ref, tabs, kc_ref, kp_ref, vc_ref, vp_ref, sink_ref, kz, vz, attn, tile, later_tile):
    band = CHUNK + WINDOW
    kd = _kv_dup(jnp.concatenate([kp_ref[0], kc_ref[0]], axis=0).astype(F32))
    vd = _kv_dup(jnp.concatenate([vp_ref[0], vc_ref[0]], axis=0).astype(F32))
    tail = _sink_tail(KEYS_PAD - band)
    for g in range(2):
        kz[g, :tile + WINDOW] = kd[g].astype(BF16)
        kz[g, tile + WINDOW:] = jnp.zeros((KEYS_PAD - band, LANES), BF16)
        vext = _v_ext(vd[g])
        for c in range(tile // CHUNK):
            vz[g, c, :band] = vext[c * CHUNK:c * CHUNK + band]
            vz[g, c, band:] = tail
    col_chunk = lax.broadcasted_iota(jnp.int32, (1, KEYS_PAD), 1) // CHUNK
    q = _rotated_q(q_ref, tabs)
    problems = []
    for c in range(tile // CHUNK):
        r0 = c * CHUNK
        valid = None if c >= WINDOW // CHUNK else jnp.logical_or(col_chunk + c >= WINDOW // CHUNK, later_tile)
        for g in range(2):
            c0 = 2 * LANES * g

            def store(o, r0=r0, c0=c0):
                attn[r0:r0 + CHUNK, c0:c0 + LANES] = o[:CHUNK]
                attn[r0:r0 + CHUNK, c0 + LANES:c0 + 2 * LANES] = o[CHUNK:]

            problems.append((g, q[r0:r0 + CHUNK, c0:c0 + LANES], q[r0:r0 + CHUNK, c0 + LANES:c0 + 2 * LANES],
                             kz[g, r0:r0 + KEYS_PAD, :], vz[g, c], valid, store))
    _sink_attention(sink_ref, problems, band)


def _attn_sample(q_ref, tabs, kc_ref, kp_ref, vc_ref, vp_ref, sink_ref, attn, nseq, seq):
    nkeys = WINDOW + seq
    tail = _sink_tail(KEYS_PAD - nkeys)
    q = _rotated_q(q_ref, tabs)
    problems = []
    for b in range(nseq):
        r0 = b * seq
        kd = _kv_dup(jnp.concatenate([kp_ref[b], kc_ref[0, r0:r0 + seq, :]], axis=0).astype(F32))
        vd = _kv_dup(jnp.concatenate([vp_ref[b], vc_ref[0, r0:r0 + seq, :]], axis=0).astype(F32))
        for g in range(2):
            c0 = 2 * LANES * g

            def store(o, r0=r0, c0=c0):
                attn[r0:r0 + seq, c0:c0 + LANES] = o[:seq]
                attn[r0:r0 + seq, c0 + LANES:c0 + 2 * LANES] = o[seq:]

            kb = jnp.concatenate([kd[g].astype(BF16), jnp.zeros((KEYS_PAD - nkeys, LANES), BF16)], axis=0)
            vb = jnp.concatenate([_v_ext(vd[g]), tail], axis=0)
            problems.append((g, q[r0:r0 + seq, c0:c0 + LANES], q[r0:r0 + seq, c0 + LANES:c0 + 2 * LANES],
                             kb, vb, None, store))
    _sink_attention(sink_ref, problems, nkeys)


def _gelu_tanh(x):
    return 0.5 * x * (1.0 + jnp.tanh(math.sqrt(2.0 / math.pi) * (x + 0.044715 * (x * x * x))))


def _mix_body(sink_ref, x_ref, g1_ref, q_ref, kc_ref, kp_ref, vc_ref, vp_ref, y_ref, gluw_ref, glub_ref, ga_ref,
              gs_ref, wout_ref, cos_ref, s1_ref, s2_ref, out, scratch, *, tile, nseq, later_tile=None):
    tabs = (cos_ref, s1_ref, s2_ref)
    if nseq is None:
        yscr, kz, vz, attn = scratch
        _attn_prompt(q_ref, tabs, kc_ref, kp_ref, vc_ref, vp_ref, sink_ref, kz, vz, attn, tile, later_tile)
    else:
        yscr, attn = scratch
        _attn_sample(q_ref, tabs, kc_ref, kp_ref, vc_ref, vp_ref, sink_ref, attn, nseq, tile // nseq)
    nslab = yscr.shape[0]
    for o in range(nslab):
        for hh in range(SSM_L // 8):
            outs = _granule_transpose([y_ref[8 * o + g8, :, LANES * hh:LANES * (hh + 1)] for g8 in range(8)])
            for t8 in range(8):
                yscr[o, pl.ds(8 * hh + t8, tile // SSM_L, stride=SSM_L), :] = outs[t8].astype(F32)
    block = min(tile, ROW_BLOCK)
    for r0 in range(0, tile, block):
        rows = slice(r0, r0 + block)
        an = _rms(attn[rows]) * ga_ref[...]
        gl = _gelu_tanh(jnp.concatenate([yscr[o, rows] for o in range(nslab)], axis=1))
        so = gl * jax.nn.sigmoid(_dot(gl.astype(BF16), gluw_ref[...]) + glub_ref[...])
        sn = _rms(so) * gs_ref[...]
        merged = jnp.concatenate([an, sn], axis=1).astype(BF16)
        g1 = g1_ref[0] if g1_ref.shape[1] == 1 else g1_ref[0, rows]
        out[rows] = x_ref[0, rows] + g1 * _dot(merged, wout_ref[...])


def _mix_kernel(*refs, tile, nseq):
    _mix_body(*refs[:MIX_INPUTS], refs[MIX_INPUTS].at[0], refs[MIX_INPUTS + 1:], tile=tile, nseq=nseq)


def _mix_call(sinks, x, g1, q, k, v, k_past, v_past, y, gluw, glub, ga, gs, wout, tabs, *, tile, nseq):
    nb, rows, d = x.shape
    aw = q.shape[2]
    kvw = k.shape[2]
    groups, _, cw = y.shape
    tiles = rows // tile
    row_spec = lambda c: pl.BlockSpec((1, tile, c), lambda b, i: (b, i, 0))
    chunk_spec = pl.BlockSpec((groups, tile // SSM_L, cw), lambda b, i: (0, b * tiles + i, 0))
    past_spec = pl.BlockSpec((nseq, WINDOW, kvw), lambda b, i: (b * tiles + i, 0, 0))
    tab_spec = pl.BlockSpec((tile, LANES), lambda b, i: (i, 0))
    scratch = [pltpu.VMEM((groups * SSM_CH // LANES, tile, LANES), F32), pltpu.VMEM((tile, aw), F32)]
    kern = functools.partial(_mix_kernel, tile=tile, nseq=nseq)
    return pl.pallas_call(
        kern,
        grid=(nb, tiles),
        in_specs=[pl.BlockSpec(memory_space=pltpu.SMEM),
                  row_spec(d), row_spec(d), row_spec(aw), row_spec(kvw), past_spec, row_spec(kvw), past_spec,
                  chunk_spec, _const_spec(gluw.shape), _const_spec(glub.shape), _const_spec(ga.shape),
                  _const_spec(gs.shape), _const_spec(wout.shape), tab_spec, tab_spec, tab_spec],
        out_specs=row_spec(d),
        out_shape=jax.ShapeDtypeStruct((nb, rows, d), F32),
        scratch_shapes=scratch,
        compiler_params=_params(2),
        name="mix",
    )(sinks, x, g1, q, k, k_past, v, v_past, y, gluw, glub, ga, gs, wout, *tabs)


def _ffn_body(x, sc_ref, sh_ref, g2_ref, ln_ref, wg_ref, wu_ref, wd_ref, out, ff_bounds):
    h = (_rms(x[...]) * (ln_ref[...] * (1.0 + sc_ref[0])) + sh_ref[0]).astype(BF16)
    acc = None
    for c0, c1 in zip(ff_bounds[:-1], ff_bounds[1:]):
        a = _dot(h, wg_ref[:, c0:c1])
        b = _dot(h, wu_ref[:, c0:c1])
        part = _dot((a * jax.nn.sigmoid(a) * b).astype(BF16), wd_ref[c0:c1, :])
        acc = part if acc is None else acc + part
    out[...] = x[...] + g2_ref[0] * acc


def _ffn_kernel(x_ref, sc_ref, sh_ref, g2_ref, ln_ref, wg_ref, wu_ref, wd_ref, o_ref, *, ff_bounds):
    _ffn_body(x_ref.at[0], sc_ref, sh_ref, g2_ref, ln_ref, wg_ref, wu_ref, wd_ref, o_ref.at[0], ff_bounds)


def _ff_bounds(dff):
    split = -(-dff // (2 * MXU_TILE)) * MXU_TILE
    return (0, split, dff) if split < dff else (0, dff)


def _mixffn_kernel(*refs, tile, tiles, ff_bounds):
    mix_in, ffn_in = refs[:MIX_INPUTS], refs[MIX_INPUTS:MIX_INPUTS + FFN_INPUTS]
    o_ref, x1_buf, *mix_scratch = refs[MIX_INPUTS + FFN_INPUTS:]
    step = pl.program_id(0)

    @pl.when(step == 0)
    def _():
        x1_buf[...] = jnp.zeros(x1_buf.shape, F32)

    _ffn_body(x1_buf, *ffn_in, o_ref.at[0], ff_bounds)
    tile_in_seq = jnp.minimum(step, pl.num_programs(0) - 2) % tiles
    _mix_body(*mix_in, x1_buf, mix_scratch, tile=tile, nseq=None, later_tile=tile_in_seq > 0)


def _mixffn_call(sinks, x, g1, q, k, v, y, gluw, glub, ga, gs, wout, tabs, sc, sh, g2, ln_g, wg, wu, wd, *, tile):
    nb, rows, d = x.shape
    aw = q.shape[2]
    kvw = k.shape[2]
    groups, _, cw = y.shape
    tiles = rows // tile
    last = nb * tiles - 1
    wpt = tile // WINDOW

    def mix_bi(s):
        sm = jnp.minimum(s, last)
        return sm // tiles, sm % tiles

    def ffn_bi(s):
        sf = jnp.maximum(s - 1, 0)
        return sf // tiles, sf % tiles

    row_spec = lambda c: pl.BlockSpec((1, tile, c), lambda s: (*mix_bi(s), 0))
    mod_spec = pl.BlockSpec((1, 1, d), lambda s: (mix_bi(s)[0], 0, 0))
    past_spec = pl.BlockSpec((1, WINDOW, kvw),
                             lambda s: (mix_bi(s)[0], jnp.maximum(mix_bi(s)[1] * wpt - 1, 0), 0))
    chunk_spec = pl.BlockSpec((groups, tile // SSM_L, cw), lambda s: (0, jnp.minimum(s, last), 0))
    tab_spec = pl.BlockSpec((tile, LANES), lambda s: (mix_bi(s)[1], 0))
    ffn_mod = pl.BlockSpec((1, 1, d), lambda s: (ffn_bi(s)[0], 0, 0))
    scratch = [pltpu.VMEM((tile, d), F32),
               pltpu.VMEM((groups * SSM_CH // LANES, tile, LANES), F32),
               pltpu.VMEM((2, tile + KEYS_PAD - CHUNK, LANES), BF16),
               pltpu.VMEM((2, tile // CHUNK, KEYS_PAD, 2 * LANES), BF16), pltpu.VMEM((tile, aw), F32)]
    kern = functools.partial(_mixffn_kernel, tile=tile, tiles=tiles, ff_bounds=_ff_bounds(wg.shape[1]))
    return pl.pallas_call(
        kern,
        grid=(nb * tiles + 1,),
        in_specs=[pl.BlockSpec(memory_space=pltpu.SMEM),
                  row_spec(d), mod_spec, row_spec(aw), row_spec(kvw), past_spec, row_spec(kvw), past_spec,
                  chunk_spec, _const_spec(gluw.shape), _const_spec(glub.shape), _const_spec(ga.shape),
                  _const_spec(gs.shape), _const_spec(wout.shape), tab_spec, tab_spec, tab_spec,
                  ffn_mod, ffn_mod, ffn_mod, _const_spec((1, d)),
                  _const_spec(wg.shape), _const_spec(wu.shape), _const_spec(wd.shape)],
        out_specs=pl.BlockSpec((1, tile, d), lambda s: (*ffn_bi(s), 0)),
        out_shape=jax.ShapeDtypeStruct((nb, rows, d), F32),
        scratch_shapes=scratch,
        compiler_params=pltpu.CompilerParams(dimension_semantics=("arbitrary",), vmem_limit_bytes=VMEM_LIMIT_FUSED),
        name="mixffn",
    )(sinks, x, g1, q, k, k, v, v, y, gluw, glub, ga, gs, wout, *tabs, sc, sh, g2, ln_g, wg, wu, wd)


def _ffn_call(x, sc, sh, g2, ln_g, wg, wu, wd, *, tile):
    nb, rows, d = x.shape
    ff_bounds = _ff_bounds(wg.shape[1])
    mrows = sc.shape[1]
    mtile = 1 if mrows == 1 else tile
    mod_spec = pl.BlockSpec((1, mtile, d), (lambda b, i: (b, 0, 0)) if mrows == 1 else (lambda b, i: (b, i, 0)))
    row_spec = pl.BlockSpec((1, tile, d), lambda b, i: (b, i, 0))
    kern = functools.partial(_ffn_kernel, ff_bounds=ff_bounds)
    return pl.pallas_call(
        kern,
        grid=(nb, rows // tile),
        in_specs=[row_spec, mod_spec, mod_spec, mod_spec, _const_spec((1, d)),
                  _const_spec(wg.shape), _const_spec(wu.shape), _const_spec(wd.shape)],
        out_specs=row_spec,
        out_shape=jax.ShapeDtypeStruct((nb, rows, d), F32),
        compiler_params=_params(2),
        name="ffn",
    )(x, sc, sh, g2, ln_g, wg, wu, wd)


def _split_bf16(a):
    hi = a.astype(BF16)
    return hi, (a - hi.astype(F32)).astype(BF16)


def _dot3(a, b):
    ah, al = _split_bf16(a)
    bh, bl = _split_bf16(b)
    return _dot(ah, bh) + _dot(ah, bl) + _dot(al, bh)


def _ssm_prep_kernel(*refs):
    for gi in range(refs[0].shape[0]):
        _ssm_prep_group(*[r.at[pl.ds(gi, 1)] for r in refs])


def _ssm_prep_group(logdt_ref, ar_ref, ai_ref, bt_re_ref, bt_im_ref, ct_re_ref, ct_im_ref, d_ref,
                     krow_ref, y_re_ref, y_im_ref, wi_re_ref, wi_im_ref, lam_ref):
    n, ch = SSM_L, SSM_CH
    dt = jnp.exp(logdt_ref[0])
    ar, ai = ar_ref[0], ai_ref[0]
    zr, zi = ar * dt, ai * dt
    ez = jnp.exp(zr)
    e_re = jnp.tanh(0.5 * zr) * (ez + 1.0) * jnp.cos(zi) - 2.0 * jnp.sin(0.5 * zi) ** 2
    e_im = ez * jnp.sin(zi)
    mag = ar * ar + ai * ai
    coef_re = (e_re * ar + e_im * ai) / mag
    coef_im = (e_im * ar - e_re * ai) / mag
    bt_re, bt_im = bt_re_ref[0], bt_im_ref[0]
    bb_re = coef_re * bt_re - coef_im * bt_im
    bb_im = coef_re * bt_im + coef_im * bt_re
    rows = 2 * n
    jc = lax.broadcasted_iota(jnp.int32, (rows, 1), 0).astype(F32)
    mg = jnp.exp(jc * zr)
    pr, pi = mg * jnp.cos(jc * zi), mg * jnp.sin(jc * zi)
    for t in range(n):
        r_re, r_im = pr[n - 1 - t:n - t], pi[n - 1 - t:n - t]
        wi_re_ref[0, ch * t:ch * (t + 1), :] = bb_re * r_re - bb_im * r_im
        wi_im_ref[0, ch * t:ch * (t + 1), :] = bb_re * r_im + bb_im * r_re
    lam_ref[0, 0:1, :] = pr[n:n + 1]
    lam_ref[0, 1:2, :] = pi[n:n + 1]
    lag = lax.broadcasted_iota(jnp.int32, (rows, n * ch), 1) // ch
    jrow = lax.broadcasted_iota(jnp.int32, (rows, n * ch), 0)
    chan = lax.broadcasted_iota(jnp.int32, (ch, n * ch), 1) % ch
    spread_chan = (chan == lax.broadcasted_iota(jnp.int32, (ch, n * ch), 0)).astype(BF16)
    nn, tn = (((1,), (0,)), ((), ())), (((0,), (0,)), ((), ()))

    def spread(a, onehot, dims):
        hi, lo = _split_bf16(a)
        return (lax.dot_general(hi, onehot, dims, preferred_element_type=F32)
                + lax.dot_general(lo, onehot, dims, preferred_element_type=F32))

    def powers(shift):
        onehot = (lag + shift == jrow).astype(BF16)
        return spread(pr, onehot, tn), spread(pi, onehot, tn)

    c_re = spread(ct_re_ref[0], spread_chan, nn)
    c_im = spread(ct_im_ref[0], spread_chan, nn)
    l1_re, l1_im = powers(1)
    y_re_ref[0] = c_re * l1_re - c_im * l1_im
    y_im_ref[0] = c_re * l1_im + c_im * l1_re
    l0_re, l0_im = powers(0)
    k_re = c_re * l0_re - c_im * l0_im
    k_im = c_re * l0_im + c_im * l0_re
    lane = lax.broadcasted_iota(jnp.int32, (ch, n * ch), 1)
    d_lag0 = jnp.where(lane == lax.broadcasted_iota(jnp.int32, (ch, n * ch), 0), d_ref[0], 0.0)
    krow_ref[0] = _dot3(bb_re, k_re) - _dot3(bb_im, k_im) + d_lag0


def _ssm_weights(a_re, a_im, log_dt, b_re, b_im, c_re, c_im, d_skip):
    groups, state = a_re.shape
    ch = b_re.shape[2]
    n = SSM_L
    row = lambda a: a.reshape(groups, 1, state)
    tr =lambda a: jnp.transpose(a, (0, 2, 1))
    d_pad = jnp.pad(d_skip.reshape(groups, 1, ch), ((0, 0), (0, 0), (0, (n - 1) * ch)))
    per_step = math.gcd(groups, PREP_GROUPS)
    blk = lambda *shape: pl.BlockSpec((per_step,) + shape, lambda g: (g, 0, 0))
    krow, y_re, y_im, wi_re, wi_im, lam = pl.pallas_call(
        _ssm_prep_kernel,
        grid=(groups // per_step,),
        in_specs=[blk(1, 1), blk(1, state), blk(1, state),
                  blk(ch, state), blk(ch, state), blk(state, ch), blk(state, ch), blk(1, n * ch)],
        out_specs=[blk(ch, n * ch), blk(state, n * ch), blk(state, n * ch), blk(n * ch, state),
                   blk(n * ch, state), blk(2, state)],
        out_shape=[jax.ShapeDtypeStruct((groups, ch, n * ch), F32),
                   jax.ShapeDtypeStruct((groups, state, n * ch), F32),
                   jax.ShapeDtypeStruct((groups, state, n * ch), F32),
                   jax.ShapeDtypeStruct((groups, n * ch, state), F32),
                   jax.ShapeDtypeStruct((groups, n * ch, state), F32),
                   jax.ShapeDtypeStruct((groups, 2, state), F32)],
        compiler_params=_params(1),
        name="ssm_prep",
    )(log_dt.reshape(groups, 1, 1), row(a_re), row(a_im), tr(b_re), tr(b_im), tr(c_re), tr(c_im), d_pad)
    pair = lambda a: a.reshape((groups // 2, 2) + a.shape[1:])
    wr, wm = pair(wi_re), pair(wi_im)
    zero = jnp.zeros_like(wr[:, 0])
    top = jnp.concatenate([wr[:, 0], zero, wm[:, 0], zero], axis=2)
    bot = jnp.concatenate([zero, wr[:, 1], zero, wm[:, 1]], axis=2)
    wi_pair = jnp.concatenate([top, bot], axis=1).astype(BF16)
    yr, ym = pair(y_re), pair(y_im)
    zc = jnp.zeros_like(yr[:, 0])
    even = jnp.concatenate([yr[:, 0], zc, -ym[:, 0], zc], axis=1)
    odd = jnp.concatenate([zc, yr[:, 1], zc, -ym[:, 1]], axis=1)
    cpow = jnp.stack([even, odd], axis=1).reshape(groups, 4 * state, n * ch).astype(BF16)
    lam16 = jnp.transpose(lam, (1, 0, 2)).reshape(2, groups // 2, 2 * state)
    return wi_pair, krow, cpow, lam16


def _rope_tables(pos):
    half = HEAD_DIM // 2
    inv = ROPE_THETA ** (-jnp.arange(half, dtype=F32) * 2.0 / HEAD_DIM)
    lane = jnp.arange(LANES)
    ang = pos.astype(F32)[:, None] * inv[lane % half][None, :]
    first = ((lane % HEAD_DIM) < half)[None, :]
    sin = jnp.sin(ang)
    return jnp.cos(ang), jnp.where(first, -sin, 0.0), jnp.where(first, 0.0, sin)


def _state_in(state):
    b, groups, p, _ = state.shape
    return jnp.transpose(state, (0, 3, 1, 2)).reshape(b, 2, groups // 2, 2 * p)


def _state_out(h, groups):
    b = h.shape[0]
    return jnp.transpose(h.reshape(b, 2, groups, -1), (0, 2, 3, 1))


def _stream(x, mods, tabs, past, h0, lw, *, tile, keep, nseq_tile, seqs, seq_len):
    sh1, sc1, g1, sh2, sc2, g2 = mods
    nb, rows, d = x.shape
    groups = lw['groups']
    q, k, v, u2, k_last, v_last = _inproj_call(x, sc1, sh1, lw['ln1'], lw['w_in'], lw['gq'], lw['gk'], lw['ones'],
                                               *tabs, tile=min(rows, INPROJ_TILE), keep=keep)
    y2, h_last = _ssm_call(u2, h0, lw['wi'], lw['krow'], lw['cpow'], lw['lam'],
                           nseq=(1 if nseq_tile is None else nseq_tile), nchunk=seq_len // SSM_L)
    if past is None:
        out = _mixffn_call(lw['sinks'], x, g1, q, k, v, y2, lw['gluw'], lw['glub'], lw['ga'], lw['gs'], lw['w_out'],
                           tabs, sc2, sh2, g2, lw['ln2'], lw['wg'], lw['wu'], lw['wd'], tile=tile)
    else:
        x1 = _mix_call(lw['sinks'], x, g1, q, k, v, *past, y2, lw['gluw'], lw['glub'], lw['ga'], lw['gs'],
                       lw['w_out'], tabs, tile=tile, nseq=nseq_tile)
        out = _ffn_call(x1, sc2, sh2, g2, lw['ln2'], lw['wg'], lw['wu'], lw['wd'], tile=tile)
    return out, k_last, v_last, h_last


def kernel(x_prompt, x_sample, cache_k, cache_v, state_ssm, c_prompt, c_sample, w_ada, b_ada, ln1_g, w_in, q_norm_g, k_norm_g, attn_sinks, ssm_A_re, ssm_A_im, ssm_log_dt, ssm_B_re, ssm_B_im, ssm_C_re, ssm_C_im, ssm_D, ssm_glu_w, ssm_glu_b, attn_out_g, ssm_out_g, w_out, ln2_g, w_gate, w_up, w_down):
    depth = w_ada.shape[0]
    bp, sp, d = x_prompt.shape
    bs, ss, _ = x_sample.shape
    groups = ssm_A_re.shape[1]
    kvw = N_KV_HEADS * HEAD_DIM
    tile_p = min(512, sp)
    seg = jnp.arange(256) // HEAD_DIM
    ones = (seg[:, None] == seg[None, :]).astype(BF16)
    tabs_p = _rope_tables(jnp.arange(sp))
    tabs_s = tuple(jnp.tile(t, (bs, 1)) for t in _rope_tables(PAST_LEN + jnp.arange(ss)))

    yp = x_prompt
    ys = x_sample.reshape(1, bs * ss, d)
    outs = [[] for _ in range(6)]
    for l in range(depth):
        wi, krow, cpow, lam = _ssm_weights(ssm_A_re[l], ssm_A_im[l], ssm_log_dt[l], ssm_B_re[l], ssm_B_im[l],
                                           ssm_C_re[l], ssm_C_im[l], ssm_D[l])
        lw = dict(groups=groups, ln1=ln1_g[l][None], w_in=w_in[l].astype(BF16),
                  gq=jnp.tile(q_norm_g[l], N_HEADS)[None] * (HEAD_DIM ** -0.5 * LOG2E), gk=jnp.tile(k_norm_g[l], N_KV_HEADS)[None], ones=ones,
                  wi=wi, krow=krow, cpow=cpow, lam=lam, sinks=attn_sinks[l],
                  gluw=ssm_glu_w[l].astype(BF16), glub=ssm_glu_b[l][None], ga=attn_out_g[l][None],
                  gs=ssm_out_g[l][None], w_out=w_out[l].astype(BF16), ln2=ln2_g[l][None],
                  wg=w_gate[l].astype(BF16), wu=w_up[l].astype(BF16), wd=w_down[l].astype(BF16))
        mod = _mod_call(jnp.concatenate([c_prompt, c_sample], axis=0), w_ada[l], b_ada[l])
        mods_p = tuple(m[:, None, :] for m in jnp.split(mod[:bp], 6, axis=-1))
        mods_s = tuple(jnp.repeat(m, ss, axis=0)[None] for m in jnp.split(mod[bp:], 6, axis=-1))

        h0_p = jnp.zeros((bp, 2, groups // 2, 2 * SSM_STATE), F32)
        yp, kpl, vpl, hpl = _stream(yp, mods_p, tabs_p, None, h0_p, lw, tile=tile_p, keep=WINDOW,
                                    nseq_tile=None, seqs=bp, seq_len=sp)
        past = (cache_k[l].reshape(bs, WINDOW, kvw).astype(BF16), cache_v[l].reshape(bs, WINDOW, kvw).astype(BF16))
        ys, ksl, vsl, hsl = _stream(ys, mods_s, tabs_s, past, _state_in(state_ssm[l]), lw, tile=bs * ss,
                                    keep=bs * ss, nseq_tile=bs, seqs=bs, seq_len=ss)
        outs[0].append(kpl.reshape(bp, WINDOW, N_KV_HEADS, HEAD_DIM))
        outs[1].append(vpl.reshape(bp, WINDOW, N_KV_HEADS, HEAD_DIM))
        outs[2].append(_state_out(hpl, groups))
        outs[3].append(ksl.reshape(bs, ss, N_KV_HEADS, HEAD_DIM))
        outs[4].append(vsl.reshape(bs, ss, N_KV_HEADS, HEAD_DIM))
        outs[5].append(_state_out(hsl, groups))
    return (yp, ys.reshape(bs, ss, d)) + tuple(jnp.stack(o) for o in outs)
```

```python
import functools
import math

import jax
import jax.numpy as jnp
from jax import lax
from jax.experimental import pallas as pl
from jax.experimental.pallas import tpu as pltpu

F32 = jnp.float32
BF16 = jnp.bfloat16

CHUNK = 64
WINDOW = 128
HEAD_DIM = 64
N_HEADS = 8
N_KV_HEADS = 2
SSM_CH = 16
SSM_STATE = 64
SSM_L = 16
ROPE_THETA = 10000.0
EPS = 1e-6
PAST_LEN = 1024
LANES = 128
MXU_TILE = 256
KEYS_PAD = MXU_TILE
ROW_BLOCK = 256
SCAN_PITCH = 24
PREP_GROUPS = 4
INPROJ_TILE = 2048
X_SLOTS = 3
MIX_INPUTS = 17
FFN_INPUTS = 7
LOG2E = math.log2(math.e)
VMEM_LIMIT = 56 * 1024 * 1024
VMEM_LIMIT_FUSED = 60 * 1024 * 1024


def _const_spec(shape):
    nd = len(shape)
    return pl.BlockSpec(shape, lambda *_: (0,) * nd, pipeline_mode=pl.Buffered(1))


def _params(n_grid):
    return pltpu.CompilerParams(dimension_semantics=("arbitrary",) * n_grid, vmem_limit_bytes=VMEM_LIMIT)


def _rms(x):
    return x * lax.rsqrt(jnp.mean(x * x, axis=-1, keepdims=True) + EPS)


def _dot(a, b):
    return jnp.dot(a, b, preferred_element_type=F32)


def _mod_kernel(c_ref, w_ref, b_ref, o_ref):
    c = c_ref[...]
    s = (c * jax.nn.sigmoid(c)).astype(BF16)
    o_ref[...] = _dot(s, w_ref[...].astype(BF16)) + b_ref[...]


def _mod_call(c, w, b):
    rows, d = c.shape
    cols = w.shape[1]
    tile = 1536
    return pl.pallas_call(
        _mod_kernel,
        grid=(cols // tile,),
        in_specs=[pl.BlockSpec((rows, d), lambda j: (0, 0)),
                  pl.BlockSpec((d, tile), lambda j: (0, j)),
                  pl.BlockSpec((1, tile), lambda j: (0, j))],
        out_specs=pl.BlockSpec((rows, tile), lambda j: (0, j)),
        out_shape=jax.ShapeDtypeStruct((rows, cols), F32),
        compiler_params=_params(1),
        name="mod",
    )(c, w, b.reshape(1, cols))


def _head_rms(t, ones_ref):
    width = t.shape[1]
    sq = t * t
    hi = sq.astype(BF16)
    lo = (sq - hi.astype(F32)).astype(BF16)
    parts = []
    for c0 in range(0, width, 256):
        w = min(256, width - c0)
        ones = ones_ref[:w, :w]
        parts.append(_dot(hi[:, c0:c0 + w], ones) + _dot(lo[:, c0:c0 + w], ones))
    ssq = parts[0] if len(parts) == 1 else jnp.concatenate(parts, axis=1)
    return t * lax.rsqrt(ssq * (1.0 / HEAD_DIM) + EPS)


def _rope(t, cos, s1, s2):
    outs = []
    for c0 in range(0, t.shape[1], LANES):
        xb = t[:, c0:c0 + LANES]
        outs.append(xb * cos + pltpu.roll(xb, LANES - HEAD_DIM // 2, 1) * s1 + pltpu.roll(xb, HEAD_DIM // 2, 1) * s2)
    return outs[0] if len(outs) == 1 else jnp.concatenate(outs, axis=1)


def _granule_transpose(arrs):
    gran = lax.broadcasted_iota(jnp.int32, arrs[0].shape, 1) // SSM_CH
    cur = list(arrs)
    for s in (4, 2, 1):
        upper = (gran & s) != 0
        nxt = list(cur)
        for a0 in range(8):
            if a0 & s:
                continue
            lo, hi = cur[a0], cur[a0 + s]
            nxt[a0] = jnp.where(upper, pltpu.roll(hi, SSM_CH * s, 1), lo)
            nxt[a0 + s] = jnp.where(upper, hi, pltpu.roll(lo, LANES - SSM_CH * s, 1))
        cur = nxt
    return cur


def _inproj_kernel(x_hbm, sc_ref, sh_ref, ln_ref, w_ref, gq_ref, gk_ref, ones_ref, cos_ref, s1_ref, s2_ref,
                   q_ref, k_ref, v_ref, u_ref, klast_ref, vlast_ref, uscr, xring, xsem, *, q_cols, kv_cols, keep):
    tile = xring.shape[1]
    tiles = pl.num_programs(1)
    nsteps = pl.num_programs(0) * tiles
    step = pl.program_id(0) * tiles + pl.program_id(1)

    def x_copy(s):
        slot = s % X_SLOTS
        return pltpu.make_async_copy(x_hbm.at[s // tiles, pl.ds((s % tiles) * tile, tile)], xring.at[slot],
                                     xsem.at[slot])

    @pl.when(step == 0)
    def _():
        for s in range(X_SLOTS - 1):
            @pl.when(s < nsteps)
            def _():
                x_copy(s).start()

    @pl.when(step + X_SLOTS - 1 < nsteps)
    def _():
        x_copy(step + X_SLOTS - 1).start()

    x_copy(step).wait()
    x_ref = xring.at[step % X_SLOTS]
    block = min(tile, ROW_BLOCK)
    for r0 in range(0, tile, block):
        rows = slice(r0, r0 + block)
        sc = sc_ref[0] if sc_ref.shape[1] == 1 else sc_ref[0, rows]
        sh = sh_ref[0] if sh_ref.shape[1] == 1 else sh_ref[0, rows]
        h = _rms(x_ref[rows]) * (ln_ref[...] * (1.0 + sc)) + sh
        proj = _dot(h.astype(BF16), w_ref[...])
        q = proj[:, :q_cols]
        k = proj[:, q_cols:q_cols + kv_cols]
        v = proj[:, q_cols + kv_cols:q_cols + 2 * kv_cols]
        u = proj[:, q_cols + 2 * kv_cols:]
        kr = _rope(_head_rms(k, ones_ref) * gk_ref[...], cos_ref[rows], s1_ref[rows], s2_ref[rows])
        q_ref[0, rows] = (_head_rms(q, ones_ref) * gq_ref[...]).astype(BF16)
        k_ref[0, rows] = kr.astype(BF16)
        v_ref[0, rows] = v.astype(BF16)
        nchunk = block // SSM_L
        c0 = r0 // SSM_L
        for o in range(u.shape[1] // LANES):
            uscr[o, rows] = u[:, LANES * o:LANES * (o + 1)]
        for o in range(u.shape[1] // LANES):
            for hh in range(SSM_L // 8):
                outs = _granule_transpose([uscr[o, pl.ds(r0 + 8 * hh + t, nchunk, stride=SSM_L), :].astype(BF16)
                                           for t in range(8)])
                for g8 in range(8):
                    u_ref[8 * o + g8, c0:c0 + nchunk, LANES * hh:LANES * (hh + 1)] = outs[g8]
        first = max(r0, tile - keep)
        if first < r0 + block:
            dst = slice(first - (tile - keep), r0 + block - (tile - keep))
            klast_ref[0, dst] = kr[first - r0:, :]
            vlast_ref[0, dst] = v[first - r0:, :]


def _inproj_call(x, sc, sh, ln_g, w_in, gq, gk, ones, cos, s1, s2, *, tile, keep):
    nb, rows, d = x.shape
    in_cols = w_in.shape[1]
    q_cols = N_HEADS * HEAD_DIM
    kv_cols = N_KV_HEADS * HEAD_DIM
    u_cols = in_cols - q_cols - 2 * kv_cols
    mrows = sc.shape[1]
    mtile = 1 if mrows == 1 else tile
    mod_spec = pl.BlockSpec((1, mtile, d), (lambda b, i: (b, 0, 0)) if mrows == 1 else (lambda b, i: (b, i, 0)))
    row_spec = lambda c: pl.BlockSpec((1, tile, c), lambda b, i: (b, i, 0))
    tab_spec = pl.BlockSpec((tile, LANES), lambda b, i: (i, 0))
    last_spec = pl.BlockSpec((1, keep, kv_cols), lambda b, i: (b, 0, 0))
    groups = u_cols // SSM_CH
    tiles = rows // tile
    chunk_spec = pl.BlockSpec((groups, tile // SSM_L, SSM_L * SSM_CH), lambda b, i: (0, b * tiles + i, 0))
    kern = functools.partial(_inproj_kernel, q_cols=q_cols, kv_cols=kv_cols, keep=keep)
    return pl.pallas_call(
        kern,
        grid=(nb, tiles),
        in_specs=[pl.BlockSpec(memory_space=pl.ANY), mod_spec, mod_spec, _const_spec((1, d)),
                  _const_spec((d, in_cols)), _const_spec((1, q_cols)), _const_spec((1, kv_cols)),
                  _const_spec((256, 256)), tab_spec, tab_spec, tab_spec],
        out_specs=[row_spec(q_cols), row_spec(kv_cols), row_spec(kv_cols), chunk_spec, last_spec, last_spec],
        out_shape=[jax.ShapeDtypeStruct((nb, rows, q_cols), BF16),
                   jax.ShapeDtypeStruct((nb, rows, kv_cols), BF16),
                   jax.ShapeDtypeStruct((nb, rows, kv_cols), BF16),
                   jax.ShapeDtypeStruct((groups, nb * rows // SSM_L, SSM_L * SSM_CH), BF16),
                   jax.ShapeDtypeStruct((nb, keep, kv_cols), F32),
                   jax.ShapeDtypeStruct((nb, keep, kv_cols), F32)],
        scratch_shapes=[pltpu.VMEM((u_cols // LANES, tile, LANES), F32), pltpu.VMEM((X_SLOTS, tile, d), F32),
                        pltpu.SemaphoreType.DMA((X_SLOTS,))],
        compiler_params=_params(2),
        name="inproj",
    )(x, sc, sh, ln_g, w_in, gq, gk, ones, cos, s1, s2)


def _ssm_kernel(u_ref, h0_ref, wi_ref, krow_ref, cpow_ref, lam_ref, y_ref, hout_ref,
                toep, s_re, s_im, hp_re, hp_im, *, nseq, nchunk):
    groups = u_ref.shape[0]
    rows = nseq * nchunk
    npair = groups // 2

    @pl.when(pl.program_id(0) == 0)
    def _():
        lane = lax.broadcasted_iota(jnp.int32, (SSM_CH, SSM_L * SSM_CH), 1)

        def expand(g, carry):
            kr = krow_ref[g]
            for t in range(SSM_L):
                blk = kr if t == 0 else jnp.where(lane >= SSM_CH * t, pltpu.roll(kr, SSM_CH * t, 1), 0.0)
                toep[g, SSM_CH * t:SSM_CH * (t + 1), :] = blk.astype(BF16)
            return carry

        lax.fori_loop(0, groups, expand, 0)

    for p in range(npair):
        lhs = jnp.concatenate([u_ref[2 * p], u_ref[2 * p + 1]], axis=1)
        s = _dot(lhs, wi_ref[p])
        s_re[pl.ds(p, rows, stride=SCAN_PITCH), :] = s[:, :LANES]
        s_im[pl.ds(p, rows, stride=SCAN_PITCH), :] = s[:, LANES:]

    a_re = lam_ref[0]
    a_im = lam_ref[1]
    for b in range(nseq):
        def step(n, carry):
            h_re, h_im = carry
            r0 = pl.multiple_of((b * nchunk + n) * SCAN_PITCH, 8)
            hp_re[pl.ds(r0, npair), :] = h_re
            hp_im[pl.ds(r0, npair), :] = h_im
            n_re = a_re * h_re - a_im * h_im + s_re[pl.ds(r0, npair), :]
            n_im = a_re * h_im + a_im * h_re + s_im[pl.ds(r0, npair), :]
            return n_re, n_im
        h_re, h_im = lax.fori_loop(0, nchunk, step, (h0_ref[b, 0], h0_ref[b, 1]), unroll=True)
        hout_ref[b, 0] = h_re
        hout_ref[b, 1] = h_im

    for p in range(npair):
        hp = jnp.concatenate([hp_re[pl.ds(p, rows, stride=SCAN_PITCH), :],
                              hp_im[pl.ds(p, rows, stride=SCAN_PITCH), :]], axis=1).astype(BF16)
        for g in (2 * p, 2 * p + 1):
            y_ref[g] = (_dot(u_ref[g], toep[g]) + _dot(hp, cpow_ref[g])).astype(BF16)


def _ssm_call(u2, h0, wi, krow, cpow, lam, *, nseq, nchunk):
    groups, total_rows, width = u2.shape
    rows = nseq * nchunk
    npair = groups // 2
    kern = functools.partial(_ssm_kernel, nseq=nseq, nchunk=nchunk)
    return pl.pallas_call(
        kern,
        grid=(total_rows // rows,),
        in_specs=[pl.BlockSpec((groups, rows, width), lambda i: (0, i, 0)),
                  pl.BlockSpec((nseq, 2, npair, LANES), lambda i: (i, 0, 0, 0)),
                  _const_spec(wi.shape), _const_spec(krow.shape), _const_spec(cpow.shape), _const_spec(lam.shape)],
        out_specs=[pl.BlockSpec((groups, rows, width), lambda i: (0, i, 0)),
                   pl.BlockSpec((nseq, 2, npair, LANES), lambda i: (i, 0, 0, 0))],
        out_shape=[jax.ShapeDtypeStruct((groups, total_rows, width), BF16),
                   jax.ShapeDtypeStruct(h0.shape, F32)],
        scratch_shapes=[pltpu.VMEM((groups, width, width), BF16)] + [pltpu.VMEM((rows * SCAN_PITCH, LANES), F32)] * 4,
        compiler_params=_params(1),
        name="ssm",
    )(u2, h0, wi, krow, cpow, lam)


def _kv_dup(a):
    lo = lax.broadcasted_iota(jnp.int32, a.shape, 1) < HEAD_DIM
    sw = pltpu.roll(a, HEAD_DIM, 1)
    return jnp.where(lo, a, sw), jnp.where(lo, sw, a)


def _v_ext(v):
    return jnp.concatenate([v, jnp.ones_like(v)], axis=1).astype(BF16)


def _sink_tail(rows):
    row = lax.broadcasted_iota(jnp.int32, (rows, 2 * LANES), 0)
    lane = lax.broadcasted_iota(jnp.int32, (rows, 2 * LANES), 1)
    return jnp.where(jnp.logical_and(row == 0, lane >= LANES), 1.0, 0.0).astype(BF16)


def _sink_attention(sink_ref, problems, nkeys):
    rep = N_HEADS // N_KV_HEADS
    col = lax.broadcasted_iota(jnp.int32, (1, KEYS_PAD), 1)
    real = col < nkeys
    scores = []
    for g, qa, qb, kb, vb, valid, store in problems:
        half = qa.shape[0]
        qst = jnp.concatenate([qa, qb], axis=0)
        lo = lax.broadcasted_iota(jnp.int32, qst.shape, 1) < HEAD_DIM
        zero = jnp.zeros_like(qst)
        q4 = jnp.concatenate([jnp.where(lo, qst, zero), jnp.where(lo, zero, qst)], axis=0)
        s = lax.dot_general(q4, kb, (((1,), (1,)), ((), ())), preferred_element_type=F32)
        keep = real if valid is None else jnp.logical_and(real, valid)
        first = 0 if valid is not None else (nkeys // LANES) * LANES
        blocks = []
        for e in range(2):
            for jj in range(2):
                fill = jnp.where(col == nkeys, sink_ref[rep * g + 2 * jj + e] * LOG2E, -jnp.inf)
                sb = s[(2 * e + jj) * half:(2 * e + jj + 1) * half]
                fixed = jnp.where(keep[:, first:], sb[:, first:], fill[:, first:])
                blocks.append(fixed if first == 0 else jnp.concatenate([sb[:, :first], fixed], axis=1))
        scores.append(jnp.concatenate(blocks, axis=0))
    probs = []
    for s in scores:
        probs.append(jnp.exp2(s - jnp.max(s, axis=-1, keepdims=True)).astype(BF16))
    for p, (g, qa, qb, kb, vb, valid, store) in zip(probs, problems):
        o4 = _dot(p, vb)
        h2 = 2 * qa.shape[0]
        lo = lax.broadcasted_iota(jnp.int32, (h2, LANES), 1) < HEAD_DIM
        store(jnp.where(lo, o4[:h2, :LANES], o4[h2:, :LANES]) / jnp.where(lo, o4[:h2, LANES:], o4[h2:, LANES:]))


def _rotated_q(q_ref, tabs):
    cos_ref, s1_ref, s2_ref = tabs
    return _rope(q_ref[0].astype(F32), cos_ref[...], s1_ref[...], s2_ref[...]).astype(BF16)


def _attn_prompt(q_ref, tabs, kc_ref, kp_ref, vc_ref, vp_ref, sink_ref, kz, vz, attn, tile, later_tile):
    band = CHUNK + WINDOW
    kd = _kv_dup(jnp.concatenate([kp_ref[0], kc_ref[0]], axis=0).astype(F32))
    vd = _kv_dup(jnp.concatenate([vp_ref[0], vc_ref[0]], axis=0).astype(F32))
    tail = _sink_tail(KEYS_PAD - band)
    for g in range(2):
        kz[g, :tile + WINDOW] = kd[g].astype(BF16)
        kz[g, tile + WINDOW:] = jnp.zeros((KEYS_PAD - band, LANES), BF16)
        vext = _v_ext(vd[g])
        for c in range(tile // CHUNK):
            vz[g, c, :band] = vext[c * CHUNK:c * CHUNK + band]
            vz[g, c, band:] = tail
    col_chunk = lax.broadcasted_iota(jnp.int32, (1, KEYS_PAD), 1) // CHUNK
    q = _rotated_q(q_ref, tabs)
    problems = []
    for c in range(tile // CHUNK):
        r0 = c * CHUNK
        valid = None if c >= WINDOW // CHUNK else jnp.logical_or(col_chunk + c >= WINDOW // CHUNK, later_tile)
        for g in range(2):
            c0 = 2 * LANES * g

            def store(o, r0=r0, c0=c0):
                attn[r0:r0 + CHUNK, c0:c0 + LANES] = o[:CHUNK]
                attn[r0:r0 + CHUNK, c0 + LANES:c0 + 2 * LANES] = o[CHUNK:]

            problems.append((g, q[r0:r0 + CHUNK, c0:c0 + LANES], q[r0:r0 + CHUNK, c0 + LANES:c0 + 2 * LANES],
                             kz[g, r0:r0 + KEYS_PAD, :], vz[g, c], valid, store))
    _sink_attention(sink_ref, problems, band)


def _attn_sample(q_ref, tabs, kc_ref, kp_ref, vc_ref, vp_ref, sink_ref, attn, nseq, seq):
    nkeys = WINDOW + seq
    tail = _sink_tail(KEYS_PAD - nkeys)
    q = _rotated_q(q_ref, tabs)
    problems = []
    for b in range(nseq):
        r0 = b * seq
        kd = _kv_dup(jnp.concatenate([kp_ref[b], kc_ref[0, r0:r0 + seq, :]], axis=0).astype(F32))
        vd = _kv_dup(jnp.concatenate([vp_ref[b], vc_ref[0, r0:r0 + seq, :]], axis=0).astype(F32))
        for g in range(2):
            c0 = 2 * LANES * g

            def store(o, r0=r0, c0=c0):
                attn[r0:r0 + seq, c0:c0 + LANES] = o[:seq]
                attn[r0:r0 + seq, c0 + LANES:c0 + 2 * LANES] = o[seq:]

            kb = jnp.concatenate([kd[g].astype(BF16), jnp.zeros((KEYS_PAD - nkeys, LANES), BF16)], axis=0)
            vb = jnp.concatenate([_v_ext(vd[g]), tail], axis=0)
            problems.append((g, q[r0:r0 + seq, c0:c0 + LANES], q[r0:r0 + seq, c0 + LANES:c0 + 2 * LANES],
                             kb, vb, None, store))
    _sink_attention(sink_ref, problems, nkeys)


def _gelu_tanh(x):
    return 0.5 * x * (1.0 + jnp.tanh(math.sqrt(2.0 / math.pi) * (x + 0.044715 * (x * x * x))))


def _mix_body(sink_ref, x_ref, g1_ref, q_ref, kc_ref, kp_ref, vc_ref, vp_ref, y_ref, gluw_ref, glub_ref, ga_ref,
              gs_ref, wout_ref, cos_ref, s1_ref, s2_ref, out, scratch, *, tile, nseq, later_tile=None):
    tabs = (cos_ref, s1_ref, s2_ref)
    if nseq is None:
        yscr, kz, vz, attn = scratch
        _attn_prompt(q_ref, tabs, kc_ref, kp_ref, vc_ref, vp_ref, sink_ref, kz, vz, attn, tile, later_tile)
    else:
        yscr, attn = scratch
        _attn_sample(q_ref, tabs, kc_ref, kp_ref, vc_ref, vp_ref, sink_ref, attn, nseq, tile // nseq)
    nslab = yscr.shape[0]
    for o in range(nslab):
        for hh in range(SSM_L // 8):
            outs = _granule_transpose([y_ref[8 * o + g8, :, LANES * hh:LANES * (hh + 1)] for g8 in range(8)])
            for t8 in range(8):
                yscr[o, pl.ds(8 * hh + t8, tile // SSM_L, stride=SSM_L), :] = outs[t8].astype(F32)
    block = min(tile, ROW_BLOCK)
    for r0 in range(0, tile, block):
        rows = slice(r0, r0 + block)
        an = _rms(attn[rows]) * ga_ref[...]
        gl = _gelu_tanh(jnp.concatenate([yscr[o, rows] for o in range(nslab)], axis=1))
        so = gl * jax.nn.sigmoid(_dot(gl.astype(BF16), gluw_ref[...]) + glub_ref[...])
        sn = _rms(so) * gs_ref[...]
        merged = jnp.concatenate([an, sn], axis=1).astype(BF16)
        g1 = g1_ref[0] if g1_ref.shape[1] == 1 else g1_ref[0, rows]
        out[rows] = x_ref[0, rows] + g1 * _dot(merged, wout_ref[...])


def _mix_kernel(*refs, tile, nseq):
    _mix_body(*refs[:MIX_INPUTS], refs[MIX_INPUTS].at[0], refs[MIX_INPUTS + 1:], tile=tile, nseq=nseq)


def _mix_call(sinks, x, g1, q, k, v, k_past, v_past, y, gluw, glub, ga, gs, wout, tabs, *, tile, nseq):
    nb, rows, d = x.shape
    aw = q.shape[2]
    kvw = k.shape[2]
    groups, _, cw = y.shape
    tiles = rows // tile
    row_spec = lambda c: pl.BlockSpec((1, tile, c), lambda b, i: (b, i, 0))
    chunk_spec = pl.BlockSpec((groups, tile // SSM_L, cw), lambda b, i: (0, b * tiles + i, 0))
    past_spec = pl.BlockSpec((nseq, WINDOW, kvw), lambda b, i: (b * tiles + i, 0, 0))
    tab_spec = pl.BlockSpec((tile, LANES), lambda b, i: (i, 0))
    scratch = [pltpu.VMEM((groups * SSM_CH // LANES, tile, LANES), F32), pltpu.VMEM((tile, aw), F32)]
    kern = functools.partial(_mix_kernel, tile=tile, nseq=nseq)
    return pl.pallas_call(
        kern,
        grid=(nb, tiles),
        in_specs=[pl.BlockSpec(memory_space=pltpu.SMEM),
                  row_spec(d), row_spec(d), row_spec(aw), row_spec(kvw), past_spec, row_spec(kvw), past_spec,
                  chunk_spec, _const_spec(gluw.shape), _const_spec(glub.shape), _const_spec(ga.shape),
                  _const_spec(gs.shape), _const_spec(wout.shape), tab_spec, tab_spec, tab_spec],
        out_specs=row_spec(d),
        out_shape=jax.ShapeDtypeStruct((nb, rows, d), F32),
        scratch_shapes=scratch,
        compiler_params=_params(2),
        name="mix",
    )(sinks, x, g1, q, k, k_past, v, v_past, y, gluw, glub, ga, gs, wout, *tabs)


def _ffn_body(x, sc_ref, sh_ref, g2_ref, ln_ref, wg_ref, wu_ref, wd_ref, out, ff_bounds):
    h = (_rms(x[...]) * (ln_ref[...] * (1.0 + sc_ref[0])) + sh_ref[0]).astype(BF16)
    acc = None
    for c0, c1 in zip(ff_bounds[:-1], ff_bounds[1:]):
        a = _dot(h, wg_ref[:, c0:c1])
        b = _dot(h, wu_ref[:, c0:c1])
        part = _dot((a * jax.nn.sigmoid(a) * b).astype(BF16), wd_ref[c0:c1, :])
        acc = part if acc is None else acc + part
    out[...] = x[...] + g2_ref[0] * acc


def _ffn_kernel(x_ref, sc_ref, sh_ref, g2_ref, ln_ref, wg_ref, wu_ref, wd_ref, o_ref, *, ff_bounds):
    _ffn_body(x_ref.at[0], sc_ref, sh_ref, g2_ref, ln_ref, wg_ref, wu_ref, wd_ref, o_ref.at[0], ff_bounds)


def _ff_bounds(dff):
    split = -(-dff // (2 * MXU_TILE)) * MXU_TILE
    return (0, split, dff) if split < dff else (0, dff)


def _mixffn_kernel(*refs, tile, tiles, ff_bounds):
    mix_in, ffn_in = refs[:MIX_INPUTS], refs[MIX_INPUTS:MIX_INPUTS + FFN_INPUTS]
    o_ref, x1_buf, *mix_scratch = refs[MIX_INPUTS + FFN_INPUTS:]
    step = pl.program_id(0)

    @pl.when(step == 0)
    def _():
        x1_buf[...] = jnp.zeros(x1_buf.shape, F32)

    _ffn_body(x1_buf, *ffn_in, o_ref.at[0], ff_bounds)
    tile_in_seq = jnp.minimum(step, pl.num_programs(0) - 2) % tiles
    _mix_body(*mix_in, x1_buf, mix_scratch, tile=tile, nseq=None, later_tile=tile_in_seq > 0)


def _mixffn_call(sinks, x, g1, q, k, v, y, gluw, glub, ga, gs, wout, tabs, sc, sh, g2, ln_g, wg, wu, wd, *, tile):
    nb, rows, d = x.shape
    aw = q.shape[2]
    kvw = k.shape[2]
    groups, _, cw = y.shape
    tiles = rows // tile
    last = nb * tiles - 1
    wpt = tile // WINDOW

    def mix_bi(s):
        sm = jnp.minimum(s, last)
        return sm // tiles, sm % tiles

    def ffn_bi(s):
        sf = jnp.maximum(s - 1, 0)
        return sf // tiles, sf % tiles

    row_spec = lambda c: pl.BlockSpec((1, tile, c), lambda s: (*mix_bi(s), 0))
    mod_spec = pl.BlockSpec((1, 1, d), lambda s: (mix_bi(s)[0], 0, 0))
    past_spec = pl.BlockSpec((1, WINDOW, kvw),
                             lambda s: (mix_bi(s)[0], jnp.maximum(mix_bi(s)[1] * wpt - 1, 0), 0))
    chunk_spec = pl.BlockSpec((groups, tile // SSM_L, cw), lambda s: (0, jnp.minimum(s, last), 0))
    tab_spec = pl.BlockSpec((tile, LANES), lambda s: (mix_bi(s)[1], 0))
    ffn_mod = pl.BlockSpec((1, 1, d), lambda s: (ffn_bi(s)[0], 0, 0))
    scratch = [pltpu.VMEM((tile, d), F32),
               pltpu.VMEM((groups * SSM_CH // LANES, tile, LANES), F32),
               pltpu.VMEM((2, tile + KEYS_PAD - CHUNK, LANES), BF16),
               pltpu.VMEM((2, tile // CHUNK, KEYS_PAD, 2 * LANES), BF16), pltpu.VMEM((tile, aw), F32)]
    kern = functools.partial(_mixffn_kernel, tile=tile, tiles=tiles, ff_bounds=_ff_bounds(wg.shape[1]))
    return pl.pallas_call(
        kern,
        grid=(nb * tiles + 1,),
        in_specs=[pl.BlockSpec(memory_space=pltpu.SMEM),
                  row_spec(d), mod_spec, row_spec(aw), row_spec(kvw), past_spec, row_spec(kvw), past_spec,
                  chunk_spec, _const_spec(gluw.shape), _const_spec(glub.shape), _const_spec(ga.shape),
                  _const_spec(gs.shape), _const_spec(wout.shape), tab_spec, tab_spec, tab_spec,
                  ffn_mod, ffn_mod, ffn_mod, _const_spec((1, d)),
                  _const_spec(wg.shape), _const_spec(wu.shape), _const_spec(wd.shape)],
        out_specs=pl.BlockSpec((1, tile, d), lambda s: (*ffn_bi(s), 0)),
        out_shape=jax.ShapeDtypeStruct((nb, rows, d), F32),
        scratch_shapes=scratch,
        compiler_params=pltpu.CompilerParams(dimension_semantics=("arbitrary",), vmem_limit_bytes=VMEM_LIMIT_FUSED),
        name="mixffn",
    )(sinks, x, g1, q, k, k, v, v, y, gluw, glub, ga, gs, wout, *tabs, sc, sh, g2, ln_g, wg, wu, wd)


def _ffn_call(x, sc, sh, g2, ln_g, wg, wu, wd, *, tile):
    nb, rows, d = x.shape
    ff_bounds = _ff_bounds(wg.shape[1])
    mrows = sc.shape[1]
    mtile = 1 if mrows == 1 else tile
    mod_spec = pl.BlockSpec((1, mtile, d), (lambda b, i: (b, 0, 0)) if mrows == 1 else (lambda b, i: (b, i, 0)))
    row_spec = pl.BlockSpec((1, tile, d), lambda b, i: (b, i, 0))
    kern = functools.partial(_ffn_kernel, ff_bounds=ff_bounds)
    return pl.pallas_call(
        kern,
        grid=(nb, rows // tile),
        in_specs=[row_spec, mod_spec, mod_spec, mod_spec, _const_spec((1, d)),
                  _const_spec(wg.shape), _const_spec(wu.shape), _const_spec(wd.shape)],
        out_specs=row_spec,
        out_shape=jax.ShapeDtypeStruct((nb, rows, d), F32),
        compiler_params=_params(2),
        name="ffn",
    )(x, sc, sh, g2, ln_g, wg, wu, wd)


def _split_bf16(a):
    hi = a.astype(BF16)
    return hi, (a - hi.astype(F32)).astype(BF16)


def _dot3(a, b):
    ah, al = _split_bf16(a)
    bh, bl = _split_bf16(b)
    return _dot(ah, bh) + _dot(ah, bl) + _dot(al, bh)


def _ssm_prep_kernel(*refs):
    for gi in range(refs[0].shape[0]):
        _ssm_prep_group(*[r.at[pl.ds(gi, 1)] for r in refs])


def _ssm_prep_group(logdt_ref, ar_ref, ai_ref, bt_re_ref, bt_im_ref, ct_re_ref, ct_im_ref, d_ref,
                     krow_ref, y_re_ref, y_im_ref, wi_re_ref, wi_im_ref, lam_ref):
    n, ch = SSM_L, SSM_CH
    dt = jnp.exp(logdt_ref[0])
    ar, ai = ar_ref[0], ai_ref[0]
    zr, zi = ar * dt, ai * dt
    ez = jnp.exp(zr)
    e_re = jnp.tanh(0.5 * zr) * (ez + 1.0) * jnp.cos(zi) - 2.0 * jnp.sin(0.5 * zi) ** 2
    e_im = ez * jnp.sin(zi)
    mag = ar * ar + ai * ai
    coef_re = (e_re * ar + e_im * ai) / mag
    coef_im = (e_im * ar - e_re * ai) / mag
    bt_re, bt_im = bt_re_ref[0], bt_im_ref[0]
    bb_re = coef_re * bt_re - coef_im * bt_im
    bb_im = coef_re * bt_im + coef_im * bt_re
    rows = 2 * n
    jc = lax.broadcasted_iota(jnp.int32, (rows, 1), 0).astype(F32)
    mg = jnp.exp(jc * zr)
    pr, pi = mg * jnp.cos(jc * zi), mg * jnp.sin(jc * zi)
    for t in range(n):
        r_re, r_im = pr[n - 1 - t:n - t], pi[n - 1 - t:n - t]
        wi_re_ref[0, ch * t:ch * (t + 1), :] = bb_re * r_re - bb_im * r_im
        wi_im_ref[0, ch * t:ch * (t + 1), :] = bb_re * r_im + bb_im * r_re
    lam_ref[0, 0:1, :] = pr[n:n + 1]
    lam_ref[0, 1:2, :] = pi[n:n + 1]
    lag = lax.broadcasted_iota(jnp.int32, (rows, n * ch), 1) // ch
    jrow = lax.broadcasted_iota(jnp.int32, (rows, n * ch), 0)
    chan = lax.broadcasted_iota(jnp.int32, (ch, n * ch), 1) % ch
    spread_chan = (chan == lax.broadcasted_iota(jnp.int32, (ch, n * ch), 0)).astype(BF16)
    nn, tn = (((1,), (0,)), ((), ())), (((0,), (0,)), ((), ()))

    def spread(a, onehot, dims):
        hi, lo = _split_bf16(a)
        return (lax.dot_general(hi, onehot, dims, preferred_element_type=F32)
                + lax.dot_general(lo, onehot, dims, preferred_element_type=F32))

    def powers(shift):
        onehot = (lag + shift == jrow).astype(BF16)
        return spread(pr, onehot, tn), spread(pi, onehot, tn)

    c_re = spread(ct_re_ref[0], spread_chan, nn)
    c_im = spread(ct_im_ref[0], spread_chan, nn)
    l1_re, l1_im = powers(1)
    y_re_ref[0] = c_re * l1_re - c_im * l1_im
    y_im_ref[0] = c_re * l1_im + c_im * l1_re
    l0_re, l0_im = powers(0)
    k_re = c_re * l0_re - c_im * l0_im
    k_im = c_re * l0_im + c_im * l0_re
    lane = lax.broadcasted_iota(jnp.int32, (ch, n * ch), 1)
    d_lag0 = jnp.where(lane == lax.broadcasted_iota(jnp.int32, (ch, n * ch), 0), d_ref[0], 0.0)
    krow_ref[0] = _dot3(bb_re, k_re) - _dot3(bb_im, k_im) + d_lag0


def _ssm_weights(a_re, a_im, log_dt, b_re, b_im, c_re, c_im, d_skip):
    groups, state = a_re.shape
    ch = b_re.shape[2]
    n = SSM_L
    row = lambda a: a.reshape(groups, 1, state)
    tr =lambda a: jnp.transpose(a, (0, 2, 1))
    d_pad = jnp.pad(d_skip.reshape(groups, 1, ch), ((0, 0), (0, 0), (0, (n - 1) * ch)))
    per_step = math.gcd(groups, PREP_GROUPS)
    blk = lambda *shape: pl.BlockSpec((per_step,) + shape, lambda g: (g, 0, 0))
    krow, y_re, y_im, wi_re, wi_im, lam = pl.pallas_call(
        _ssm_prep_kernel,
        grid=(groups // per_step,),
        in_specs=[blk(1, 1), blk(1, state), blk(1, state),
                  blk(ch, state), blk(ch, state), blk(state, ch), blk(state, ch), blk(1, n * ch)],
        out_specs=[blk(ch, n * ch), blk(state, n * ch), blk(state, n * ch), blk(n * ch, state),
                   blk(n * ch, state), blk(2, state)],
        out_shape=[jax.ShapeDtypeStruct((groups, ch, n * ch), F32),
                   jax.ShapeDtypeStruct((groups, state, n * ch), F32),
                   jax.ShapeDtypeStruct((groups, state, n * ch), F32),
                   jax.ShapeDtypeStruct((groups, n * ch, state), F32),
                   jax.ShapeDtypeStruct((groups, n * ch, state), F32),
                   jax.ShapeDtypeStruct((groups, 2, state), F32)],
        compiler_params=_params(1),
        name="ssm_prep",
    )(log_dt.reshape(groups, 1, 1), row(a_re), row(a_im), tr(b_re), tr(b_im), tr(c_re), tr(c_im), d_pad)
    pair = lambda a: a.reshape((groups // 2, 2) + a.shape[1:])
    wr, wm = pair(wi_re), pair(wi_im)
    zero = jnp.zeros_like(wr[:, 0])
    top = jnp.concatenate([wr[:, 0], zero, wm[:, 0], zero], axis=2)
    bot = jnp.concatenate([zero, wr[:, 1], zero, wm[:, 1]], axis=2)
    wi_pair = jnp.concatenate([top, bot], axis=1).astype(BF16)
    yr, ym = pair(y_re), pair(y_im)
    zc = jnp.zeros_like(yr[:, 0])
    even = jnp.concatenate([yr[:, 0], zc, -ym[:, 0], zc], axis=1)
    odd = jnp.concatenate([zc, yr[:, 1], zc, -ym[:, 1]], axis=1)
    cpow = jnp.stack([even, odd], axis=1).reshape(groups, 4 * state, n * ch).astype(BF16)
    lam16 = jnp.transpose(lam, (1, 0, 2)).reshape(2, groups // 2, 2 * state)
    return wi_pair, krow, cpow, lam16


def _rope_tables(pos):
    half = HEAD_DIM // 2
    inv = ROPE_THETA ** (-jnp.arange(half, dtype=F32) * 2.0 / HEAD_DIM)
    lane = jnp.arange(LANES)
    ang = pos.astype(F32)[:, None] * inv[lane % half][None, :]
    first = ((lane % HEAD_DIM) < half)[None, :]
    sin = jnp.sin(ang)
    return jnp.cos(ang), jnp.where(first, -sin, 0.0), jnp.where(first, 0.0, sin)


def _state_in(state):
    b, groups, p, _ = state.shape
    return jnp.transpose(state, (0, 3, 1, 2)).reshape(b, 2, groups // 2, 2 * p)


def _state_out(h, groups):
    b = h.shape[0]
    return jnp.transpose(h.reshape(b, 2, groups, -1), (0, 2, 3, 1))


def _stream(x, mods, tabs, past, h0, lw, *, tile, keep, nseq_tile, seqs, seq_len):
    sh1, sc1, g1, sh2, sc2, g2 = mods
    nb, rows, d = x.shape
    groups = lw['groups']
    q, k, v, u2, k_last, v_last = _inproj_call(x, sc1, sh1, lw['ln1'], lw['w_in'], lw['gq'], lw['gk'], lw['ones'],
                                               *tabs, tile=min(rows, INPROJ_TILE), keep=keep)
    y2, h_last = _ssm_call(u2, h0, lw['wi'], lw['krow'], lw['cpow'], lw['lam'],
                           nseq=(1 if nseq_tile is None else nseq_tile), nchunk=seq_len // SSM_L)
    if past is None:
        out = _mixffn_call(lw['sinks'], x, g1, q, k, v, y2, lw['gluw'], lw['glub'], lw['ga'], lw['gs'], lw['w_out'],
                           tabs, sc2, sh2, g2, lw['ln2'], lw['wg'], lw['wu'], lw['wd'], tile=tile)
    else:
        x1 = _mix_call(lw['sinks'], x, g1, q, k, v, *past, y2, lw['gluw'], lw['glub'], lw['ga'], lw['gs'],
                       lw['w_out'], tabs, tile=tile, nseq=nseq_tile)
        out = _ffn_call(x1, sc2, sh2, g2, lw['ln2'], lw['wg'], lw['wu'], lw['wd'], tile=tile)
    return out, k_last, v_last, h_last


def kernel(x_prompt, x_sample, cache_k, cache_v, state_ssm, c_prompt, c_sample, w_ada, b_ada, ln1_g, w_in, q_norm_g, k_norm_g, attn_sinks, ssm_A_re, ssm_A_im, ssm_log_dt, ssm_B_re, ssm_B_im, ssm_C_re, ssm_C_im, ssm_D, ssm_glu_w, ssm_glu_b, attn_out_g, ssm_out_g, w_out, ln2_g, w_gate, w_up, w_down):
    depth = w_ada.shape[0]
    bp, sp, d = x_prompt.shape
    bs, ss, _ = x_sample.shape
    groups = ssm_A_re.shape[1]
    kvw = N_KV_HEADS * HEAD_DIM
    tile_p = min(512, sp)
    seg = jnp.arange(256) // HEAD_DIM
    ones = (seg[:, None] == seg[None, :]).astype(BF16)
    tabs_p = _rope_tables(jnp.arange(sp))
    tabs_s = tuple(jnp.tile(t, (bs, 1)) for t in _rope_tables(PAST_LEN + jnp.arange(ss)))

    yp = x_prompt
    ys = x_sample.reshape(1, bs * ss, d)
    outs = [[] for _ in range(6)]
    for l in range(depth):
        wi, krow, cpow, lam = _ssm_weights(ssm_A_re[l], ssm_A_im[l], ssm_log_dt[l], ssm_B_re[l], ssm_B_im[l],
                                           ssm_C_re[l], ssm_C_im[l], ssm_D[l])
        lw = dict(groups=groups, ln1=ln1_g[l][None], w_in=w_in[l].astype(BF16),
                  gq=jnp.tile(q_norm_g[l], N_HEADS)[None] * (HEAD_DIM ** -0.5 * LOG2E), gk=jnp.tile(k_norm_g[l], N_KV_HEADS)[None], ones=ones,
                  wi=wi, krow=krow, cpow=cpow, lam=lam, sinks=attn_sinks[l],
                  gluw=ssm_glu_w[l].astype(BF16), glub=ssm_glu_b[l][None], ga=attn_out_g[l][None],
                  gs=ssm_out_g[l][None], w_out=w_out[l].astype(BF16), ln2=ln2_g[l][None],
                  wg=w_gate[l].astype(BF16), wu=w_up[l].astype(BF16), wd=w_down[l].astype(BF16))
        mod = _mod_call(jnp.concatenate([c_prompt, c_sample], axis=0), w_ada[l], b_ada[l])
        mods_p = tuple(m[:, None, :] for m in jnp.split(mod[:bp], 6, axis=-1))
        mods_s = tuple(jnp.repeat(m, ss, axis=0)[None] for m in jnp.split(mod[bp:], 6, axis=-1))

        h0_p = jnp.zeros((bp, 2, groups // 2, 2 * SSM_STATE), F32)
        yp, kpl, vpl, hpl = _stream(yp, mods_p, tabs_p, None, h0_p, lw, tile=tile_p, keep=WINDOW,
                                    nseq_tile=None, seqs=bp, seq_len=sp)
        past = (cache_k[l].reshape(bs, WINDOW, kvw).astype(BF16), cache_v[l].reshape(bs, WINDOW, kvw).astype(BF16))
        ys, ksl, vsl, hsl = _stream(ys, mods_s, tabs_s, past, _state_in(state_ssm[l]), lw, tile=bs * ss,
                                    keep=bs * ss, nseq_tile=bs, seqs=bs, seq_len=ss)
        outs[0].append(kpl.reshape(bp, WINDOW, N_KV_HEADS, HEAD_DIM))
        outs[1].append(vpl.reshape(bp, WINDOW, N_KV_HEADS, HEAD_DIM))
        outs[2].append(_state_out(hpl, groups))
        outs[3].append(ksl.reshape(bs, ss, N_KV_HEADS, HEAD_DIM))
        outs[4].append(vsl.reshape(bs, ss, N_KV_HEADS, HEAD_DIM))
        outs[5].append(_state_out(hsl, groups))
    return (yp, ys.reshape(bs, ss, d)) + tuple(jnp.stack(o) for o in outs)
```

```python
import functools
import math

import jax
import jax.numpy as jnp
from jax import lax
from jax.experimental import pallas as pl
from jax.experimental.pallas import tpu as pltpu

F32 = jnp.float32
BF16 = jnp.bfloat16

CHUNK = 64
WINDOW = 128
HEAD_DIM = 64
N_HEADS = 8
N_KV_HEADS = 2
SSM_CH = 16
SSM_STATE = 64
SSM_L = 16
ROPE_THETA = 10000.0
EPS = 1e-6
PAST_LEN = 1024
LANES = 128
MXU_TILE = 256
KEYS_PAD = MXU_TILE
ROW_BLOCK = 256
SCAN_PITCH = 24
PREP_GROUPS = 4
INPROJ_TILE = 2048
U_SLOTS = 3
MIX_INPUTS = 17
FFN_INPUTS = 7
LOG2E = math.log2(math.e)
VMEM_LIMIT = 56 * 1024 * 1024
VMEM_LIMIT_FUSED = 60 * 1024 * 1024


def _const_spec(shape):
    nd = len(shape)
    return pl.BlockSpec(shape, lambda *_: (0,) * nd, pipeline_mode=pl.Buffered(1))


def _params(n_grid):
    return pltpu.CompilerParams(dimension_semantics=("arbitrary",) * n_grid, vmem_limit_bytes=VMEM_LIMIT)


def _rms(x):
    return x * lax.rsqrt(jnp.mean(x * x, axis=-1, keepdims=True) + EPS)


def _dot(a, b):
    return jnp.dot(a, b, preferred_element_type=F32)


def _mod_kernel(c_ref, w_ref, b_ref, o_ref):
    c = c_ref[...]
    s = (c * jax.nn.sigmoid(c)).astype(BF16)
    o_ref[...] = _dot(s, w_ref[...].astype(BF16)) + b_ref[...]


def _mod_call(c, w, b):
    rows, d = c.shape
    cols = w.shape[1]
    tile = 1536
    return pl.pallas_call(
        _mod_kernel,
        grid=(cols // tile,),
        in_specs=[pl.BlockSpec((rows, d), lambda j: (0, 0)),
                  pl.BlockSpec((d, tile), lambda j: (0, j)),
                  pl.BlockSpec((1, tile), lambda j: (0, j))],
        out_specs=pl.BlockSpec((rows, tile), lambda j: (0, j)),
        out_shape=jax.ShapeDtypeStruct((rows, cols), F32),
        compiler_params=_params(1),
        name="mod",
    )(c, w, b.reshape(1, cols))


def _head_rms(t, ones_ref):
    width = t.shape[1]
    sq = t * t
    hi = sq.astype(BF16)
    lo = (sq - hi.astype(F32)).astype(BF16)
    parts = []
    for c0 in range(0, width, 256):
        w = min(256, width - c0)
        ones = ones_ref[:w, :w]
        parts.append(_dot(hi[:, c0:c0 + w], ones) + _dot(lo[:, c0:c0 + w], ones))
    ssq = parts[0] if len(parts) == 1 else jnp.concatenate(parts, axis=1)
    return t * lax.rsqrt(ssq * (1.0 / HEAD_DIM) + EPS)


def _rope(t, cos, s1, s2):
    outs = []
    for c0 in range(0, t.shape[1], LANES):
        xb = t[:, c0:c0 + LANES]
        outs.append(xb * cos + pltpu.roll(xb, LANES - HEAD_DIM // 2, 1) * s1 + pltpu.roll(xb, HEAD_DIM // 2, 1) * s2)
    return outs[0] if len(outs) == 1 else jnp.concatenate(outs, axis=1)


def _granule_transpose(arrs):
    gran = lax.broadcasted_iota(jnp.int32, arrs[0].shape, 1) // SSM_CH
    cur = list(arrs)
    for s in (4, 2, 1):
        upper = (gran & s) != 0
        nxt = list(cur)
        for a0 in range(8):
            if a0 & s:
                continue
            lo, hi = cur[a0], cur[a0 + s]
            nxt[a0] = jnp.where(upper, pltpu.roll(hi, SSM_CH * s, 1), lo)
            nxt[a0 + s] = jnp.where(upper, hi, pltpu.roll(lo, LANES - SSM_CH * s, 1))
        cur = nxt
    return cur


def _inproj_kernel(x_ref, sc_ref, sh_ref, ln_ref, w_ref, gq_ref, gk_ref, ones_ref, cos_ref, s1_ref, s2_ref,
                   q_ref, k_ref, v_ref, u_ref, klast_ref, vlast_ref, uscr, *, q_cols, kv_cols, keep):
    tile = x_ref.shape[1]
    block = min(tile, ROW_BLOCK)
    for r0 in range(0, tile, block):
        rows = slice(r0, r0 + block)
        sc = sc_ref[0] if sc_ref.shape[1] == 1 else sc_ref[0, rows]
        sh = sh_ref[0] if sh_ref.shape[1] == 1 else sh_ref[0, rows]
        h = _rms(x_ref[0, rows]) * (ln_ref[...] * (1.0 + sc)) + sh
        proj = _dot(h.astype(BF16), w_ref[...])
        q = proj[:, :q_cols]
        k = proj[:, q_cols:q_cols + kv_cols]
        v = proj[:, q_cols + kv_cols:q_cols + 2 * kv_cols]
        u = proj[:, q_cols + 2 * kv_cols:]
        kr = _rope(_head_rms(k, ones_ref) * gk_ref[...], cos_ref[rows], s1_ref[rows], s2_ref[rows])
        q_ref[0, rows] = (_head_rms(q, ones_ref) * gq_ref[...]).astype(BF16)
        k_ref[0, rows] = kr.astype(BF16)
        v_ref[0, rows] = v.astype(BF16)
        nchunk = block // SSM_L
        c0 = r0 // SSM_L
        for o in range(u.shape[1] // LANES):
            uscr[o, rows] = u[:, LANES * o:LANES * (o + 1)]
        for o in range(u.shape[1] // LANES):
            for hh in range(SSM_L // 8):
                outs = _granule_transpose([uscr[o, pl.ds(r0 + 8 * hh + t, nchunk, stride=SSM_L), :].astype(BF16)
                                           for t in range(8)])
                for g8 in range(8):
                    u_ref[8 * o + g8, c0:c0 + nchunk, LANES * hh:LANES * (hh + 1)] = outs[g8]
        first = max(r0, tile - keep)
        if first < r0 + block:
            dst = slice(first - (tile - keep), r0 + block - (tile - keep))
            klast_ref[0, dst] = kr[first - r0:, :]
            vlast_ref[0, dst] = v[first - r0:, :]


def _inproj_call(x, sc, sh, ln_g, w_in, gq, gk, ones, cos, s1, s2, *, tile, keep):
    nb, rows, d = x.shape
    in_cols = w_in.shape[1]
    q_cols = N_HEADS * HEAD_DIM
    kv_cols = N_KV_HEADS * HEAD_DIM
    u_cols = in_cols - q_cols - 2 * kv_cols
    mrows = sc.shape[1]
    mtile = 1 if mrows == 1 else tile
    mod_spec = pl.BlockSpec((1, mtile, d), (lambda b, i: (b, 0, 0)) if mrows == 1 else (lambda b, i: (b, i, 0)))
    row_spec = lambda c: pl.BlockSpec((1, tile, c), lambda b, i: (b, i, 0))
    tab_spec = pl.BlockSpec((tile, LANES), lambda b, i: (i, 0))
    last_spec = pl.BlockSpec((1, keep, kv_cols), lambda b, i: (b, 0, 0))
    groups = u_cols // SSM_CH
    tiles = rows // tile
    chunk_spec = pl.BlockSpec((groups, tile // SSM_L, SSM_L * SSM_CH), lambda b, i: (0, b * tiles + i, 0))
    kern = functools.partial(_inproj_kernel, q_cols=q_cols, kv_cols=kv_cols, keep=keep)
    return pl.pallas_call(
        kern,
        grid=(nb, tiles),
        in_specs=[row_spec(d), mod_spec, mod_spec, _const_spec((1, d)), _const_spec((d, in_cols)),
                  _const_spec((1, q_cols)), _const_spec((1, kv_cols)), _const_spec((256, 256)),
                  tab_spec, tab_spec, tab_spec],
        out_specs=[row_spec(q_cols), row_spec(kv_cols), row_spec(kv_cols), chunk_spec, last_spec, last_spec],
        out_shape=[jax.ShapeDtypeStruct((nb, rows, q_cols), BF16),
                   jax.ShapeDtypeStruct((nb, rows, kv_cols), BF16),
                   jax.ShapeDtypeStruct((nb, rows, kv_cols), BF16),
                   jax.ShapeDtypeStruct((groups, nb * rows // SSM_L, SSM_L * SSM_CH), BF16),
                   jax.ShapeDtypeStruct((nb, keep, kv_cols), F32),
                   jax.ShapeDtypeStruct((nb, keep, kv_cols), F32)],
        scratch_shapes=[pltpu.VMEM((u_cols // LANES, tile, LANES), F32)],
        compiler_params=_params(2),
        name="inproj",
    )(x, sc, sh, ln_g, w_in, gq, gk, ones, cos, s1, s2)


def _ssm_kernel(u_hbm, h0_ref, wi_ref, krow_ref, cpow_ref, lam_ref, y_ref, hout_ref,
                toep, s_re, s_im, hp_re, hp_im, uring, usem, *, nseq, nchunk):
    groups = uring.shape[1]
    rows = nseq * nchunk
    npair = groups // 2
    step, nsteps = pl.program_id(0), pl.num_programs(0)

    def u_copy(s):
        slot = s % U_SLOTS
        return pltpu.make_async_copy(u_hbm.at[:, pl.ds(s * rows, rows), :], uring.at[slot], usem.at[slot])

    @pl.when(step == 0)
    def _():
        for s in range(U_SLOTS - 1):
            @pl.when(s < nsteps)
            def _():
                u_copy(s).start()

    @pl.when(step + U_SLOTS - 1 < nsteps)
    def _():
        u_copy(step + U_SLOTS - 1).start()
    u_ref = uring.at[step % U_SLOTS]

    @pl.when(pl.program_id(0) == 0)
    def _():
        lane = lax.broadcasted_iota(jnp.int32, (SSM_CH, SSM_L * SSM_CH), 1)

        def expand(g, carry):
            kr = krow_ref[g]
            for t in range(SSM_L):
                blk = kr if t == 0 else jnp.where(lane >= SSM_CH * t, pltpu.roll(kr, SSM_CH * t, 1), 0.0)
                toep[g, SSM_CH * t:SSM_CH * (t + 1), :] = blk.astype(BF16)
            return carry

        lax.fori_loop(0, groups, expand, 0)

    u_copy(step).wait()
    for p in range(npair):
        lhs = jnp.concatenate([u_ref[2 * p], u_ref[2 * p + 1]], axis=1)
        s = _dot(lhs, wi_ref[p])
        s_re[pl.ds(p, rows, stride=SCAN_PITCH), :] = s[:, :LANES]
        s_im[pl.ds(p, rows, stride=SCAN_PITCH), :] = s[:, LANES:]

    a_re = lam_ref[0]
    a_im = lam_ref[1]
    for b in range(nseq):
        def step(n, carry):
            h_re, h_im = carry
            r0 = pl.multiple_of((b * nchunk + n) * SCAN_PITCH, 8)
            hp_re[pl.ds(r0, npair), :] = h_re
            hp_im[pl.ds(r0, npair), :] = h_im
            n_re = a_re * h_re - a_im * h_im + s_re[pl.ds(r0, npair), :]
            n_im = a_re * h_im + a_im * h_re + s_im[pl.ds(r0, npair), :]
            return n_re, n_im
        h_re, h_im = lax.fori_loop(0, nchunk, step, (h0_ref[b, 0], h0_ref[b, 1]), unroll=True)
        hout_ref[b, 0] = h_re
        hout_ref[b, 1] = h_im

    for p in range(npair):
        hp = jnp.concatenate([hp_re[pl.ds(p, rows, stride=SCAN_PITCH), :],
                              hp_im[pl.ds(p, rows, stride=SCAN_PITCH), :]], axis=1).astype(BF16)
        for g in (2 * p, 2 * p + 1):
            y_ref[g] = (_dot(u_ref[g], toep[g]) + _dot(hp, cpow_ref[g])).astype(BF16)


def _ssm_call(u2, h0, wi, krow, cpow, lam, *, nseq, nchunk):
    groups, total_rows, width = u2.shape
    rows = nseq * nchunk
    npair = groups // 2
    kern = functools.partial(_ssm_kernel, nseq=nseq, nchunk=nchunk)
    return pl.pallas_call(
        kern,
        grid=(total_rows // rows,),
        in_specs=[pl.BlockSpec(memory_space=pl.ANY),
                  pl.BlockSpec((nseq, 2, npair, LANES), lambda i: (i, 0, 0, 0)),
                  _const_spec(wi.shape), _const_spec(krow.shape), _const_spec(cpow.shape), _const_spec(lam.shape)],
        out_specs=[pl.BlockSpec((groups, rows, width), lambda i: (0, i, 0)),
                   pl.BlockSpec((nseq, 2, npair, LANES), lambda i: (i, 0, 0, 0))],
        out_shape=[jax.ShapeDtypeStruct((groups, total_rows, width), BF16),
                   jax.ShapeDtypeStruct(h0.shape, F32)],
        scratch_shapes=([pltpu.VMEM((groups, width, width), BF16)] + [pltpu.VMEM((rows * SCAN_PITCH, LANES), F32)] * 4
                        + [pltpu.VMEM((U_SLOTS, groups, rows, width), BF16), pltpu.SemaphoreType.DMA((U_SLOTS,))]),
        compiler_params=_params(1),
        name="ssm",
    )(u2, h0, wi, krow, cpow, lam)


def _kv_dup(a):
    lo = lax.broadcasted_iota(jnp.int32, a.shape, 1) < HEAD_DIM
    sw = pltpu.roll(a, HEAD_DIM, 1)
    return jnp.where(lo, a, sw), jnp.where(lo, sw, a)


def _v_ext(v):
    return jnp.concatenate([v, jnp.ones_like(v)], axis=1).astype(BF16)


def _sink_tail(rows):
    row = lax.broadcasted_iota(jnp.int32, (rows, 2 * LANES), 0)
    lane = lax.broadcasted_iota(jnp.int32, (rows, 2 * LANES), 1)
    return jnp.where(jnp.logical_and(row == 0, lane >= LANES), 1.0, 0.0).astype(BF16)


def _sink_attention(sink_ref, problems, nkeys):
    rep = N_HEADS // N_KV_HEADS
    col = lax.broadcasted_iota(jnp.int32, (1, KEYS_PAD), 1)
    real = col < nkeys
    scores = []
    for g, qa, qb, kb, vb, valid, store in problems:
        half = qa.shape[0]
        qst = jnp.concatenate([qa, qb], axis=0)
        lo = lax.broadcasted_iota(jnp.int32, qst.shape, 1) < HEAD_DIM
        zero = jnp.zeros_like(qst)
        q4 = jnp.concatenate([jnp.where(lo, qst, zero), jnp.where(lo, zero, qst)], axis=0)
        s = lax.dot_general(q4, kb, (((1,), (1,)), ((), ())), preferred_element_type=F32)
        keep = real if valid is None else jnp.logical_and(real, valid)
        first = 0 if valid is not None else (nkeys // LANES) * LANES
        blocks = []
        for e in range(2):
            for jj in range(2):
                fill = jnp.where(col == nkeys, sink_ref[rep * g + 2 * jj + e] * LOG2E, -jnp.inf)
                sb = s[(2 * e + jj) * half:(2 * e + jj + 1) * half]
                fixed = jnp.where(keep[:, first:], sb[:, first:], fill[:, first:])
                blocks.append(fixed if first == 0 else jnp.concatenate([sb[:, :first], fixed], axis=1))
        scores.append(jnp.concatenate(blocks, axis=0))
    probs = []
    for s in scores:
        probs.append(jnp.exp2(s - jnp.max(s, axis=-1, keepdims=True)).astype(BF16))
    for p, (g, qa, qb, kb, vb, valid, store) in zip(probs, problems):
        o4 = _dot(p, vb)
        h2 = 2 * qa.shape[0]
        lo = lax.broadcasted_iota(jnp.int32, (h2, LANES), 1) < HEAD_DIM
        store(jnp.where(lo, o4[:h2, :LANES], o4[h2:, :LANES]) / jnp.where(lo, o4[:h2, LANES:], o4[h2:, LANES:]))


def _rotated_q(q_ref, tabs):
    cos_ref, s1_ref, s2_ref = tabs
    return _rope(q_ref[0].astype(F32), cos_ref[...], s1_ref[...], s2_ref[...]).astype(BF16)


def _attn_prompt(q_ref, tabs, kc_ref, kp_ref, vc_ref, vp_ref, sink_ref, kz, vz, attn, tile, later_tile):
    band = CHUNK + WINDOW
    kd = _kv_dup(jnp.concatenate([kp_ref[0], kc_ref[0]], axis=0).astype(F32))
    vd = _kv_dup(jnp.concatenate([vp_ref[0], vc_ref[0]], axis=0).astype(F32))
    tail = _sink_tail(KEYS_PAD - band)
    for g in range(2):
        kz[g, :tile + WINDOW] = kd[g].astype(BF16)
        kz[g, tile + WINDOW:] = jnp.zeros((KEYS_PAD - band, LANES), BF16)
        vext = _v_ext(vd[g])
        for c in range(tile // CHUNK):
            vz[g, c, :band] = vext[c * CHUNK:c * CHUNK + band]
            vz[g, c, band:] = tail
    col_chunk = lax.broadcasted_iota(jnp.int32, (1, KEYS_PAD), 1) // CHUNK
    q = _rotated_q(q_ref, tabs)
    problems = []
    for c in range(tile // CHUNK):
        r0 = c * CHUNK
        valid = None if c >= WINDOW // CHUNK else jnp.logical_or(col_chunk + c >= WINDOW // CHUNK, later_tile)
        for g in range(2):
            c0 = 2 * LANES * g

            def store(o, r0=r0, c0=c0):
                attn[r0:r0 + CHUNK, c0:c0 + LANES] = o[:CHUNK]
                attn[r0:r0 + CHUNK, c0 + LANES:c0 + 2 * LANES] = o[CHUNK:]

            problems.append((g, q[r0:r0 + CHUNK, c0:c0 + LANES], q[r0:r0 + CHUNK, c0 + LANES:c0 + 2 * LANES],
                             kz[g, r0:r0 + KEYS_PAD, :], vz[g, c], valid, store))
    _sink_attention(sink_ref, problems, band)


def _attn_sample(q_ref, tabs, kc_ref, kp_ref, vc_ref, vp_ref, sink_ref, attn, nseq, seq):
    nkeys = WINDOW + seq
    tail = _sink_tail(KEYS_PAD - nkeys)
    q = _rotated_q(q_ref, tabs)
    problems = []
    for b in range(nseq):
        r0 = b * seq
        kd = _kv_dup(jnp.concatenate([kp_ref[b], kc_ref[0, r0:r0 + seq, :]], axis=0).astype(F32))
        vd = _kv_dup(jnp.concatenate([vp_ref[b], vc_ref[0, r0:r0 + seq, :]], axis=0).astype(F32))
        for g in range(2):
            c0 = 2 * LANES * g

            def store(o, r0=r0, c0=c0):
                attn[r0:r0 + seq, c0:c0 + LANES] = o[:seq]
                attn[r0:r0 + seq, c0 + LANES:c0 + 2 * LANES] = o[seq:]

            kb = jnp.concatenate([kd[g].astype(BF16), jnp.zeros((KEYS_PAD - nkeys, LANES), BF16)], axis=0)
            vb = jnp.concatenate([_v_ext(vd[g]), tail], axis=0)
            problems.append((g, q[r0:r0 + seq, c0:c0 + LANES], q[r0:r0 + seq, c0 + LANES:c0 + 2 * LANES],
                             kb, vb, None, store))
    _sink_attention(sink_ref, problems, nkeys)


def _gelu_tanh(x):
    return 0.5 * x * (1.0 + jnp.tanh(math.sqrt(2.0 / math.pi) * (x + 0.044715 * (x * x * x))))


def _mix_body(sink_ref, x_ref, g1_ref, q_ref, kc_ref, kp_ref, vc_ref, vp_ref, y_ref, gluw_ref, glub_ref, ga_ref,
              gs_ref, wout_ref, cos_ref, s1_ref, s2_ref, out, scratch, *, tile, nseq, later_tile=None):
    tabs = (cos_ref, s1_ref, s2_ref)
    if nseq is None:
        yscr, kz, vz, attn = scratch
        _attn_prompt(q_ref, tabs, kc_ref, kp_ref, vc_ref, vp_ref, sink_ref, kz, vz, attn, tile, later_tile)
    else:
        yscr, attn = scratch
        _attn_sample(q_ref, tabs, kc_ref, kp_ref, vc_ref, vp_ref, sink_ref, attn, nseq, tile // nseq)
    nslab = yscr.shape[0]
    for o in range(nslab):
        for hh in range(SSM_L // 8):
            outs = _granule_transpose([y_ref[8 * o + g8, :, LANES * hh:LANES * (hh + 1)] for g8 in range(8)])
            for t8 in range(8):
                yscr[o, pl.ds(8 * hh + t8, tile // SSM_L, stride=SSM_L), :] = outs[t8].astype(F32)
    block = min(tile, ROW_BLOCK)
    for r0 in range(0, tile, block):
        rows = slice(r0, r0 + block)
        an = _rms(attn[rows]) * ga_ref[...]
        gl = _gelu_tanh(jnp.concatenate([yscr[o, rows] for o in range(nslab)], axis=1))
        so = gl * jax.nn.sigmoid(_dot(gl.astype(BF16), gluw_ref[...]) + glub_ref[...])
        sn = _rms(so) * gs_ref[...]
        merged = jnp.concatenate([an, sn], axis=1).astype(BF16)
        g1 = g1_ref[0] if g1_ref.shape[1] == 1 else g1_ref[0, rows]
        out[rows] = x_ref[0, rows] + g1 * _dot(merged, wout_ref[...])


def _mix_kernel(*refs, tile, nseq):
    _mix_body(*refs[:MIX_INPUTS], refs[MIX_INPUTS].at[0], refs[MIX_INPUTS + 1:], tile=tile, nseq=nseq)


def _mix_call(sinks, x, g1, q, k, v, k_past, v_past, y, gluw, glub, ga, gs, wout, tabs, *, tile, nseq):
    nb, rows, d = x.shape
    aw = q.shape[2]
    kvw = k.shape[2]
    groups, _, cw = y.shape
    tiles = rows // tile
    row_spec = lambda c: pl.BlockSpec((1, tile, c), lambda b, i: (b, i, 0))
    chunk_spec = pl.BlockSpec((groups, tile // SSM_L, cw), lambda b, i: (0, b * tiles + i, 0))
    past_spec = pl.BlockSpec((nseq, WINDOW, kvw), lambda b, i: (b * tiles + i, 0, 0))
    tab_spec = pl.BlockSpec((tile, LANES), lambda b, i: (i, 0))
    scratch = [pltpu.VMEM((groups * SSM_CH // LANES, tile, LANES), F32), pltpu.VMEM((tile, aw), F32)]
    kern = functools.partial(_mix_kernel, tile=tile, nseq=nseq)
    return pl.pallas_call(
        kern,
        grid=(nb, tiles),
        in_specs=[pl.BlockSpec(memory_space=pltpu.SMEM),
                  row_spec(d), row_spec(d), row_spec(aw), row_spec(kvw), past_spec, row_spec(kvw), past_spec,
                  chunk_spec, _const_spec(gluw.shape), _const_spec(glub.shape), _const_spec(ga.shape),
                  _const_spec(gs.shape), _const_spec(wout.shape), tab_spec, tab_spec, tab_spec],
        out_specs=row_spec(d),
        out_shape=jax.ShapeDtypeStruct((nb, rows, d), F32),
        scratch_shapes=scratch,
        compiler_params=_params(2),
        name="mix",
    )(sinks, x, g1, q, k, k_past, v, v_past, y, gluw, glub, ga, gs, wout, *tabs)


def _ffn_body(x, sc_ref, sh_ref, g2_ref, ln_ref, wg_ref, wu_ref, wd_ref, out, ff_bounds):
    h = (_rms(x[...]) * (ln_ref[...] * (1.0 + sc_ref[0])) + sh_ref[0]).astype(BF16)
    acc = None
    for c0, c1 in zip(ff_bounds[:-1], ff_bounds[1:]):
        a = _dot(h, wg_ref[:, c0:c1])
        b = _dot(h, wu_ref[:, c0:c1])
        part = _dot((a * jax.nn.sigmoid(a) * b).astype(BF16), wd_ref[c0:c1, :])
        acc = part if acc is None else acc + part
    out[...] = x[...] + g2_ref[0] * acc


def _ffn_kernel(x_ref, sc_ref, sh_ref, g2_ref, ln_ref, wg_ref, wu_ref, wd_ref, o_ref, *, ff_bounds):
    _ffn_body(x_ref.at[0], sc_ref, sh_ref, g2_ref, ln_ref, wg_ref, wu_ref, wd_ref, o_ref.at[0], ff_bounds)


def _ff_bounds(dff):
    split = -(-dff // (2 * MXU_TILE)) * MXU_TILE
    return (0, split, dff) if split < dff else (0, dff)


def _mixffn_kernel(*refs, tile, tiles, ff_bounds):
    mix_in, ffn_in = refs[:MIX_INPUTS], refs[MIX_INPUTS:MIX_INPUTS + FFN_INPUTS]
    o_ref, x1_buf, *mix_scratch = refs[MIX_INPUTS + FFN_INPUTS:]
    step = pl.program_id(0)

    @pl.when(step == 0)
    def _():
        x1_buf[...] = jnp.zeros(x1_buf.shape, F32)

    _ffn_body(x1_buf, *ffn_in, o_ref.at[0], ff_bounds)
    tile_in_seq = jnp.minimum(step, pl.num_programs(0) - 2) % tiles
    _mix_body(*mix_in, x1_buf, mix_scratch, tile=tile, nseq=None, later_tile=tile_in_seq > 0)


def _mixffn_call(sinks, x, g1, q, k, v, y, gluw, glub, ga, gs, wout, tabs, sc, sh, g2, ln_g, wg, wu, wd, *, tile):
    nb, rows, d = x.shape
    aw = q.shape[2]
    kvw = k.shape[2]
    groups, _, cw = y.shape
    tiles = rows // tile
    last = nb * tiles - 1
    wpt = tile // WINDOW

    def mix_bi(s):
        sm = jnp.minimum(s, last)
        return sm // tiles, sm % tiles

    def ffn_bi(s):
        sf = jnp.maximum(s - 1, 0)
        return sf // tiles, sf % tiles

    row_spec = lambda c: pl.BlockSpec((1, tile, c), lambda s: (*mix_bi(s), 0))
    mod_spec = pl.BlockSpec((1, 1, d), lambda s: (mix_bi(s)[0], 0, 0))
    past_spec = pl.BlockSpec((1, WINDOW, kvw),
                             lambda s: (mix_bi(s)[0], jnp.maximum(mix_bi(s)[1] * wpt - 1, 0), 0))
    chunk_spec = pl.BlockSpec((groups, tile // SSM_L, cw), lambda s: (0, jnp.minimum(s, last), 0))
    tab_spec = pl.BlockSpec((tile, LANES), lambda s: (mix_bi(s)[1], 0))
    ffn_mod = pl.BlockSpec((1, 1, d), lambda s: (ffn_bi(s)[0], 0, 0))
    scratch = [pltpu.VMEM((tile, d), F32),
               pltpu.VMEM((groups * SSM_CH // LANES, tile, LANES), F32),
               pltpu.VMEM((2, tile + KEYS_PAD - CHUNK, LANES), BF16),
               pltpu.VMEM((2, tile // CHUNK, KEYS_PAD, 2 * LANES), BF16), pltpu.VMEM((tile, aw), F32)]
    kern = functools.partial(_mixffn_kernel, tile=tile, tiles=tiles, ff_bounds=_ff_bounds(wg.shape[1]))
    return pl.pallas_call(
        kern,
        grid=(nb * tiles + 1,),
        in_specs=[pl.BlockSpec(memory_space=pltpu.SMEM),
                  row_spec(d), mod_spec, row_spec(aw), row_spec(kvw), past_spec, row_spec(kvw), past_spec,
                  chunk_spec, _const_spec(gluw.shape), _const_spec(glub.shape), _const_spec(ga.shape),
                  _const_spec(gs.shape), _const_spec(wout.shape), tab_spec, tab_spec, tab_spec,
                  ffn_mod, ffn_mod, ffn_mod, _const_spec((1, d)),
                  _const_spec(wg.shape), _const_spec(wu.shape), _const_spec(wd.shape)],
        out_specs=pl.BlockSpec((1, tile, d), lambda s: (*ffn_bi(s), 0)),
        out_shape=jax.ShapeDtypeStruct((nb, rows, d), F32),
        scratch_shapes=scratch,
        compiler_params=pltpu.CompilerParams(dimension_semantics=("arbitrary",), vmem_limit_bytes=VMEM_LIMIT_FUSED),
        name="mixffn",
    )(sinks, x, g1, q, k, k, v, v, y, gluw, glub, ga, gs, wout, *tabs, sc, sh, g2, ln_g, wg, wu, wd)


def _ffn_call(x, sc, sh, g2, ln_g, wg, wu, wd, *, tile):
    nb, rows, d = x.shape
    ff_bounds = _ff_bounds(wg.shape[1])
    mrows = sc.shape[1]
    mtile = 1 if mrows == 1 else tile
    mod_spec = pl.BlockSpec((1, mtile, d), (lambda b, i: (b, 0, 0)) if mrows == 1 else (lambda b, i: (b, i, 0)))
    row_spec = pl.BlockSpec((1, tile, d), lambda b, i: (b, i, 0))
    kern = functools.partial(_ffn_kernel, ff_bounds=ff_bounds)
    return pl.pallas_call(
        kern,
        grid=(nb, rows // tile),
        in_specs=[row_spec, mod_spec, mod_spec, mod_spec, _const_spec((1, d)),
                  _const_spec(wg.shape), _const_spec(wu.shape), _const_spec(wd.shape)],
        out_specs=row_spec,
        out_shape=jax.ShapeDtypeStruct((nb, rows, d), F32),
        compiler_params=_params(2),
        name="ffn",
    )(x, sc, sh, g2, ln_g, wg, wu, wd)


def _split_bf16(a):
    hi = a.astype(BF16)
    return hi, (a - hi.astype(F32)).astype(BF16)


def _dot3(a, b):
    ah, al = _split_bf16(a)
    bh, bl = _split_bf16(b)
    return _dot(ah, bh) + _dot(ah, bl) + _dot(al, bh)


def _ssm_prep_kernel(*refs):
    for gi in range(refs[0].shape[0]):
        _ssm_prep_group(*[r.at[pl.ds(gi, 1)] for r in refs])


def _ssm_prep_group(logdt_ref, ar_ref, ai_ref, bt_re_ref, bt_im_ref, ct_re_ref, ct_im_ref, d_ref,
                     krow_ref, y_re_ref, y_im_ref, wi_re_ref, wi_im_ref, lam_ref):
    n, ch = SSM_L, SSM_CH
    dt = jnp.exp(logdt_ref[0])
    ar, ai = ar_ref[0], ai_ref[0]
    zr, zi = ar * dt, ai * dt
    ez = jnp.exp(zr)
    e_re = jnp.tanh(0.5 * zr) * (ez + 1.0) * jnp.cos(zi) - 2.0 * jnp.sin(0.5 * zi) ** 2
    e_im = ez * jnp.sin(zi)
    mag = ar * ar + ai * ai
    coef_re = (e_re * ar + e_im * ai) / mag
    coef_im = (e_im * ar - e_re * ai) / mag
    bt_re, bt_im = bt_re_ref[0], bt_im_ref[0]
    bb_re = coef_re * bt_re - coef_im * bt_im
    bb_im = coef_re * bt_im + coef_im * bt_re
    rows = 2 * n
    jc = lax.broadcasted_iota(jnp.int32, (rows, 1), 0).astype(F32)
    mg = jnp.exp(jc * zr)
    pr, pi = mg * jnp.cos(jc * zi), mg * jnp.sin(jc * zi)
    for t in range(n):
        r_re, r_im = pr[n - 1 - t:n - t], pi[n - 1 - t:n - t]
        wi_re_ref[0, ch * t:ch * (t + 1), :] = bb_re * r_re - bb_im * r_im
        wi_im_ref[0, ch * t:ch * (t + 1), :] = bb_re * r_im + bb_im * r_re
    lam_ref[0, 0:1, :] = pr[n:n + 1]
    lam_ref[0, 1:2, :] = pi[n:n + 1]
    lag = lax.broadcasted_iota(jnp.int32, (rows, n * ch), 1) // ch
    jrow = lax.broadcasted_iota(jnp.int32, (rows, n * ch), 0)
    chan = lax.broadcasted_iota(jnp.int32, (ch, n * ch), 1) % ch
    spread_chan = (chan == lax.broadcasted_iota(jnp.int32, (ch, n * ch), 0)).astype(BF16)
    nn, tn = (((1,), (0,)), ((), ())), (((0,), (0,)), ((), ()))

    def spread(a, onehot, dims):
        hi, lo = _split_bf16(a)
        return (lax.dot_general(hi, onehot, dims, preferred_element_type=F32)
                + lax.dot_general(lo, onehot, dims, preferred_element_type=F32))

    def powers(shift):
        onehot = (lag + shift == jrow).astype(BF16)
        return spread(pr, onehot, tn), spread(pi, onehot, tn)

    c_re = spread(ct_re_ref[0], spread_chan, nn)
    c_im = spread(ct_im_ref[0], spread_chan, nn)
    l1_re, l1_im = powers(1)
    y_re_ref[0] = c_re * l1_re - c_im * l1_im
    y_im_ref[0] = c_re * l1_im + c_im * l1_re
    l0_re, l0_im = powers(0)
    k_re = c_re * l0_re - c_im * l0_im
    k_im = c_re * l0_im + c_im * l0_re
    lane = lax.broadcasted_iota(jnp.int32, (ch, n * ch), 1)
    d_lag0 = jnp.where(lane == lax.broadcasted_iota(jnp.int32, (ch, n * ch), 0), d_ref[0], 0.0)
    krow_ref[0] = _dot3(bb_re, k_re) - _dot3(bb_im, k_im) + d_lag0


def _ssm_weights(a_re, a_im, log_dt, b_re, b_im, c_re, c_im, d_skip):
    groups, state = a_re.shape
    ch = b_re.shape[2]
    n = SSM_L
    row = lambda a: a.reshape(groups, 1, state)
    tr =lambda a: jnp.transpose(a, (0, 2, 1))
    d_pad = jnp.pad(d_skip.reshape(groups, 1, ch), ((0, 0), (0, 0), (0, (n - 1) * ch)))
    per_step = math.gcd(groups, PREP_GROUPS)
    blk = lambda *shape: pl.BlockSpec((per_step,) + shape, lambda g: (g, 0, 0))
    krow, y_re, y_im, wi_re, wi_im, lam = pl.pallas_call(
        _ssm_prep_kernel,
        grid=(groups // per_step,),
        in_specs=[blk(1, 1), blk(1, state), blk(1, state),
                  blk(ch, state), blk(ch, state), blk(state, ch), blk(state, ch), blk(1, n * ch)],
        out_specs=[blk(ch, n * ch), blk(state, n * ch), blk(state, n * ch), blk(n * ch, state),
                   blk(n * ch, state), blk(2, state)],
        out_shape=[jax.ShapeDtypeStruct((groups, ch, n * ch), F32),
                   jax.ShapeDtypeStruct((groups, state, n * ch), F32),
                   jax.ShapeDtypeStruct((groups, state, n * ch), F32),
                   jax.ShapeDtypeStruct((groups, n * ch, state), F32),
                   jax.ShapeDtypeStruct((groups, n * ch, state), F32),
                   jax.ShapeDtypeStruct((groups, 2, state), F32)],
        compiler_params=_params(1),
        name="ssm_prep",
    )(log_dt.reshape(groups, 1, 1), row(a_re), row(a_im), tr(b_re), tr(b_im), tr(c_re), tr(c_im), d_pad)
    pair = lambda a: a.reshape((groups // 2, 2) + a.shape[1:])
    wr, wm = pair(wi_re), pair(wi_im)
    zero = jnp.zeros_like(wr[:, 0])
    top = jnp.concatenate([wr[:, 0], zero, wm[:, 0], zero], axis=2)
    bot = jnp.concatenate([zero, wr[:, 1], zero, wm[:, 1]], axis=2)
    wi_pair = jnp.concatenate([top, bot], axis=1).astype(BF16)
    yr, ym = pair(y_re), pair(y_im)
    zc = jnp.zeros_like(yr[:, 0])
    even = jnp.concatenate([yr[:, 0], zc, -ym[:, 0], zc], axis=1)
    odd = jnp.concatenate([zc, yr[:, 1], zc, -ym[:, 1]], axis=1)
    cpow = jnp.stack([even, odd], axis=1).reshape(groups, 4 * state, n * ch).astype(BF16)
    lam16 = jnp.transpose(lam, (1, 0, 2)).reshape(2, groups // 2, 2 * state)
    return wi_pair, krow, cpow, lam16


def _rope_tables(pos):
    half = HEAD_DIM // 2
    inv = ROPE_THETA ** (-jnp.arange(half, dtype=F32) * 2.0 / HEAD_DIM)
    lane = jnp.arange(LANES)
    ang = pos.astype(F32)[:, None] * inv[lane % half][None, :]
    first = ((lane % HEAD_DIM) < half)[None, :]
    sin = jnp.sin(ang)
    return jnp.cos(ang), jnp.where(first, -sin, 0.0), jnp.where(first, 0.0, sin)


def _state_in(state):
    b, groups, p, _ = state.shape
    return jnp.transpose(state, (0, 3, 1, 2)).reshape(b, 2, groups // 2, 2 * p)


def _state_out(h, groups):
    b = h.shape[0]
    return jnp.transpose(h.reshape(b, 2, groups, -1), (0, 2, 3, 1))


def _stream(x, mods, tabs, past, h0, lw, *, tile, keep, nseq_tile, seqs, seq_len):
    sh1, sc1, g1, sh2, sc2, g2 = mods
    nb, rows, d = x.shape
    groups = lw['groups']
    q, k, v, u2, k_last, v_last = _inproj_call(x, sc1, sh1, lw['ln1'], lw['w_in'], lw['gq'], lw['gk'], lw['ones'],
                                               *tabs, tile=min(rows, INPROJ_TILE), keep=keep)
    y2, h_last = _ssm_call(u2, h0, lw['wi'], lw['krow'], lw['cpow'], lw['lam'],
                           nseq=(1 if nseq_tile is None else nseq_tile), nchunk=seq_len // SSM_L)
    if past is None:
        out = _mixffn_call(lw['sinks'], x, g1, q, k, v, y2, lw['gluw'], lw['glub'], lw['ga'], lw['gs'], lw['w_out'],
                           tabs, sc2, sh2, g2, lw['ln2'], lw['wg'], lw['wu'], lw['wd'], tile=tile)
    else:
        x1 = _mix_call(lw['sinks'], x, g1, q, k, v, *past, y2, lw['gluw'], lw['glub'], lw['ga'], lw['gs'],
                       lw['w_out'], tabs, tile=tile, nseq=nseq_tile)
        out = _ffn_call(x1, sc2, sh2, g2, lw['ln2'], lw['wg'], lw['wu'], lw['wd'], tile=tile)
    return out, k_last, v_last, h_last


def kernel(x_prompt, x_sample, cache_k, cache_v, state_ssm, c_prompt, c_sample, w_ada, b_ada, ln1_g, w_in, q_norm_g, k_norm_g, attn_sinks, ssm_A_re, ssm_A_im, ssm_log_dt, ssm_B_re, ssm_B_im, ssm_C_re, ssm_C_im, ssm_D, ssm_glu_w, ssm_glu_b, attn_out_g, ssm_out_g, w_out, ln2_g, w_gate, w_up, w_down):
    depth = w_ada.shape[0]
    bp, sp, d = x_prompt.shape
    bs, ss, _ = x_sample.shape
    groups = ssm_A_re.shape[1]
    kvw = N_KV_HEADS * HEAD_DIM
    tile_p = min(512, sp)
    seg = jnp.arange(256) // HEAD_DIM
    ones = (seg[:, None] == seg[None, :]).astype(BF16)
    tabs_p = _rope_tables(jnp.arange(sp))
    tabs_s = tuple(jnp.tile(t, (bs, 1)) for t in _rope_tables(PAST_LEN + jnp.arange(ss)))

    yp = x_prompt
    ys = x_sample.reshape(1, bs * ss, d)
    outs = [[] for _ in range(6)]
    for l in range(depth):
        wi, krow, cpow, lam = _ssm_weights(ssm_A_re[l], ssm_A_im[l], ssm_log_dt[l], ssm_B_re[l], ssm_B_im[l],
                                           ssm_C_re[l], ssm_C_im[l], ssm_D[l])
        lw = dict(groups=groups, ln1=ln1_g[l][None], w_in=w_in[l].astype(BF16),
                  gq=jnp.tile(q_norm_g[l], N_HEADS)[None] * (HEAD_DIM ** -0.5 * LOG2E), gk=jnp.tile(k_norm_g[l], N_KV_HEADS)[None], ones=ones,
                  wi=wi, krow=krow, cpow=cpow, lam=lam, sinks=attn_sinks[l],
                  gluw=ssm_glu_w[l].astype(BF16), glub=ssm_glu_b[l][None], ga=attn_out_g[l][None],
                  gs=ssm_out_g[l][None], w_out=w_out[l].astype(BF16), ln2=ln2_g[l][None],
                  wg=w_gate[l].astype(BF16), wu=w_up[l].astype(BF16), wd=w_down[l].astype(BF16))
        mod = _mod_call(jnp.concatenate([c_prompt, c_sample], axis=0), w_ada[l], b_ada[l])
        mods_p = tuple(m[:, None, :] for m in jnp.split(mod[:bp], 6, axis=-1))
        mods_s = tuple(jnp.repeat(m, ss, axis=0)[None] for m in jnp.split(mod[bp:], 6, axis=-1))

        h0_p = jnp.zeros((bp, 2, groups // 2, 2 * SSM_STATE), F32)
        yp, kpl, vpl, hpl = _stream(yp, mods_p, tabs_p, None, h0_p, lw, tile=tile_p, keep=WINDOW,
                                    nseq_tile=None, seqs=bp, seq_len=sp)
        past = (cache_k[l].reshape(bs, WINDOW, kvw).astype(BF16), cache_v[l].reshape(bs, WINDOW, kvw).astype(BF16))
        ys, ksl, vsl, hsl = _stream(ys, mods_s, tabs_s, past, _state_in(state_ssm[l]), lw, tile=bs * ss,
                                    keep=bs * ss, nseq_tile=bs, seqs=bs, seq_len=ss)
        outs[0].append(kpl.reshape(bp, WINDOW, N_KV_HEADS, HEAD_DIM))
        outs[1].append(vpl.reshape(bp, WINDOW, N_KV_HEADS, HEAD_DIM))
        outs[2].append(_state_out(hpl, groups))
        outs[3].append(ksl.reshape(bs, ss, N_KV_HEADS, HEAD_DIM))
        outs[4].append(vsl.reshape(bs, ss, N_KV_HEADS, HEAD_DIM))
        outs[5].append(_state_out(hsl, groups))
    return (yp, ys.reshape(bs, ss, d)) + tuple(jnp.stack(o) for o in outs)
```

```python
import functools
import math

import jax
import jax.numpy as jnp
from jax import lax
from jax.experimental import pallas as pl
from jax.experimental.pallas import tpu as pltpu

F32 = jnp.float32
BF16 = jnp.bfloat16

CHUNK = 64
WINDOW = 128
HEAD_DIM = 64
N_HEADS = 8
N_KV_HEADS = 2
SSM_CH = 16
SSM_STATE = 64
SSM_L = 16
ROPE_THETA = 10000.0
EPS = 1e-6
PAST_LEN = 1024
LANES = 128
MXU_TILE = 256
KEYS_PAD = MXU_TILE
ROW_BLOCK = 256
SCAN_PITCH = 24
PREP_GROUPS = 4
INPROJ_TILE = 2048
MIX_INPUTS = 17
FFN_INPUTS = 7
LOG2E = math.log2(math.e)
VMEM_LIMIT = 56 * 1024 * 1024
VMEM_LIMIT_FUSED = 60 * 1024 * 1024


def _const_spec(shape):
    nd = len(shape)
    return pl.BlockSpec(shape, lambda *_: (0,) * nd, pipeline_mode=pl.Buffered(1))


def _params(n_grid):
    return pltpu.CompilerParams(dimension_semantics=("arbitrary",) * n_grid, vmem_limit_bytes=VMEM_LIMIT)


def _rms(x):
    return x * lax.rsqrt(jnp.mean(x * x, axis=-1, keepdims=True) + EPS)


def _dot(a, b):
    return jnp.dot(a, b, preferred_element_type=F32)


def _mod_kernel(c_ref, w_ref, b_ref, o_ref):
    c = c_ref[...]
    s = (c * jax.nn.sigmoid(c)).astype(BF16)
    o_ref[...] = _dot(s, w_ref[...].astype(BF16)) + b_ref[...]


def _mod_call(c, w, b):
    rows, d = c.shape
    cols = w.shape[1]
    tile = 1536
    return pl.pallas_call(
        _mod_kernel,
        grid=(cols // tile,),
        in_specs=[pl.BlockSpec((rows, d), lambda j: (0, 0)),
                  pl.BlockSpec((d, tile), lambda j: (0, j)),
                  pl.BlockSpec((1, tile), lambda j: (0, j))],
        out_specs=pl.BlockSpec((rows, tile), lambda j: (0, j)),
        out_shape=jax.ShapeDtypeStruct((rows, cols), F32),
        compiler_params=_params(1),
        name="mod",
    )(c, w, b.reshape(1, cols))


def _head_rms(t, ones_ref):
    width = t.shape[1]
    sq = t * t
    hi = sq.astype(BF16)
    lo = (sq - hi.astype(F32)).astype(BF16)
    parts = []
    for c0 in range(0, width, 256):
        w = min(256, width - c0)
        ones = ones_ref[:w, :w]
        parts.append(_dot(hi[:, c0:c0 + w], ones) + _dot(lo[:, c0:c0 + w], ones))
    ssq = parts[0] if len(parts) == 1 else jnp.concatenate(parts, axis=1)
    return t * lax.rsqrt(ssq * (1.0 / HEAD_DIM) + EPS)


def _rope(t, cos, s1, s2):
    outs = []
    for c0 in range(0, t.shape[1], LANES):
        xb = t[:, c0:c0 + LANES]
        outs.append(xb * cos + pltpu.roll(xb, LANES - HEAD_DIM // 2, 1) * s1 + pltpu.roll(xb, HEAD_DIM // 2, 1) * s2)
    return outs[0] if len(outs) == 1 else jnp.concatenate(outs, axis=1)


def _granule_transpose(arrs):
    gran = lax.broadcasted_iota(jnp.int32, arrs[0].shape, 1) // SSM_CH
    cur = list(arrs)
    for s in (4, 2, 1):
        upper = (gran & s) != 0
        nxt = list(cur)
        for a0 in range(8):
            if a0 & s:
                continue
            lo, hi = cur[a0], cur[a0 + s]
            nxt[a0] = jnp.where(upper, pltpu.roll(hi, SSM_CH * s, 1), lo)
            nxt[a0 + s] = jnp.where(upper, hi, pltpu.roll(lo, LANES - SSM_CH * s, 1))
        cur = nxt
    return cur


def _inproj_kernel(x_ref, sc_ref, sh_ref, ln_ref, w_ref, gq_ref, gk_ref, ones_ref, cos_ref, s1_ref, s2_ref,
                   q_ref, k_ref, v_ref, u_ref, klast_ref, vlast_ref, uscr, *, q_cols, kv_cols, keep):
    tile = x_ref.shape[1]
    block = min(tile, 2 * ROW_BLOCK)
    for r0 in range(0, tile, block):
        rows = slice(r0, r0 + block)
        sc = sc_ref[0] if sc_ref.shape[1] == 1 else sc_ref[0, rows]
        sh = sh_ref[0] if sh_ref.shape[1] == 1 else sh_ref[0, rows]
        h = _rms(x_ref[0, rows]) * (ln_ref[...] * (1.0 + sc)) + sh
        proj = _dot(h.astype(BF16), w_ref[...])
        q = proj[:, :q_cols]
        k = proj[:, q_cols:q_cols + kv_cols]
        v = proj[:, q_cols + kv_cols:q_cols + 2 * kv_cols]
        u = proj[:, q_cols + 2 * kv_cols:]
        kr = _rope(_head_rms(k, ones_ref) * gk_ref[...], cos_ref[rows], s1_ref[rows], s2_ref[rows])
        q_ref[0, rows] = (_head_rms(q, ones_ref) * gq_ref[...]).astype(BF16)
        k_ref[0, rows] = kr.astype(BF16)
        v_ref[0, rows] = v.astype(BF16)
        nchunk = block // SSM_L
        c0 = r0 // SSM_L
        for o in range(u.shape[1] // LANES):
            uscr[o, rows] = u[:, LANES * o:LANES * (o + 1)]
        for o in range(u.shape[1] // LANES):
            for hh in range(SSM_L // 8):
                outs = _granule_transpose([uscr[o, pl.ds(r0 + 8 * hh + t, nchunk, stride=SSM_L), :].astype(BF16)
                                           for t in range(8)])
                for g8 in range(8):
                    u_ref[8 * o + g8, c0:c0 + nchunk, LANES * hh:LANES * (hh + 1)] = outs[g8]
        first = max(r0, tile - keep)
        if first < r0 + block:
            dst = slice(first - (tile - keep), r0 + block - (tile - keep))
            klast_ref[0, dst] = kr[first - r0:, :]
            vlast_ref[0, dst] = v[first - r0:, :]


def _inproj_call(x, sc, sh, ln_g, w_in, gq, gk, ones, cos, s1, s2, *, tile, keep):
    nb, rows, d = x.shape
    in_cols = w_in.shape[1]
    q_cols = N_HEADS * HEAD_DIM
    kv_cols = N_KV_HEADS * HEAD_DIM
    u_cols = in_cols - q_cols - 2 * kv_cols
    mrows = sc.shape[1]
    mtile = 1 if mrows == 1 else tile
    mod_spec = pl.BlockSpec((1, mtile, d), (lambda b, i: (b, 0, 0)) if mrows == 1 else (lambda b, i: (b, i, 0)))
    row_spec = lambda c: pl.BlockSpec((1, tile, c), lambda b, i: (b, i, 0))
    tab_spec = pl.BlockSpec((tile, LANES), lambda b, i: (i, 0))
    last_spec = pl.BlockSpec((1, keep, kv_cols), lambda b, i: (b, 0, 0))
    groups = u_cols // SSM_CH
    tiles = rows // tile
    chunk_spec = pl.BlockSpec((groups, tile // SSM_L, SSM_L * SSM_CH), lambda b, i: (0, b * tiles + i, 0))
    kern = functools.partial(_inproj_kernel, q_cols=q_cols, kv_cols=kv_cols, keep=keep)
    return pl.pallas_call(
        kern,
        grid=(nb, tiles),
        in_specs=[row_spec(d), mod_spec, mod_spec, _const_spec((1, d)), _const_spec((d, in_cols)),
                  _const_spec((1, q_cols)), _const_spec((1, kv_cols)), _const_spec((256, 256)),
                  tab_spec, tab_spec, tab_spec],
        out_specs=[row_spec(q_cols), row_spec(kv_cols), row_spec(kv_cols), chunk_spec, last_spec, last_spec],
        out_shape=[jax.ShapeDtypeStruct((nb, rows, q_cols), BF16),
                   jax.ShapeDtypeStruct((nb, rows, kv_cols), BF16),
                   jax.ShapeDtypeStruct((nb, rows, kv_cols), BF16),
                   jax.ShapeDtypeStruct((groups, nb * rows // SSM_L, SSM_L * SSM_CH), BF16),
                   jax.ShapeDtypeStruct((nb, keep, kv_cols), F32),
                   jax.ShapeDtypeStruct((nb, keep, kv_cols), F32)],
        scratch_shapes=[pltpu.VMEM((u_cols // LANES, tile, LANES), F32)],
        compiler_params=_params(2),
        name="inproj",
    )(x, sc, sh, ln_g, w_in, gq, gk, ones, cos, s1, s2)


def _ssm_kernel(u_ref, h0_ref, wi_ref, krow_ref, cpow_ref, lam_ref, y_ref, hout_ref,
                toep, s_re, s_im, hp_re, hp_im, *, nseq, nchunk):
    groups = u_ref.shape[0]
    rows = nseq * nchunk
    npair = groups // 2

    @pl.when(pl.program_id(0) == 0)
    def _():
        lane = lax.broadcasted_iota(jnp.int32, (SSM_CH, SSM_L * SSM_CH), 1)

        def expand(g, carry):
            kr = krow_ref[g]
            for t in range(SSM_L):
                blk = kr if t == 0 else jnp.where(lane >= SSM_CH * t, pltpu.roll(kr, SSM_CH * t, 1), 0.0)
                toep[g, SSM_CH * t:SSM_CH * (t + 1), :] = blk.astype(BF16)
            return carry

        lax.fori_loop(0, groups, expand, 0)

    for p in range(npair):
        lhs = jnp.concatenate([u_ref[2 * p], u_ref[2 * p + 1]], axis=1)
        s = _dot(lhs, wi_ref[p])
        s_re[pl.ds(p, rows, stride=SCAN_PITCH), :] = s[:, :LANES]
        s_im[pl.ds(p, rows, stride=SCAN_PITCH), :] = s[:, LANES:]

    a_re = lam_ref[0]
    a_im = lam_ref[1]
    for b in range(nseq):
        def step(n, carry):
            h_re, h_im = carry
            r0 = pl.multiple_of((b * nchunk + n) * SCAN_PITCH, 8)
            hp_re[pl.ds(r0, npair), :] = h_re
            hp_im[pl.ds(r0, npair), :] = h_im
            n_re = a_re * h_re - a_im * h_im + s_re[pl.ds(r0, npair), :]
            n_im = a_re * h_im + a_im * h_re + s_im[pl.ds(r0, npair), :]
            return n_re, n_im
        h_re, h_im = lax.fori_loop(0, nchunk, step, (h0_ref[b, 0], h0_ref[b, 1]), unroll=True)
        hout_ref[b, 0] = h_re
        hout_ref[b, 1] = h_im

    for p in range(npair):
        hp = jnp.concatenate([hp_re[pl.ds(p, rows, stride=SCAN_PITCH), :],
                              hp_im[pl.ds(p, rows, stride=SCAN_PITCH), :]], axis=1).astype(BF16)
        for g in (2 * p, 2 * p + 1):
            y_ref[g] = (_dot(u_ref[g], toep[g]) + _dot(hp, cpow_ref[g])).astype(BF16)


def _ssm_call(u2, h0, wi, krow, cpow, lam, *, nseq, nchunk):
    groups, total_rows, width = u2.shape
    rows = nseq * nchunk
    npair = groups // 2
    kern = functools.partial(_ssm_kernel, nseq=nseq, nchunk=nchunk)
    return pl.pallas_call(
        kern,
        grid=(total_rows // rows,),
        in_specs=[pl.BlockSpec((groups, rows, width), lambda i: (0, i, 0)),
                  pl.BlockSpec((nseq, 2, npair, LANES), lambda i: (i, 0, 0, 0)),
                  _const_spec(wi.shape), _const_spec(krow.shape), _const_spec(cpow.shape), _const_spec(lam.shape)],
        out_specs=[pl.BlockSpec((groups, rows, width), lambda i: (0, i, 0)),
                   pl.BlockSpec((nseq, 2, npair, LANES), lambda i: (i, 0, 0, 0))],
        out_shape=[jax.ShapeDtypeStruct((groups, total_rows, width), BF16),
                   jax.ShapeDtypeStruct(h0.shape, F32)],
        scratch_shapes=[pltpu.VMEM((groups, width, width), BF16)] + [pltpu.VMEM((rows * SCAN_PITCH, LANES), F32)] * 4,
        compiler_params=_params(1),
        name="ssm",
    )(u2, h0, wi, krow, cpow, lam)


def _kv_dup(a):
    lo = lax.broadcasted_iota(jnp.int32, a.shape, 1) < HEAD_DIM
    sw = pltpu.roll(a, HEAD_DIM, 1)
    return jnp.where(lo, a, sw), jnp.where(lo, sw, a)


def _v_ext(v):
    return jnp.concatenate([v, jnp.ones_like(v)], axis=1).astype(BF16)


def _sink_tail(rows):
    row = lax.broadcasted_iota(jnp.int32, (rows, 2 * LANES), 0)
    lane = lax.broadcasted_iota(jnp.int32, (rows, 2 * LANES), 1)
    return jnp.where(jnp.logical_and(row == 0, lane >= LANES), 1.0, 0.0).astype(BF16)


def _sink_attention(sink_ref, problems, nkeys):
    rep = N_HEADS // N_KV_HEADS
    col = lax.broadcasted_iota(jnp.int32, (1, KEYS_PAD), 1)
    real = col < nkeys
    scores = []
    for g, qa, qb, kb, vb, valid, store in problems:
        half = qa.shape[0]
        qst = jnp.concatenate([qa, qb], axis=0)
        lo = lax.broadcasted_iota(jnp.int32, qst.shape, 1) < HEAD_DIM
        zero = jnp.zeros_like(qst)
        q4 = jnp.concatenate([jnp.where(lo, qst, zero), jnp.where(lo, zero, qst)], axis=0)
        s = lax.dot_general(q4, kb, (((1,), (1,)), ((), ())), preferred_element_type=F32)
        keep = real if valid is None else jnp.logical_and(real, valid)
        first = 0 if valid is not None else (nkeys // LANES) * LANES
        blocks = []
        for e in range(2):
            for jj in range(2):
                fill = jnp.where(col == nkeys, sink_ref[rep * g + 2 * jj + e] * LOG2E, -jnp.inf)
                sb = s[(2 * e + jj) * half:(2 * e + jj + 1) * half]
                fixed = jnp.where(keep[:, first:], sb[:, first:], fill[:, first:])
                blocks.append(fixed if first == 0 else jnp.concatenate([sb[:, :first], fixed], axis=1))
        scores.append(jnp.concatenate(blocks, axis=0))
    probs = []
    for s in scores:
        probs.append(jnp.exp2(s - jnp.max(s, axis=-1, keepdims=True)).astype(BF16))
    for p, (g, qa, qb, kb, vb, valid, store) in zip(probs, problems):
        o4 = _dot(p, vb)
        h2 = 2 * qa.shape[0]
        lo = lax.broadcasted_iota(jnp.int32, (h2, LANES), 1) < HEAD_DIM
        store(jnp.where(lo, o4[:h2, :LANES], o4[h2:, :LANES]) / jnp.where(lo, o4[:h2, LANES:], o4[h2:, LANES:]))


def _rotated_q(q_ref, tabs):
    cos_ref, s1_ref, s2_ref = tabs
    return _rope(q_ref[0].astype(F32), cos_ref[...], s1_ref[...], s2_ref[...]).astype(BF16)


def _attn_prompt(q_ref, tabs, kc_ref, kp_ref, vc_ref, vp_ref, sink_ref, kz, vz, attn, tile, later_tile):
    band = CHUNK + WINDOW
    kd = _kv_dup(jnp.concatenate([kp_ref[0], kc_ref[0]], axis=0).astype(F32))
    vd = _kv_dup(jnp.concatenate([vp_ref[0], vc_ref[0]], axis=0).astype(F32))
    tail = _sink_tail(KEYS_PAD - band)
    for g in range(2):
        kz[g, :tile + WINDOW] = kd[g].astype(BF16)
        kz[g, tile + WINDOW:] = jnp.zeros((KEYS_PAD - band, LANES), BF16)
        vext = _v_ext(vd[g])
        for c in range(tile // CHUNK):
            vz[g, c, :band] = vext[c * CHUNK:c * CHUNK + band]
            vz[g, c, band:] = tail
    col_chunk = lax.broadcasted_iota(jnp.int32, (1, KEYS_PAD), 1) // CHUNK
    q = _rotated_q(q_ref, tabs)
    problems = []
    for c in range(tile // CHUNK):
        r0 = c * CHUNK
        valid = None if c >= WINDOW // CHUNK else jnp.logical_or(col_chunk + c >= WINDOW // CHUNK, later_tile)
        for g in range(2):
            c0 = 2 * LANES * g

            def store(o, r0=r0, c0=c0):
                attn[r0:r0 + CHUNK, c0:c0 + LANES] = o[:CHUNK]
                attn[r0:r0 + CHUNK, c0 + LANES:c0 + 2 * LANES] = o[CHUNK:]

            problems.append((g, q[r0:r0 + CHUNK, c0:c0 + LANES], q[r0:r0 + CHUNK, c0 + LANES:c0 + 2 * LANES],
                             kz[g, r0:r0 + KEYS_PAD, :], vz[g, c], valid, store))
    _sink_attention(sink_ref, problems, band)


def _attn_sample(q_ref, tabs, kc_ref, kp_ref, vc_ref, vp_ref, sink_ref, attn, nseq, seq):
    nkeys = WINDOW + seq
    tail = _sink_tail(KEYS_PAD - nkeys)
    q = _rotated_q(q_ref, tabs)
    problems = []
    for b in range(nseq):
        r0 = b * seq
        kd = _kv_dup(jnp.concatenate([kp_ref[b], kc_ref[0, r0:r0 + seq, :]], axis=0).astype(F32))
        vd = _kv_dup(jnp.concatenate([vp_ref[b], vc_ref[0, r0:r0 + seq, :]], axis=0).astype(F32))
        for g in range(2):
            c0 = 2 * LANES * g

            def store(o, r0=r0, c0=c0):
                attn[r0:r0 + seq, c0:c0 + LANES] = o[:seq]
                attn[r0:r0 + seq, c0 + LANES:c0 + 2 * LANES] = o[seq:]

            kb = jnp.concatenate([kd[g].astype(BF16), jnp.zeros((KEYS_PAD - nkeys, LANES), BF16)], axis=0)
            vb = jnp.concatenate([_v_ext(vd[g]), tail], axis=0)
            problems.append((g, q[r0:r0 + seq, c0:c0 + LANES], q[r0:r0 + seq, c0 + LANES:c0 + 2 * LANES],
                             kb, vb, None, store))
    _sink_attention(sink_ref, problems, nkeys)


def _gelu_tanh(x):
    return 0.5 * x * (1.0 + jnp.tanh(math.sqrt(2.0 / math.pi) * (x + 0.044715 * (x * x * x))))


def _mix_body(sink_ref, x_ref, g1_ref, q_ref, kc_ref, kp_ref, vc_ref, vp_ref, y_ref, gluw_ref, glub_ref, ga_ref,
              gs_ref, wout_ref, cos_ref, s1_ref, s2_ref, out, scratch, *, tile, nseq, later_tile=None):
    tabs = (cos_ref, s1_ref, s2_ref)
    if nseq is None:
        yscr, kz, vz, attn = scratch
        _attn_prompt(q_ref, tabs, kc_ref, kp_ref, vc_ref, vp_ref, sink_ref, kz, vz, attn, tile, later_tile)
    else:
        yscr, attn = scratch
        _attn_sample(q_ref, tabs, kc_ref, kp_ref, vc_ref, vp_ref, sink_ref, attn, nseq, tile // nseq)
    nslab = yscr.shape[0]
    for o in range(nslab):
        for hh in range(SSM_L // 8):
            outs = _granule_transpose([y_ref[8 * o + g8, :, LANES * hh:LANES * (hh + 1)] for g8 in range(8)])
            for t8 in range(8):
                yscr[o, pl.ds(8 * hh + t8, tile // SSM_L, stride=SSM_L), :] = outs[t8].astype(F32)
    block = min(tile, ROW_BLOCK)
    for r0 in range(0, tile, block):
        rows = slice(r0, r0 + block)
        an = _rms(attn[rows]) * ga_ref[...]
        gl = _gelu_tanh(jnp.concatenate([yscr[o, rows] for o in range(nslab)], axis=1))
        so = gl * jax.nn.sigmoid(_dot(gl.astype(BF16), gluw_ref[...]) + glub_ref[...])
        sn = _rms(so) * gs_ref[...]
        merged = jnp.concatenate([an, sn], axis=1).astype(BF16)
        g1 = g1_ref[0] if g1_ref.shape[1] == 1 else g1_ref[0, rows]
        out[rows] = x_ref[0, rows] + g1 * _dot(merged, wout_ref[...])


def _mix_kernel(*refs, tile, nseq):
    _mix_body(*refs[:MIX_INPUTS], refs[MIX_INPUTS].at[0], refs[MIX_INPUTS + 1:], tile=tile, nseq=nseq)


def _mix_call(sinks, x, g1, q, k, v, k_past, v_past, y, gluw, glub, ga, gs, wout, tabs, *, tile, nseq):
    nb, rows, d = x.shape
    aw = q.shape[2]
    kvw = k.shape[2]
    groups, _, cw = y.shape
    tiles = rows // tile
    row_spec = lambda c: pl.BlockSpec((1, tile, c), lambda b, i: (b, i, 0))
    chunk_spec = pl.BlockSpec((groups, tile // SSM_L, cw), lambda b, i: (0, b * tiles + i, 0))
    past_spec = pl.BlockSpec((nseq, WINDOW, kvw), lambda b, i: (b * tiles + i, 0, 0))
    tab_spec = pl.BlockSpec((tile, LANES), lambda b, i: (i, 0))
    scratch = [pltpu.VMEM((groups * SSM_CH // LANES, tile, LANES), F32), pltpu.VMEM((tile, aw), F32)]
    kern = functools.partial(_mix_kernel, tile=tile, nseq=nseq)
    return pl.pallas_call(
        kern,
        grid=(nb, tiles),
        in_specs=[pl.BlockSpec(memory_space=pltpu.SMEM),
                  row_spec(d), row_spec(d), row_spec(aw), row_spec(kvw), past_spec, row_spec(kvw), past_spec,
                  chunk_spec, _const_spec(gluw.shape), _const_spec(glub.shape), _const_spec(ga.shape),
                  _const_spec(gs.shape), _const_spec(wout.shape), tab_spec, tab_spec, tab_spec],
        out_specs=row_spec(d),
        out_shape=jax.ShapeDtypeStruct((nb, rows, d), F32),
        scratch_shapes=scratch,
        compiler_params=_params(2),
        name="mix",
    )(sinks, x, g1, q, k, k_past, v, v_past, y, gluw, glub, ga, gs, wout, *tabs)


def _ffn_body(x, sc_ref, sh_ref, g2_ref, ln_ref, wg_ref, wu_ref, wd_ref, out, ff_bounds):
    h = (_rms(x[...]) * (ln_ref[...] * (1.0 + sc_ref[0])) + sh_ref[0]).astype(BF16)
    acc = None
    for c0, c1 in zip(ff_bounds[:-1], ff_bounds[1:]):
        a = _dot(h, wg_ref[:, c0:c1])
        b = _dot(h, wu_ref[:, c0:c1])
        part = _dot((a * jax.nn.sigmoid(a) * b).astype(BF16), wd_ref[c0:c1, :])
        acc = part if acc is None else acc + part
    out[...] = x[...] + g2_ref[0] * acc


def _ffn_kernel(x_ref, sc_ref, sh_ref, g2_ref, ln_ref, wg_ref, wu_ref, wd_ref, o_ref, *, ff_bounds):
    _ffn_body(x_ref.at[0], sc_ref, sh_ref, g2_ref, ln_ref, wg_ref, wu_ref, wd_ref, o_ref.at[0], ff_bounds)


def _ff_bounds(dff):
    split = -(-dff // (2 * MXU_TILE)) * MXU_TILE
    return (0, split, dff) if split < dff else (0, dff)


def _mixffn_kernel(*refs, tile, tiles, ff_bounds):
    mix_in, ffn_in = refs[:MIX_INPUTS], refs[MIX_INPUTS:MIX_INPUTS + FFN_INPUTS]
    o_ref, x1_buf, *mix_scratch = refs[MIX_INPUTS + FFN_INPUTS:]
    step = pl.program_id(0)

    @pl.when(step == 0)
    def _():
        x1_buf[...] = jnp.zeros(x1_buf.shape, F32)

    _ffn_body(x1_buf, *ffn_in, o_ref.at[0], ff_bounds)
    tile_in_seq = jnp.minimum(step, pl.num_programs(0) - 2) % tiles
    _mix_body(*mix_in, x1_buf, mix_scratch, tile=tile, nseq=None, later_tile=tile_in_seq > 0)


def _mixffn_call(sinks, x, g1, q, k, v, y, gluw, glub, ga, gs, wout, tabs, sc, sh, g2, ln_g, wg, wu, wd, *, tile):
    nb, rows, d = x.shape
    aw = q.shape[2]
    kvw = k.shape[2]
    groups, _, cw = y.shape
    tiles = rows // tile
    last = nb * tiles - 1
    wpt = tile // WINDOW

    def mix_bi(s):
        sm = jnp.minimum(s, last)
        return sm // tiles, sm % tiles

    def ffn_bi(s):
        sf = jnp.maximum(s - 1, 0)
        return sf // tiles, sf % tiles

    row_spec = lambda c: pl.BlockSpec((1, tile, c), lambda s: (*mix_bi(s), 0))
    mod_spec = pl.BlockSpec((1, 1, d), lambda s: (mix_bi(s)[0], 0, 0))
    past_spec = pl.BlockSpec((1, WINDOW, kvw),
                             lambda s: (mix_bi(s)[0], jnp.maximum(mix_bi(s)[1] * wpt - 1, 0), 0))
    chunk_spec = pl.BlockSpec((groups, tile // SSM_L, cw), lambda s: (0, jnp.minimum(s, last), 0))
    tab_spec = pl.BlockSpec((tile, LANES), lambda s: (mix_bi(s)[1], 0))
    ffn_mod = pl.BlockSpec((1, 1, d), lambda s: (ffn_bi(s)[0], 0, 0))
    scratch = [pltpu.VMEM((tile, d), F32),
               pltpu.VMEM((groups * SSM_CH // LANES, tile, LANES), F32),
               pltpu.VMEM((2, tile + KEYS_PAD - CHUNK, LANES), BF16),
               pltpu.VMEM((2, tile // CHUNK, KEYS_PAD, 2 * LANES), BF16), pltpu.VMEM((tile, aw), F32)]
    kern = functools.partial(_mixffn_kernel, tile=tile, tiles=tiles, ff_bounds=_ff_bounds(wg.shape[1]))
    return pl.pallas_call(
        kern,
        grid=(nb * tiles + 1,),
        in_specs=[pl.BlockSpec(memory_space=pltpu.SMEM),
                  row_spec(d), mod_spec, row_spec(aw), row_spec(kvw), past_spec, row_spec(kvw), past_spec,
                  chunk_spec, _const_spec(gluw.shape), _const_spec(glub.shape), _const_spec(ga.shape),
                  _const_spec(gs.shape), _const_spec(wout.shape), tab_spec, tab_spec, tab_spec,
                  ffn_mod, ffn_mod, ffn_mod, _const_spec((1, d)),
                  _const_spec(wg.shape), _const_spec(wu.shape), _const_spec(wd.shape)],
        out_specs=pl.BlockSpec((1, tile, d), lambda s: (*ffn_bi(s), 0)),
        out_shape=jax.ShapeDtypeStruct((nb, rows, d), F32),
        scratch_shapes=scratch,
        compiler_params=pltpu.CompilerParams(dimension_semantics=("arbitrary",), vmem_limit_bytes=VMEM_LIMIT_FUSED),
        name="mixffn",
    )(sinks, x, g1, q, k, k, v, v, y, gluw, glub, ga, gs, wout, *tabs, sc, sh, g2, ln_g, wg, wu, wd)


def _ffn_call(x, sc, sh, g2, ln_g, wg, wu, wd, *, tile):
    nb, rows, d = x.shape
    ff_bounds = _ff_bounds(wg.shape[1])
    mrows = sc.shape[1]
    mtile = 1 if mrows == 1 else tile
    mod_spec = pl.BlockSpec((1, mtile, d), (lambda b, i: (b, 0, 0)) if mrows == 1 else (lambda b, i: (b, i, 0)))
    row_spec = pl.BlockSpec((1, tile, d), lambda b, i: (b, i, 0))
    kern = functools.partial(_ffn_kernel, ff_bounds=ff_bounds)
    return pl.pallas_call(
        kern,
        grid=(nb, rows // tile),
        in_specs=[row_spec, mod_spec, mod_spec, mod_spec, _const_spec((1, d)),
                  _const_spec(wg.shape), _const_spec(wu.shape), _const_spec(wd.shape)],
        out_specs=row_spec,
        out_shape=jax.ShapeDtypeStruct((nb, rows, d), F32),
        compiler_params=_params(2),
        name="ffn",
    )(x, sc, sh, g2, ln_g, wg, wu, wd)


def _split_bf16(a):
    hi = a.astype(BF16)
    return hi, (a - hi.astype(F32)).astype(BF16)


def _dot3(a, b):
    ah, al = _split_bf16(a)
    bh, bl = _split_bf16(b)
    return _dot(ah, bh) + _dot(ah, bl) + _dot(al, bh)


def _ssm_prep_kernel(*refs):
    for gi in range(refs[0].shape[0]):
        _ssm_prep_group(*[r.at[pl.ds(gi, 1)] for r in refs])


def _ssm_prep_group(logdt_ref, ar_ref, ai_ref, bt_re_ref, bt_im_ref, ct_re_ref, ct_im_ref, d_ref,
                     krow_ref, y_re_ref, y_im_ref, wi_re_ref, wi_im_ref, lam_ref):
    n, ch = SSM_L, SSM_CH
    dt = jnp.exp(logdt_ref[0])
    ar, ai = ar_ref[0], ai_ref[0]
    zr, zi = ar * dt, ai * dt
    ez = jnp.exp(zr)
    e_re = jnp.tanh(0.5 * zr) * (ez + 1.0) * jnp.cos(zi) - 2.0 * jnp.sin(0.5 * zi) ** 2
    e_im = ez * jnp.sin(zi)
    mag = ar * ar + ai * ai
    coef_re = (e_re * ar + e_im * ai) / mag
    coef_im = (e_im * ar - e_re * ai) / mag
    bt_re, bt_im = bt_re_ref[0], bt_im_ref[0]
    bb_re = coef_re * bt_re - coef_im * bt_im
    bb_im = coef_re * bt_im + coef_im * bt_re
    rows = 2 * n
    jc = lax.broadcasted_iota(jnp.int32, (rows, 1), 0).astype(F32)
    mg = jnp.exp(jc * zr)
    pr, pi = mg * jnp.cos(jc * zi), mg * jnp.sin(jc * zi)
    for t in range(n):
        r_re, r_im = pr[n - 1 - t:n - t], pi[n - 1 - t:n - t]
        wi_re_ref[0, ch * t:ch * (t + 1), :] = bb_re * r_re - bb_im * r_im
        wi_im_ref[0, ch * t:ch * (t + 1), :] = bb_re * r_im + bb_im * r_re
    lam_ref[0, 0:1, :] = pr[n:n + 1]
    lam_ref[0, 1:2, :] = pi[n:n + 1]
    lag = lax.broadcasted_iota(jnp.int32, (rows, n * ch), 1) // ch
    jrow = lax.broadcasted_iota(jnp.int32, (rows, n * ch), 0)
    chan = lax.broadcasted_iota(jnp.int32, (ch, n * ch), 1) % ch
    spread_chan = (chan == lax.broadcasted_iota(jnp.int32, (ch, n * ch), 0)).astype(BF16)
    nn, tn = (((1,), (0,)), ((), ())), (((0,), (0,)), ((), ()))

    def spread(a, onehot, dims):
        hi, lo = _split_bf16(a)
        return (lax.dot_general(hi, onehot, dims, preferred_element_type=F32)
                + lax.dot_general(lo, onehot, dims, preferred_element_type=F32))

    def powers(shift):
        onehot = (lag + shift == jrow).astype(BF16)
        return spread(pr, onehot, tn), spread(pi, onehot, tn)

    c_re = spread(ct_re_ref[0], spread_chan, nn)
    c_im = spread(ct_im_ref[0], spread_chan, nn)
    l1_re, l1_im = powers(1)
    y_re_ref[0] = c_re * l1_re - c_im * l1_im
    y_im_ref[0] = c_re * l1_im + c_im * l1_re
    l0_re, l0_im = powers(0)
    k_re = c_re * l0_re - c_im * l0_im
    k_im = c_re * l0_im + c_im * l0_re
    lane = lax.broadcasted_iota(jnp.int32, (ch, n * ch), 1)
    d_lag0 = jnp.where(lane == lax.broadcasted_iota(jnp.int32, (ch, n * ch), 0), d_ref[0], 0.0)
    krow_ref[0] = _dot3(bb_re, k_re) - _dot3(bb_im, k_im) + d_lag0


def _ssm_weights(a_re, a_im, log_dt, b_re, b_im, c_re, c_im, d_skip):
    groups, state = a_re.shape
    ch = b_re.shape[2]
    n = SSM_L
    row = lambda a: a.reshape(groups, 1, state)
    tr =lambda a: jnp.transpose(a, (0, 2, 1))
    d_pad = jnp.pad(d_skip.reshape(groups, 1, ch), ((0, 0), (0, 0), (0, (n - 1) * ch)))
    per_step = math.gcd(groups, PREP_GROUPS)
    blk = lambda *shape: pl.BlockSpec((per_step,) + shape, lambda g: (g, 0, 0))
    krow, y_re, y_im, wi_re, wi_im, lam = pl.pallas_call(
        _ssm_prep_kernel,
        grid=(groups // per_step,),
        in_specs=[blk(1, 1), blk(1, state), blk(1, state),
                  blk(ch, state), blk(ch, state), blk(state, ch), blk(state, ch), blk(1, n * ch)],
        out_specs=[blk(ch, n * ch), blk(state, n * ch), blk(state, n * ch), blk(n * ch, state),
                   blk(n * ch, state), blk(2, state)],
        out_shape=[jax.ShapeDtypeStruct((groups, ch, n * ch), F32),
                   jax.ShapeDtypeStruct((groups, state, n * ch), F32),
                   jax.ShapeDtypeStruct((groups, state, n * ch), F32),
                   jax.ShapeDtypeStruct((groups, n * ch, state), F32),
                   jax.ShapeDtypeStruct((groups, n * ch, state), F32),
                   jax.ShapeDtypeStruct((groups, 2, state), F32)],
        compiler_params=_params(1),
        name="ssm_prep",
    )(log_dt.reshape(groups, 1, 1), row(a_re), row(a_im), tr(b_re), tr(b_im), tr(c_re), tr(c_im), d_pad)
    pair = lambda a: a.reshape((groups // 2, 2) + a.shape[1:])
    wr, wm = pair(wi_re), pair(wi_im)
    zero = jnp.zeros_like(wr[:, 0])
    top = jnp.concatenate([wr[:, 0], zero, wm[:, 0], zero], axis=2)
    bot = jnp.concatenate([zero, wr[:, 1], zero, wm[:, 1]], axis=2)
    wi_pair = jnp.concatenate([top, bot], axis=1).astype(BF16)
    yr, ym = pair(y_re), pair(y_im)
    zc = jnp.zeros_like(yr[:, 0])
    even = jnp.concatenate([yr[:, 0], zc, -ym[:, 0], zc], axis=1)
    odd = jnp.concatenate([zc, yr[:, 1], zc, -ym[:, 1]], axis=1)
    cpow = jnp.stack([even, odd], axis=1).reshape(groups, 4 * state, n * ch).astype(BF16)
    lam16 = jnp.transpose(lam, (1, 0, 2)).reshape(2, groups // 2, 2 * state)
    return wi_pair, krow, cpow, lam16


def _rope_tables(pos):
    half = HEAD_DIM // 2
    inv = ROPE_THETA ** (-jnp.arange(half, dtype=F32) * 2.0 / HEAD_DIM)
    lane = jnp.arange(LANES)
    ang = pos.astype(F32)[:, None] * inv[lane % half][None, :]
    first = ((lane % HEAD_DIM) < half)[None, :]
    sin = jnp.sin(ang)
    return jnp.cos(ang), jnp.where(first, -sin, 0.0), jnp.where(first, 0.0, sin)


def _state_in(state):
    b, groups, p, _ = state.shape
    return jnp.transpose(state, (0, 3, 1, 2)).reshape(b, 2, groups // 2, 2 * p)


def _state_out(h, groups):
    b = h.shape[0]
    return jnp.transpose(h.reshape(b, 2, groups, -1), (0, 2, 3, 1))


def _stream(x, mods, tabs, past, h0, lw, *, tile, keep, nseq_tile, seqs, seq_len):
    sh1, sc1, g1, sh2, sc2, g2 = mods
    nb, rows, d = x.shape
    groups = lw['groups']
    q, k, v, u2, k_last, v_last = _inproj_call(x, sc1, sh1, lw['ln1'], lw['w_in'], lw['gq'], lw['gk'], lw['ones'],
                                               *tabs, tile=min(rows, INPROJ_TILE), keep=keep)
    y2, h_last = _ssm_call(u2, h0, lw['wi'], lw['krow'], lw['cpow'], lw['lam'],
                           nseq=(1 if nseq_tile is None else nseq_tile), nchunk=seq_len // SSM_L)
    if past is None:
        out = _mixffn_call(lw['sinks'], x, g1, q, k, v, y2, lw['gluw'], lw['glub'], lw['ga'], lw['gs'], lw['w_out'],
                           tabs, sc2, sh2, g2, lw['ln2'], lw['wg'], lw['wu'], lw['wd'], tile=tile)
    else:
        x1 = _mix_call(lw['sinks'], x, g1, q, k, v, *past, y2, lw['gluw'], lw['glub'], lw['ga'], lw['gs'],
                       lw['w_out'], tabs, tile=tile, nseq=nseq_tile)
        out = _ffn_call(x1, sc2, sh2, g2, lw['ln2'], lw['wg'], lw['wu'], lw['wd'], tile=tile)
    return out, k_last, v_last, h_last


def kernel(x_prompt, x_sample, cache_k, cache_v, state_ssm, c_prompt, c_sample, w_ada, b_ada, ln1_g, w_in, q_norm_g, k_norm_g, attn_sinks, ssm_A_re, ssm_A_im, ssm_log_dt, ssm_B_re, ssm_B_im, ssm_C_re, ssm_C_im, ssm_D, ssm_glu_w, ssm_glu_b, attn_out_g, ssm_out_g, w_out, ln2_g, w_gate, w_up, w_down):
    depth = w_ada.shape[0]
    bp, sp, d = x_prompt.shape
    bs, ss, _ = x_sample.shape
    groups = ssm_A_re.shape[1]
    kvw = N_KV_HEADS * HEAD_DIM
    tile_p = min(512, sp)
    seg = jnp.arange(256) // HEAD_DIM
    ones = (seg[:, None] == seg[None, :]).astype(BF16)
    tabs_p = _rope_tables(jnp.arange(sp))
    tabs_s = tuple(jnp.tile(t, (bs, 1)) for t in _rope_tables(PAST_LEN + jnp.arange(ss)))

    yp = x_prompt
    ys = x_sample.reshape(1, bs * ss, d)
    outs = [[] for _ in range(6)]
    for l in range(depth):
        wi, krow, cpow, lam = _ssm_weights(ssm_A_re[l], ssm_A_im[l], ssm_log_dt[l], ssm_B_re[l], ssm_B_im[l],
                                           ssm_C_re[l], ssm_C_im[l], ssm_D[l])
        lw = dict(groups=groups, ln1=ln1_g[l][None], w_in=w_in[l].astype(BF16),
                  gq=jnp.tile(q_norm_g[l], N_HEADS)[None] * (HEAD_DIM ** -0.5 * LOG2E), gk=jnp.tile(k_norm_g[l], N_KV_HEADS)[None], ones=ones,
                  wi=wi, krow=krow, cpow=cpow, lam=lam, sinks=attn_sinks[l],
                  gluw=ssm_glu_w[l].astype(BF16), glub=ssm_glu_b[l][None], ga=attn_out_g[l][None],
                  gs=ssm_out_g[l][None], w_out=w_out[l].astype(BF16), ln2=ln2_g[l][None],
                  wg=w_gate[l].astype(BF16), wu=w_up[l].astype(BF16), wd=w_down[l].astype(BF16))
        mod = _mod_call(jnp.concatenate([c_prompt, c_sample], axis=0), w_ada[l], b_ada[l])
        mods_p = tuple(m[:, None, :] for m in jnp.split(mod[:bp], 6, axis=-1))
        mods_s = tuple(jnp.repeat(m, ss, axis=0)[None] for m in jnp.split(mod[bp:], 6, axis=-1))

        h0_p = jnp.zeros((bp, 2, groups // 2, 2 * SSM_STATE), F32)
        yp, kpl, vpl, hpl = _stream(yp, mods_p, tabs_p, None, h0_p, lw, tile=tile_p, keep=WINDOW,
                                    nseq_tile=None, seqs=bp, seq_len=sp)
        past = (cache_k[l].reshape(bs, WINDOW, kvw).astype(BF16), cache_v[l].reshape(bs, WINDOW, kvw).astype(BF16))
        ys, ksl, vsl, hsl = _stream(ys, mods_s, tabs_s, past, _state_in(state_ssm[l]), lw, tile=bs * ss,
                                    keep=bs * ss, nseq_tile=bs, seqs=bs, seq_len=ss)
        outs[0].append(kpl.reshape(bp, WINDOW, N_KV_HEADS, HEAD_DIM))
        outs[1].append(vpl.reshape(bp, WINDOW, N_KV_HEADS, HEAD_DIM))
        outs[2].append(_state_out(hpl, groups))
        outs[3].append(ksl.reshape(bs, ss, N_KV_HEADS, HEAD_DIM))
        outs[4].append(vsl.reshape(bs, ss, N_KV_HEADS, HEAD_DIM))
        outs[5].append(_state_out(hsl, groups))
    return (yp, ys.reshape(bs, ss, d)) + tuple(jnp.stack(o) for o in outs)
```
